```python
import math, functools
import jax, jax.numpy as jnp
from jax import lax
import numpy as np

D_MODEL = 1024
BATCH = 32
SEQ = 256
DEPTH = 2
DEC_BATCH = 4
DEC_SEQ = 2048
PAST_LEN = 256

GRID_W = 64
CONV_W = 4
CONV_LEFT = 2
D_LRU = D_MODEL
LRU_BLOCK_W = 64
LRU_BLOCKS = D_LRU // LRU_BLOCK_W
LRU_C = 8.0
D_SSD = D_MODEL
SSD_HEAD_DIM = 64
SSD_HEADS = D_SSD // SSD_HEAD_DIM
SSD_GROUPS = 4
SSD_STATE = 64
SSD_CONV_DIM = D_SSD + 2 * SSD_GROUPS * SSD_STATE
SSD_CHUNK = 64
GLA_HEADS = 4
GLA_DK = D_MODEL // 2
GLA_DV = D_MODEL
GLA_HEAD_K = GLA_DK // GLA_HEADS
GLA_HEAD_V = GLA_DV // GLA_HEADS
GLA_GATE_RANK = 16
GLA_GATE_NORM = 16.0
GLA_CHUNK = 32
N_BRANCH = 3
D_FF = ((8 * D_MODEL // 3 + 255) // 256) * 256
IN_SPLITS = (D_LRU, D_LRU, D_SSD, SSD_CONV_DIM, SSD_HEADS, SSD_HEADS,
             GLA_DK, GLA_DK, GLA_DV, GLA_DV, GLA_GATE_RANK, GLA_GATE_RANK, N_BRANCH * D_MODEL)
IN_WIDTH = sum(IN_SPLITS)
EPS = 1e-6

kernel_name = "bidir_hybrid_lru_ssd_gla_diffusion_step"


def rms_norm(x, g):
    xf = x.astype(jnp.float32)
    y = xf * lax.rsqrt(jnp.mean(xf * xf, axis=-1, keepdims=True) + EPS)
    return (y * g).astype(x.dtype)


def snake(x):
    bsz, t, d = x.shape
    rows = t // GRID_W
    grid = x.reshape(bsz, rows, GRID_W, d)
    odd = (jnp.arange(rows) % 2 == 1)[None, :, None, None]
    return jnp.where(odd, jnp.flip(grid, 2), grid).reshape(bsz, t, d)


def conv_centered(x, w, b):
    t = x.shape[1]
    xp = jnp.pad(x, ((0, 0), (CONV_LEFT, CONV_W - 1 - CONV_LEFT), (0, 0)))
    return sum(xp[:, j:j + t] * w[j] for j in range(CONV_W)) + b


def linear_recurrence(a, b, h0):
    a = a.astype(jnp.float32)
    b = b.astype(jnp.float32)
    def comb(l, r):
        return (l[0] * r[0], r[0] * l[1] + r[1])
    a_cum, b_cum = lax.associative_scan(comb, (a, b), axis=1)
    h = a_cum * h0.astype(jnp.float32)[:, None, :] + b_cum
    return h, h[:, -1]


def chunked_gated_scan(q, k, v, log_a, h0, chunk):
    f32 = jnp.float32
    bsz, t, nh, _ = q.shape
    n = t // chunk
    def to_chunks(a):
        return a.astype(f32).reshape(bsz, n, chunk, nh, a.shape[-1]).transpose(1, 0, 3, 2, 4)
    qc, kc, vc, lc = (to_chunks(a) for a in (q, k, v, log_a))
    causal = jnp.tril(jnp.ones((chunk, chunk), bool))[:, :, None]
    per_key = log_a.shape[-1] != 1

    def step(s, inp):
        qi, ki, vi, li = inp
        b = jnp.cumsum(li, axis=2)
        seg = jnp.exp(jnp.where(causal, b[:, :, :, None, :] - b[:, :, None, :, :], -jnp.inf))
        if per_key:
            scores = jnp.einsum('bhik,bhijk,bhjk->bhij', qi, seg, ki)
        else:
            scores = jnp.einsum('bhik,bhjk->bhij', qi, ki) * seg[..., 0]
        o = jnp.einsum('bhij,bhjv->bhiv', scores, vi) + jnp.einsum('bhik,bhkv->bhiv', qi * jnp.exp(b), s)
        b_end = b[:, :, -1:, :]
        s = jnp.exp(b_end[:, :, 0, :, None]) * s + jnp.einsum('bhjk,bhjv->bhkv', ki * jnp.exp(b_end - b), vi)
        return s, o

    s_final, o = lax.scan(step, h0.astype(f32), (qc, kc, vc, lc))
    o = o.transpose(1, 0, 3, 2, 4).reshape(bsz, t, nh, v.shape[-1])
    return o, s_final


def bidirectional(scan_fn, fwd_inputs, bwd_inputs, h0):
    o_f, s_f = scan_fn(*fwd_inputs, h0[:, 0])
    o_b, s_b = scan_fn(*[jnp.flip(a, 1) for a in bwd_inputs], h0[:, 1])
    return o_f + jnp.flip(o_b, 1), jnp.stack([s_f, s_b], axis=1)


def rglru_coeffs(xl, w_a, b_a, w_x, b_x, lam):
    xb = xl.reshape(xl.shape[0], xl.shape[1], LRU_BLOCKS, LRU_BLOCK_W)
    r = jax.nn.sigmoid(jnp.einsum('btnw,nwv->btnv', xb, w_a).reshape(xl.shape) + b_a)
    i = jax.nn.sigmoid(jnp.einsum('btnw,nwv->btnv', xb, w_x).reshape(xl.shape) + b_x)
    log_a = -LRU_C * r * jax.nn.softplus(-lam.astype(jnp.float32))
    a = jnp.exp(log_a)
    b = jnp.sqrt(-jnp.expm1(2.0 * log_a)) * (i * xl)
    return a, b


def mixer(h, lp, st_lru, st_ssd, st_gla):
    f32 = jnp.float32
    bsz, t, _ = h.shape
    split_idx = [int(i) for i in np.cumsum(IN_SPLITS)[:-1]]
    (x_lru, gate_lru, z_ssd, xbc, dt_f, dt_b, q, k, v, g_gla, lr_f, lr_b, merge) = jnp.split(
        h @ lp['w_in'], split_idx, axis=-1)

    xl = conv_centered(x_lru, lp['lru_conv_w'], lp['lru_conv_b']).astype(f32)
    coeffs = [rglru_coeffs(xl, lp['lru_w_a'][d], lp['lru_b_a'][d], lp['lru_w_x'][d], lp['lru_b_x'][d],
                           lp['lru_lambda'][d]) for d in range(2)]
    hl, st_lru_new = bidirectional(linear_recurrence, coeffs[0], coeffs[1], st_lru)
    y_lru = hl * jax.nn.gelu(gate_lru.astype(f32))

    xbc = jax.nn.silu(conv_centered(xbc, lp['ssd_conv_w'], lp['ssd_conv_b'])).astype(f32)
    xs, bs, cs = jnp.split(xbc, [D_SSD, D_SSD + SSD_GROUPS * SSD_STATE], axis=-1)
    xs = xs.reshape(bsz, t, SSD_HEADS, SSD_HEAD_DIM)
    rep = SSD_HEADS // SSD_GROUPS
    bs = jnp.repeat(bs.reshape(bsz, t, SSD_GROUPS, SSD_STATE), rep, axis=2)
    cs = jnp.repeat(cs.reshape(bsz, t, SSD_GROUPS, SSD_STATE), rep, axis=2)

    def ssd_inputs(dt_raw, d):
        dt = jax.nn.softplus(dt_raw.astype(f32) + lp['ssd_dt_bias'][d])
        log_a = (dt * -jnp.exp(lp['ssd_a_log'][d].astype(f32)))[..., None]
        return (cs, bs * dt[..., None], xs, log_a)

    ssd_scan = functools.partial(chunked_gated_scan, chunk=SSD_CHUNK)
    ys, st_ssd_new = bidirectional(ssd_scan, ssd_inputs(dt_f, 0), ssd_inputs(dt_b, 1), st_ssd)
    ys = ys + lp['ssd_d'][:, None] * xs
    y_ssd = rms_norm(ys.reshape(bsz, t, D_SSD) * jax.nn.silu(z_ssd.astype(f32)), lp['ssd_norm_g'])

    qh = q.astype(f32).reshape(bsz, t, GLA_HEADS, GLA_HEAD_K) * (GLA_HEAD_K ** -0.5)
    kh = k.astype(f32).reshape(bsz, t, GLA_HEADS, GLA_HEAD_K)
    vh = v.astype(f32).reshape(bsz, t, GLA_HEADS, GLA_HEAD_V)

    def gla_log_decay(lr, d):
        z = lr.astype(f32) @ lp['gla_gate_w2'][d] + lp['gla_gate_b'][d]
        return (jax.nn.log_sigmoid(z) / GLA_GATE_NORM).reshape(bsz, t, GLA_HEADS, GLA_HEAD_K)

    gla_scan = functools.partial(chunked_gated_scan, chunk=GLA_CHUNK)
    og, st_gla_new = bidirectional(gla_scan, (qh, kh, vh, gla_log_decay(lr_f, 0)),
                                   (qh, kh, vh, gla_log_decay(lr_b, 1)), st_gla)
    og = rms_norm(og, lp['gla_norm_g'])
    y_gla = og.reshape(bsz, t, GLA_DV) * jax.nn.silu(g_gla.astype(f32))

    gates = jax.nn.sigmoid(merge.astype(f32)).reshape(bsz, t, N_BRANCH, D_MODEL)
    merged = (gates[:, :, 0] * (y_lru @ lp['w_br_lru'])
              + gates[:, :, 1] * (y_ssd @ lp['w_br_ssd'])
              + gates[:, :, 2] * (y_gla @ lp['w_br_gla']))
    out = (merged @ lp['w_out']).astype(h.dtype)
    return out, (st_lru_new, st_ssd_new, st_gla_new)


def swiglu(h, w_in, w_out):
    g, u = jnp.split(h @ w_in, 2, axis=-1)
    return (jax.nn.silu(g) * u) @ w_out


def trunk_layer(x, cond, states, lp):
    mod = (jax.nn.silu(cond) @ lp['w_ada'] + lp['b_ada'])[:, None, :]
    sh1, sc1, g1, sh2, sc2, g2 = jnp.split(mod, 6, axis=-1)
    h = rms_norm(x, lp['norm_mix_g']) * (1 + sc1) + sh1
    mix, new_states = mixer(h, lp, *states)
    x = x + g1 * mix
    h = rms_norm(x, lp['norm_ffn_g']) * (1 + sc2) + sh2
    x = x + g2 * swiglu(h, lp['w_ffn_in'], lp['w_ffn_out'])
    return x, new_states


def setup_inputs(seed: int = 0) -> dict:
    key = jax.random.key(seed)
    ks = iter(jax.random.split(key, 64))
    L = DEPTH
    def nrm(shape, scale):
        return jax.random.normal(next(ks), shape, jnp.float32) * scale
    def gain(shape):
        return 1.0 + nrm(shape, 0.05)
    dt = jnp.exp(jax.random.uniform(next(ks), (L, 2, SSD_HEADS), jnp.float32, math.log(1e-3), math.log(1e-1)))
    return {
        "x_prompt": nrm((BATCH, SEQ, D_MODEL), 1.0),
        "x_sample": nrm((DEC_BATCH, DEC_SEQ, D_MODEL), 1.0),
        "state_lru": nrm((DEC_BATCH, DEPTH, 2, D_LRU), 0.5),
        "state_ssd": nrm((DEC_BATCH, DEPTH, 2, SSD_HEADS, SSD_STATE, SSD_HEAD_DIM), 0.1),
        "state_gla": nrm((DEC_BATCH, DEPTH, 2, GLA_HEADS, GLA_HEAD_K, GLA_HEAD_V), 0.5),
        "c": nrm((DEC_BATCH, D_MODEL), 1.0),
        "c_ctx": nrm((D_MODEL,), 1.0),
        "norm_mix_g": gain((L, D_MODEL)),
        "norm_ffn_g": gain((L, D_MODEL)),
        "w_ada": nrm((L, D_MODEL, 6 * D_MODEL), 0.5 * D_MODEL ** -0.5),
        "b_ada": nrm((L, 6 * D_MODEL), 0.01),
        "w_in": nrm((L, D_MODEL, IN_WIDTH), D_MODEL ** -0.5),
        "lru_conv_w": nrm((L, CONV_W, D_LRU), CONV_W ** -0.5),
        "lru_conv_b": nrm((L, D_LRU), 0.01),
        "lru_w_a": nrm((L, 2, LRU_BLOCKS, LRU_BLOCK_W, LRU_BLOCK_W), LRU_BLOCK_W ** -0.5),
        "lru_b_a": nrm((L, 2, D_LRU), 0.01),
        "lru_w_x": nrm((L, 2, LRU_BLOCKS, LRU_BLOCK_W, LRU_BLOCK_W), LRU_BLOCK_W ** -0.5),
        "lru_b_x": nrm((L, 2, D_LRU), 0.01),
        "lru_lambda": jax.random.uniform(next(ks), (L, 2, D_LRU), jnp.float32, 4.3, 9.0),
        "ssd_conv_w": nrm((L, CONV_W, SSD_CONV_DIM), CONV_W ** -0.5),
        "ssd_conv_b": nrm((L, SSD_CONV_DIM), 0.01),
        "ssd_dt_bias": dt + jnp.log(-jnp.expm1(-dt)),
        "ssd_a_log": jnp.log(jax.random.uniform(next(ks), (L, 2, SSD_HEADS), jnp.float32, 1.0, 16.0)),
        "ssd_d": gain((L, SSD_HEADS)),
        "ssd_norm_g": gain((L, D_SSD)),
        "gla_gate_w2": nrm((L, 2, GLA_GATE_RANK, GLA_DK), GLA_GATE_RANK ** -0.5),
        "gla_gate_b": nrm((L, 2, GLA_DK), 0.1),
        "gla_norm_g": gain((L, GLA_HEAD_V)),
        "w_br_lru": nrm((L, D_LRU, D_MODEL), D_LRU ** -0.5),
        "w_br_ssd": nrm((L, D_SSD, D_MODEL), D_SSD ** -0.5),
        "w_br_gla": nrm((L, GLA_DV, D_MODEL), GLA_DV ** -0.5),
        "w_out": nrm((L, D_MODEL, D_MODEL), D_MODEL ** -0.5),
        "w_ffn_in": nrm((L, D_MODEL, 2 * D_FF), D_MODEL ** -0.5),
        "w_ffn_out": nrm((L, D_FF, D_MODEL), D_FF ** -0.5),
        "final_norm_g": gain((D_MODEL,)),
    }


def reference(x_prompt, x_sample, state_lru, state_ssd, state_gla, c, c_ctx,
              norm_mix_g, norm_ffn_g, w_ada, b_ada, w_in,
              lru_conv_w, lru_conv_b, lru_w_a, lru_b_a, lru_w_x, lru_b_x, lru_lambda,
              ssd_conv_w, ssd_conv_b, ssd_dt_bias, ssd_a_log, ssd_d, ssd_norm_g,
              gla_gate_w2, gla_gate_b, gla_norm_g,
              w_br_lru, w_br_ssd, w_br_gla, w_out, w_ffn_in, w_ffn_out, final_norm_g):
    layers = [dict(norm_mix_g=norm_mix_g[l], norm_ffn_g=norm_ffn_g[l], w_ada=w_ada[l], b_ada=b_ada[l],
                   w_in=w_in[l], lru_conv_w=lru_conv_w[l], lru_conv_b=lru_conv_b[l],
                   lru_w_a=lru_w_a[l], lru_b_a=lru_b_a[l], lru_w_x=lru_w_x[l], lru_b_x=lru_b_x[l],
                   lru_lambda=lru_lambda[l], ssd_conv_w=ssd_conv_w[l], ssd_conv_b=ssd_conv_b[l],
                   ssd_dt_bias=ssd_dt_bias[l], ssd_a_log=ssd_a_log[l], ssd_d=ssd_d[l], ssd_norm_g=ssd_norm_g[l],
                   gla_gate_w2=gla_gate_w2[l], gla_gate_b=gla_gate_b[l], gla_norm_g=gla_norm_g[l],
                   w_br_lru=w_br_lru[l], w_br_ssd=w_br_ssd[l], w_br_gla=w_br_gla[l], w_out=w_out[l],
                   w_ffn_in=w_ffn_in[l], w_ffn_out=w_ffn_out[l]) for l in range(DEPTH)]

    b_ctx = x_prompt.shape[0]
    xp = x_prompt
    lru_states, ssd_states, gla_states = [], [], []
    for l in range(DEPTH):
        zeros = (jnp.zeros((b_ctx, 2, D_LRU), jnp.float32),
                 jnp.zeros((b_ctx, 2, SSD_HEADS, SSD_STATE, SSD_HEAD_DIM), jnp.float32),
                 jnp.zeros((b_ctx, 2, GLA_HEADS, GLA_HEAD_K, GLA_HEAD_V), jnp.float32))
        xp, (s_lru, s_ssd, s_gla) = trunk_layer(xp, c_ctx[None, :], zeros, layers[l])
        lru_states.append(s_lru)
        ssd_states.append(s_ssd)
        gla_states.append(s_gla)
    y_prompt = rms_norm(xp, final_norm_g)

    xs = snake(x_sample)
    for l in range(DEPTH):
        xs, _ = trunk_layer(xs, c, (state_lru[:, l], state_ssd[:, l], state_gla[:, l]), layers[l])
    y_sample = snake(rms_norm(xs, final_norm_g))

    return (y_prompt, y_sample, jnp.stack(lru_states, axis=1), jnp.stack(ssd_states, axis=1),
            jnp.stack(gla_states, axis=1))
```

```python
import functools

import jax
import jax.numpy as jnp
from jax import lax
from jax.experimental import pallas as pl
from jax.experimental.pallas import tpu as pltpu

F32 = jnp.float32
BF16 = jnp.bfloat16

D_MODEL = 1024
GRID_W = 64
CONV_W = 4
CONV_LEFT = 2
LRU_BLOCK_W = 64
LRU_C = 8.0
SSD_HEAD_DIM = 64
SSD_HEADS = 16
SSD_GROUPS = 4
SSD_STATE = 64
GLA_HEADS = 4
GLA_HEAD_K = 128
GLA_HEAD_V = 256
GLA_DK = GLA_HEADS * GLA_HEAD_K
GLA_GATE_RANK = 16
GLA_GATE_NORM = 16.0
D_FF = 2816
EPS = 1e-6

LANES = 128
SUBLANES = 8
MXU_DIM = 256
VMEM_LIMIT_BYTES = 56 * 1024 * 1024

CONV_COLS = D_MODEL + D_MODEL + 2 * SSD_GROUPS * SSD_STATE
Q_OFF = CONV_COLS
K_OFF = Q_OFF + GLA_DK
V_OFF = K_OFF + GLA_DK
SMALL_OFF = V_OFF + D_MODEL
P1_COLS = SMALL_OFF + LANES
P2_COLS = 6 * D_MODEL
SMALL_DT = 0
SMALL_LR = 2 * SSD_HEADS

T_CHUNK = 256
GLA_SUB = 64
GLA_BLK = 16
NEG_BIG = -1e30


def _params(sem):
    return pltpu.CompilerParams(dimension_semantics=sem, vmem_limit_bytes=VMEM_LIMIT_BYTES)


def _softplus(x):
    return jnp.maximum(x, 0.0) + jnp.log1p(jnp.exp(-jnp.abs(x)))


def _sigmoid(x):
    return 1.0 / (1.0 + jnp.exp(-x))


def _silu(x):
    return x * _sigmoid(x)


def _gelu_tanh(x):
    c = 0.7978845608028654
    return 0.5 * x * (1.0 + jnp.tanh(c * (x + 0.044715 * (x * x * x))))


def _rms(x, g):
    return x * lax.rsqrt(jnp.mean(x * x, axis=-1, keepdims=True) + EPS) * g


def _dot(a, b):
    return jnp.dot(a, b, preferred_element_type=F32)


def _dot_nt(a, b):
    return lax.dot_general(a, b, (((1,), (1,)), ((), ())), preferred_element_type=F32)


def _dot_tn(a, b):
    return lax.dot_general(a, b, (((0,), (0,)), ((), ())), preferred_element_type=F32)


def _mod_kernel(c_ref, w_ref, b_ref, o_ref):
    c = _silu(c_ref[...]).astype(BF16)
    o_ref[...] = _dot(c, w_ref[...].astype(BF16)) + b_ref[...]


def _modulation(cond, w_ada, b_ada):
    n_layers = w_ada.shape[0]
    rows = cond.shape[0]
    tn = D_MODEL
    return pl.pallas_call(
        _mod_kernel,
        grid=(n_layers, 6 * D_MODEL // tn),
        in_specs=[
            pl.BlockSpec((rows, D_MODEL), lambda l, j: (0, 0)),
            pl.BlockSpec((None, D_MODEL, tn), lambda l, j: (l, 0, j)),
            pl.BlockSpec((None, 1, tn), lambda l, j: (l, 0, j)),
        ],
        out_specs=pl.BlockSpec((None, rows, tn), lambda l, j: (l, 0, j)),
        out_shape=jax.ShapeDtypeStruct((n_layers, rows, 6 * D_MODEL), F32),
        compiler_params=_params(("parallel", "parallel")),
        name="modulation",
    )(cond, w_ada, b_ada.reshape(n_layers, 1, 6 * D_MODEL))


def _inproj_kernel(x_ref, mod_ref, g_ref, w_ref, o_ref):
    x = x_ref[...]
    sh = mod_ref[:, 0:D_MODEL]
    sc = mod_ref[:, D_MODEL:2 * D_MODEL]
    h = _rms(x, g_ref[...]) * (1.0 + sc) + sh
    o_ref[...] = _dot(h.astype(BF16), w_ref[...])


def _inproj(x, mod, row_of_block, g, w, tm, tn):
    m = x.shape[0]
    n = w.shape[1]
    return pl.pallas_call(
        _inproj_kernel,
        grid=(n // tn, m // tm),
        in_specs=[
            pl.BlockSpec((tm, D_MODEL), lambda j, i: (i, 0)),
            pl.BlockSpec((None, 1, 6 * D_MODEL), lambda j, i: (row_of_block(i, tm), 0, 0)),
            pl.BlockSpec((1, D_MODEL), lambda j, i: (0, 0)),
            pl.BlockSpec((D_MODEL, tn), lambda j, i: (0, j)),
        ],
        out_specs=pl.BlockSpec((tm, tn), lambda j, i: (i, j)),
        out_shape=jax.ShapeDtypeStruct((m, n), F32),
        compiler_params=_params(("parallel", "parallel")),
        name="inproj",
    )(x, mod, g, w)


def _lin_scan(a, b, rev):
    t = a.shape[0]
    row = lax.broadcasted_iota(jnp.int32, a.shape, 0)
    s = 1
    while s < t:
        shift = (t - s) if rev else s
        valid = (row < t - s) if rev else (row >= s)
        a_sh = pltpu.roll(a, shift, 0)
        b_sh = pltpu.roll(b, shift, 0)
        b = jnp.where(valid, a * b_sh + b, b)
        a = jnp.where(valid, a * a_sh, a)
        s *= 2
    return a, b


def _seg_cumsum(x, seg, rev):
    t = x.shape[0]
    pos = lax.broadcasted_iota(jnp.int32, x.shape, 0) & (seg - 1)
    s = 1
    while s < seg:
        shift = (t - s) if rev else s
        valid = (pos < seg - s) if rev else (pos >= s)
        x = x + jnp.where(valid, pltpu.roll(x, shift, 0), 0.0)
        s *= 2
    return x


def _scan_kernel(*refs, rev, zero_init, emit_state, nchunks):
    it = iter(refs)
    p_ref, prev_ref, next_ref = next(it), next(it), next(it)
    convw_ref, convb_ref = next(it), next(it)
    lruw_ref, lruba_ref, lrubx_ref, lam_ref = next(it), next(it), next(it), next(it)
    dtb_ref, alog_ref, dskip_ref = next(it), next(it), next(it)
    w2_ref, gb_ref = next(it), next(it)
    if not zero_init:
        i_lru, i_ssd, i_gla = next(it), next(it), next(it)
    o_ref = next(it)
    if emit_state:
        s_lru, s_ssd, s_gla = next(it), next(it), next(it)
    cbuf, h_scr, ssd_scr, gla_scr = next(it), next(it), next(it), next(it)

    tc = T_CHUNK
    c = pl.program_id(1)
    c_time = (nchunks - 1 - c) if rev else c

    @pl.when(c == 0)
    def _():
        if zero_init:
            h_scr[...] = jnp.zeros_like(h_scr)
            ssd_scr[...] = jnp.zeros_like(ssd_scr)
            gla_scr[...] = jnp.zeros_like(gla_scr)
        else:
            h_scr[...] = jnp.broadcast_to(i_lru[...], h_scr.shape)
            ssd_scr[...] = i_ssd[...]
            gla_scr[...] = i_gla[...]

    cbuf[SUBLANES:SUBLANES + tc, :] = p_ref[:, 0:CONV_COLS]
    head = jnp.where(c_time == 0, 0.0, 1.0)
    tail = jnp.where(c_time == nchunks - 1, 0.0, 1.0)
    cbuf[0:SUBLANES, :] = prev_ref[...] * head
    cbuf[SUBLANES + tc:2 * SUBLANES + tc, :] = next_ref[...] * tail
    xc = convb_ref[...]
    for j in range(CONV_W):
        xc = xc + convw_ref[j:j + 1, :] * cbuf[pl.ds(SUBLANES - CONV_LEFT + j, tc), :]

    c8 = -LRU_C * _softplus(-lam_ref[...])
    for g in range(D_MODEL // MXU_DIM):
        sl = slice(g * MXU_DIM, (g + 1) * MXU_DIM)
        xl = xc[:, sl]
        pre = _dot(xl.astype(BF16), lruw_ref[g])
        r = _sigmoid(pre[:, :MXU_DIM] + lruba_ref[:, sl])
        i = _sigmoid(pre[:, MXU_DIM:] + lrubx_ref[:, sl])
        log_a = c8[:, sl] * r
        a = jnp.exp(log_a)
        th = jnp.tanh(log_a)
        b = jnp.sqrt(-2.0 * th / (1.0 - th)) * (i * xl)
        a_cum, b_cum = _lin_scan(a, b, rev)
        h = a_cum * h_scr[0:1, sl] + b_cum
        o_ref[:, sl] = h
        last = h[0:1, :] if rev else h[tc - 1:tc, :]
        h_scr[:, sl] = jnp.broadcast_to(last, (SUBLANES, MXU_DIM))

    small = p_ref[:, SMALL_OFF:SMALL_OFF + LANES]
    lane = lax.broadcasted_iota(jnp.int32, (1, LANES), 1)
    dt_lo = SMALL_DT + (SSD_HEADS if rev else 0)
    dt_mask = (lane >= dt_lo) & (lane < dt_lo + SSD_HEADS)
    dt = jnp.where(dt_mask, _softplus(small + dtb_ref[...]), 0.0)
    la = dt * (-jnp.exp(alog_ref[...]))
    bcum = _seg_cumsum(la, tc, rev)
    bcum_t = bcum.T
    tot = bcum[0:1, :] if rev else bcum[tc - 1:tc, :]

    xbc = _silu(xc[:, D_MODEL:CONV_COLS])
    xs = xbc[:, 0:D_MODEL]
    bmat = xbc[:, D_MODEL:D_MODEL + SSD_GROUPS * SSD_STATE]
    cmat = xbc[:, D_MODEL + SSD_GROUPS * SSD_STATE:]
    ri = lax.broadcasted_iota(jnp.int32, (tc, tc), 0)
    ci = lax.broadcasted_iota(jnp.int32, (tc, tc), 1)
    causal = (ci >= ri) if rev else (ci <= ri)
    lane_t = lax.broadcasted_iota(jnp.int32, (tc, LANES), 1)
    lo_half = lane_t < SSD_HEAD_DIM
    row_s = lax.broadcasted_iota(jnp.int32, (2 * SSD_STATE, 4 * SSD_HEAD_DIM), 0)
    for pair in range(SSD_GROUPS // 2):
        psl = slice(pair * LANES, (pair + 1) * LANES)
        b_tile = bmat[:, psl]
        c_tile = cmat[:, psl]
        s_pair = ssd_scr[pair * LANES:(pair + 1) * LANES, :]
        s_pair_bf = s_pair.astype(BF16)
        upd = []
        for gl in range(2):
            g = pair * 2 + gl
            gmask = lo_half if gl == 0 else jnp.logical_not(lo_half)
            c_g = jnp.where(gmask, c_tile, 0.0).astype(BF16)
            gmat = _dot_nt(c_g, b_tile.astype(BF16))
            inter = _dot(c_g, s_pair_bf)
            wx_parts = []
            for hp in range(2):
                h0 = g * 4 + hp * 2
                col0 = bcum[:, dt_lo + h0:dt_lo + h0 + 1]
                col1 = bcum[:, dt_lo + h0 + 1:dt_lo + h0 + 2]
                dtc0 = dt[:, dt_lo + h0:dt_lo + h0 + 1]
                dtc1 = dt[:, dt_lo + h0 + 1:dt_lo + h0 + 2]
                xsl = slice(h0 * SSD_HEAD_DIM, (h0 + 2) * SSD_HEAD_DIM)
                xhat = xs[:, xsl] * jnp.where(lo_half, dtc0, dtc1)
                xhat_bf = xhat.astype(BF16)
                outs = []
                for hh, col in ((0, col0), (1, col1)):
                    rowv = bcum_t[dt_lo + h0 + hh:dt_lo + h0 + hh + 1, :]
                    seg = jnp.exp(jnp.where(causal, col - rowv, NEG_BIG))
                    sc = (gmat * seg).astype(BF16)
                    outs.append(_dot(sc, xhat_bf))
                o_pair = jnp.where(lo_half, outs[0], outs[1])
                e_col = jnp.where(lo_half, jnp.exp(col0), jnp.exp(col1))
                isl = slice(hp * LANES, (hp + 1) * LANES)
                o_pair = o_pair + e_col * inter[:, isl]
                if not rev:
                    o_pair = o_pair + dskip_ref[:, xsl] * xs[:, xsl]
                o_ref[:, D_MODEL + h0 * SSD_HEAD_DIM:D_MODEL + (h0 + 2) * SSD_HEAD_DIM] = o_pair
                t0 = tot[:, dt_lo + h0:dt_lo + h0 + 1]
                t1 = tot[:, dt_lo + h0 + 1:dt_lo + h0 + 2]
                w_dec = jnp.where(lo_half, jnp.exp(t0 - col0), jnp.exp(t1 - col1))
                wx_parts.append((w_dec * xhat).astype(BF16))
            wx = jnp.concatenate(wx_parts, axis=1)
            upd.append(_dot_tn(b_tile.astype(BF16), wx))
        new = jnp.where(row_s < SSD_STATE, upd[0], upd[1])
        lane_s = lax.broadcasted_iota(jnp.int32, (1, 4 * SSD_HEAD_DIM), 1) // SSD_HEAD_DIM
        decs = []
        for gl in range(2):
            g = pair * 2 + gl
            d = jnp.zeros((1, 4 * SSD_HEAD_DIM), F32)
            for hl in range(4):
                th = tot[:, dt_lo + g * 4 + hl:dt_lo + g * 4 + hl + 1]
                d = jnp.where(lane_s == hl, jnp.exp(th), d)
            decs.append(d)
        dec = jnp.where(row_s < SSD_STATE, decs[0], decs[1])
        ssd_scr[pair * LANES:(pair + 1) * LANES, :] = dec * s_pair + new

    lr_lo = SMALL_LR + (GLA_GATE_RANK if rev else 0)
    del lr_lo
    z = _dot(small.astype(BF16), w2_ref[...]) + gb_ref[...]
    ld = -_softplus(-z) * (1.0 / GLA_GATE_NORM)
    bg = _seg_cumsum(ld, GLA_SUB, rev)
    nsub = tc // GLA_SUB
    nblk = GLA_SUB // GLA_BLK
    rj = lax.broadcasted_iota(jnp.int32, (GLA_SUB, GLA_HEAD_K), 0)
    ai = lax.broadcasted_iota(jnp.int32, (GLA_SUB, GLA_SUB), 0)
    aj = lax.broadcasted_iota(jnp.int32, (GLA_SUB, GLA_SUB), 1)
    a_causal = (aj >= ai) if rev else (aj <= ai)
    scale = GLA_HEAD_K ** -0.5
    for hd in range(GLA_HEADS):
        ksl = slice(hd * GLA_HEAD_K, (hd + 1) * GLA_HEAD_K)
        vsl = slice(hd * GLA_HEAD_V, (hd + 1) * GLA_HEAD_V)
        st = gla_scr[hd * GLA_HEAD_K:(hd + 1) * GLA_HEAD_K, :]
        order = range(nsub - 1, -1, -1) if rev else range(nsub)
        for sc_i in order:
            rsl = slice(sc_i * GLA_SUB, (sc_i + 1) * GLA_SUB)
            q = p_ref[rsl, Q_OFF + hd * GLA_HEAD_K:Q_OFF + (hd + 1) * GLA_HEAD_K] * scale
            k = p_ref[rsl, K_OFF + hd * GLA_HEAD_K:K_OFF + (hd + 1) * GLA_HEAD_K]
            v = p_ref[rsl, V_OFF + hd * GLA_HEAD_V:V_OFF + (hd + 1) * GLA_HEAD_V].astype(BF16)
            b = bg[rsl, ksl]
            inter = _dot((q * jnp.exp(b)).astype(BF16), st.astype(BF16))
            a_rows = []
            for blk in range(nblk):
                bsl = slice(blk * GLA_BLK, (blk + 1) * GLA_BLK)
                if rev:
                    ref_row = b[(blk + 1) * GLA_BLK - 1:(blk + 1) * GLA_BLK, :]
                    kvalid = rj >= blk * GLA_BLK
                else:
                    ref_row = b[blk * GLA_BLK:blk * GLA_BLK + 1, :]
                    kvalid = rj < (blk + 1) * GLA_BLK
                qt = (q[bsl, :] * jnp.exp(b[bsl, :] - ref_row)).astype(BF16)
                kt = (k * jnp.exp(jnp.where(kvalid, ref_row - b, NEG_BIG))).astype(BF16)
                a_rows.append(_dot_nt(qt, kt))
            amat = jnp.concatenate(a_rows, axis=0)
            amat = jnp.where(a_causal, amat, 0.0).astype(BF16)
            o = inter + _dot(amat, v)
            o_ref[rsl, 2 * D_MODEL + hd * GLA_HEAD_V:2 * D_MODEL + (hd + 1) * GLA_HEAD_V] = o
            end = b[0:1, :] if rev else b[GLA_SUB - 1:GLA_SUB, :]
            k_dec = (k * jnp.exp(end - b)).astype(BF16)
            dec_col = jnp.broadcast_to(jnp.exp(end), (GLA_HEAD_K, GLA_HEAD_K)).T[:, 0:1]
            st = dec_col * st + _dot_tn(k_dec, v)
        gla_scr[hd * GLA_HEAD_K:(hd + 1) * GLA_HEAD_K, :] = st

    if emit_state:
        @pl.when(c == nchunks - 1)
        def _():
            s_lru[...] = h_scr[0:1, :]
            s_ssd[...] = ssd_scr[...]
            s_gla[...] = gla_scr[...]


def _scan(p1, tok0, nseq, nchunks, lw, d, init, emit_state):
    rev = d == 1
    tc = T_CHUNK
    blk0 = tok0 // tc
    rows8 = tc // SUBLANES
    last8 = p1.shape[0] // SUBLANES - 1

    def chunk_of(c):
        return (nchunks - 1 - c) if rev else c

    def main_map(b, c):
        return (blk0 + b * nchunks + chunk_of(c), 0)

    def prev_map(b, c):
        return (jnp.maximum((blk0 + b * nchunks + chunk_of(c)) * rows8 - 1, 0), 0)

    def next_map(b, c):
        return (jnp.minimum((blk0 + b * nchunks + chunk_of(c) + 1) * rows8, last8), 0)

    const2 = lambda b, c: (0, 0)
    const3 = lambda b, c: (0, 0, 0)
    in_specs = [
        pl.BlockSpec((tc, P1_COLS), main_map),
        pl.BlockSpec((SUBLANES, CONV_COLS), prev_map),
        pl.BlockSpec((SUBLANES, CONV_COLS), next_map),
        pl.BlockSpec((CONV_W, CONV_COLS), const2),
        pl.BlockSpec((1, CONV_COLS), const2),
        pl.BlockSpec((D_MODEL // MXU_DIM, MXU_DIM, 2 * MXU_DIM), const3),
        pl.BlockSpec((1, D_MODEL), const2),
        pl.BlockSpec((1, D_MODEL), const2),
        pl.BlockSpec((1, D_MODEL), const2),
        pl.BlockSpec((1, LANES), const2),
        pl.BlockSpec((1, LANES), const2),
        pl.BlockSpec((1, D_MODEL), const2),
        pl.BlockSpec((LANES, GLA_DK), const2),
        pl.BlockSpec((1, GLA_DK), const2),
    ]
    args = [p1, p1, p1, lw["conv_w"], lw["conv_b"], lw["lru_w"][d], lw["lru_ba"][d], lw["lru_bx"][d],
            lw["lam"][d], lw["dt_bias"][d], lw["a_log"][d], lw["d_skip"], lw["gate_w2"][d], lw["gate_b"][d]]
    ssd_rows = SSD_GROUPS * SSD_STATE
    gla_rows = GLA_HEADS * GLA_HEAD_K
    state_specs = [
        pl.BlockSpec((None, 1, D_MODEL), lambda b, c: (b, 0, 0)),
        pl.BlockSpec((None, ssd_rows, 4 * SSD_HEAD_DIM), lambda b, c: (b, 0, 0)),
        pl.BlockSpec((None, gla_rows, GLA_HEAD_V), lambda b, c: (b, 0, 0)),
    ]
    if init is not None:
        in_specs += state_specs
        args += list(init)
    out_specs = [pl.BlockSpec((tc, 3 * D_MODEL), lambda b, c: (b * nchunks + chunk_of(c), 0))]
    out_shape = [jax.ShapeDtypeStruct((nseq * nchunks * tc, 3 * D_MODEL), F32)]
    if emit_state:
        out_specs += state_specs
        out_shape += [jax.ShapeDtypeStruct((nseq, 1, D_MODEL), F32),
                      jax.ShapeDtypeStruct((nseq, ssd_rows, 4 * SSD_HEAD_DIM), F32),
                      jax.ShapeDtypeStruct((nseq, gla_rows, GLA_HEAD_V), F32)]
    kern = functools.partial(_scan_kernel, rev=rev, zero_init=init is None,
                             emit_state=emit_state, nchunks=nchunks)
    return pl.pallas_call(
        kern,
        grid=(nseq, nchunks),
        in_specs=in_specs,
        out_specs=out_specs,
        out_shape=out_shape,
        scratch_shapes=[
            pltpu.VMEM((tc + 2 * SUBLANES, CONV_COLS), F32),
            pltpu.VMEM((SUBLANES, D_MODEL), F32),
            pltpu.VMEM((ssd_rows, 4 * SSD_HEAD_DIM), F32),
            pltpu.VMEM((gla_rows, GLA_HEAD_V), F32),
        ],
        compiler_params=_params(("parallel", "arbitrary")),
        name="scan_rev" if rev else "scan_fwd",
    )(*args)


def _post_kernel(x_ref, of_ref, ob_ref, p2_ref, mod_ref, ssdg_ref, glag_ref,
                 wl_ref, ws_ref, wg_ref, wo_ref, o_ref):
    dm = D_MODEL
    hl = of_ref[:, 0:dm] + ob_ref[:, 0:dm]
    y_lru = hl * _gelu_tanh(p2_ref[:, 0:dm])
    ys = of_ref[:, dm:2 * dm] + ob_ref[:, dm:2 * dm]
    y_ssd = _rms(ys * _silu(p2_ref[:, dm:2 * dm]), ssdg_ref[...])
    parts = []
    for hd in range(GLA_HEADS):
        sl = slice(2 * dm + hd * GLA_HEAD_V, 2 * dm + (hd + 1) * GLA_HEAD_V)
        parts.append(_rms(of_ref[:, sl] + ob_ref[:, sl], glag_ref[...]))
    y_gla = jnp.concatenate(parts, axis=1) * _silu(p2_ref[:, 2 * dm:3 * dm])
    merged = (_sigmoid(p2_ref[:, 3 * dm:4 * dm]) * _dot(y_lru.astype(BF16), wl_ref[...])
              + _sigmoid(p2_ref[:, 4 * dm:5 * dm]) * _dot(y_ssd.astype(BF16), ws_ref[...])
              + _sigmoid(p2_ref[:, 5 * dm:6 * dm]) * _dot(y_gla.astype(BF16), wg_ref[...]))
    out = _dot(merged.astype(BF16), wo_ref[...])
    o_ref[...] = x_ref[...] + mod_ref[:, 2 * dm:3 * dm] * out


def _post(x, o_f, o_b, p2, mod, row_of_block, lw, tm):
    m = x.shape[0]
    wspec = pl.BlockSpec((D_MODEL, D_MODEL), lambda i: (0, 0))
    return pl.pallas_call(
        _post_kernel,
        grid=(m // tm,),
        in_specs=[
            pl.BlockSpec((tm, D_MODEL), lambda i: (i, 0)),
            pl.BlockSpec((tm, 3 * D_MODEL), lambda i: (i, 0)),
            pl.BlockSpec((tm, 3 * D_MODEL), lambda i: (i, 0)),
            pl.BlockSpec((tm, P2_COLS), lambda i: (i, 0)),
            pl.BlockSpec((None, 1, 6 * D_MODEL), lambda i: (row_of_block(i, tm), 0, 0)),
            pl.BlockSpec((1, D_MODEL), lambda i: (0, 0)),
            pl.BlockSpec((1, GLA_HEAD_V), lambda i: (0, 0)),
            wspec, wspec, wspec, wspec,
        ],
        out_specs=pl.BlockSpec((tm, D_MODEL), lambda i: (i, 0)),
        out_shape=jax.ShapeDtypeStruct((m, D_MODEL), F32),
        compiler_params=_params(("parallel",)),
        name="post",
    )(x, o_f, o_b, p2, mod, lw["ssd_norm_g"], lw["gla_norm_g"],
      lw["w_br_lru"], lw["w_br_ssd"], lw["w_br_gla"], lw["w_out"])


def _ffn_kernel(x_ref, mod_ref, g_ref, wg_ref, wu_ref, wo_ref, fg_ref, o_ref, *, final_norm, nsplit):
    dm = D_MODEL
    x = x_ref[...]
    h = (_rms(x, g_ref[...]) * (1.0 + mod_ref[:, 4 * dm:5 * dm]) + mod_ref[:, 3 * dm:4 * dm]).astype(BF16)
    tf = D_FF // nsplit
    acc = jnp.zeros(x.shape, F32)
    for s in range(nsplit):
        fs = slice(s * tf, (s + 1) * tf)
        gate = _dot(h, wg_ref[:, fs])
        up = _dot(h, wu_ref[:, fs])
        acc = acc + _dot((_silu(gate) * up).astype(BF16), wo_ref[fs, :])
    y = x + mod_ref[:, 5 * dm:6 * dm] * acc
    if final_norm:
        y = _rms(y, fg_ref[...])
    o_ref[...] = y


def _ffn(x, mod, row_of_block, lw, final_g, final_norm, tm):
    m = x.shape[0]
    single = dict(pipeline_mode=pl.Buffered(1))
    kern = functools.partial(_ffn_kernel, final_norm=final_norm, nsplit=2)
    return pl.pallas_call(
        kern,
        grid=(m // tm,),
        in_specs=[
            pl.BlockSpec((tm, D_MODEL), lambda i: (i, 0)),
            pl.BlockSpec((None, 1, 6 * D_MODEL), lambda i: (row_of_block(i, tm), 0, 0)),
            pl.BlockSpec((1, D_MODEL), lambda i: (0, 0)),
            pl.BlockSpec((D_MODEL, D_FF), lambda i: (0, 0), **single),
            pl.BlockSpec((D_MODEL, D_FF), lambda i: (0, 0), **single),
            pl.BlockSpec((D_FF, D_MODEL), lambda i: (0, 0), **single),
            pl.BlockSpec((1, D_MODEL), lambda i: (0, 0)),
        ],
        out_specs=pl.BlockSpec((tm, D_MODEL), lambda i: (i, 0)),
        out_shape=jax.ShapeDtypeStruct((m, D_MODEL), F32),
        compiler_params=_params(("parallel",)),
        name="ffn",
    )(x, mod, lw["norm_ffn_g"], lw["w_ffn_g"], lw["w_ffn_u"], lw["w_ffn_out"], final_g)


def _snake(x):
    bsz, t, d = x.shape
    grid = x.reshape(bsz, t // (2 * GRID_W), 2, GRID_W, d)
    return jnp.concatenate([grid[:, :, :1], jnp.flip(grid[:, :, 1:], 3)], axis=2).reshape(bsz, t, d)


def _layer_weights(l, w):
    dm = D_MODEL
    wi = w["w_in"][l]
    o_gate, o_z, o_xbc = dm, 2 * dm, 3 * dm
    o_dt = o_xbc + CONV_COLS - dm
    o_q = o_dt + 2 * SSD_HEADS
    o_k = o_q + GLA_DK
    o_v = o_k + GLA_DK
    o_g = o_v + dm
    o_lr = o_g + dm
    o_m = o_lr + 2 * GLA_GATE_RANK
    pad = jnp.zeros((dm, LANES - 2 * SSD_HEADS - 2 * GLA_GATE_RANK), wi.dtype)
    w1 = jnp.concatenate([wi[:, 0:dm], wi[:, o_xbc:o_dt], wi[:, o_q:o_g],
                          wi[:, o_dt:o_q], wi[:, o_lr:o_m], pad], axis=1).astype(BF16)
    w2 = jnp.concatenate([wi[:, o_gate:o_xbc], wi[:, o_g:o_lr], wi[:, o_m:]], axis=1).astype(BF16)

    def block_diag_tiles(wa, wx):
        per = MXU_DIM // LRU_BLOCK_W
        eye = jnp.eye(per, dtype=wa.dtype)
        def tiles(wb):
            wb = wb.reshape(dm // MXU_DIM, per, LRU_BLOCK_W, LRU_BLOCK_W)
            t = jnp.einsum("gpwv,pq->gpwqv", wb, eye)
            return t.reshape(dm // MXU_DIM, MXU_DIM, MXU_DIM)
        return jnp.concatenate([tiles(wa), tiles(wx)], axis=2).astype(BF16)

    def lane_pad(v, lo):
        return jnp.zeros((1, LANES), F32).at[0, lo:lo + v.shape[0]].set(v)

    def gate_w2(d):
        lo = SMALL_LR + d * GLA_GATE_RANK
        return jnp.zeros((LANES, GLA_DK), F32).at[lo:lo + GLA_GATE_RANK].set(w["gla_gate_w2"][l, d]).astype(BF16)

    return dict(
        w1=w1, w2=w2,
        norm_mix_g=w["norm_mix_g"][l][None], norm_ffn_g=w["norm_ffn_g"][l][None],
        conv_w=jnp.concatenate([w["lru_conv_w"][l], w["ssd_conv_w"][l]], axis=1),
        conv_b=jnp.concatenate([w["lru_conv_b"][l], w["ssd_conv_b"][l]])[None],
        lru_w=[block_diag_tiles(w["lru_w_a"][l, d], w["lru_w_x"][l, d]) for d in range(2)],
        lru_ba=[w["lru_b_a"][l, d][None] for d in range(2)],
        lru_bx=[w["lru_b_x"][l, d][None] for d in range(2)],
        lam=[w["lru_lambda"][l, d][None] for d in range(2)],
        dt_bias=[lane_pad(w["ssd_dt_bias"][l, d], SMALL_DT + d * SSD_HEADS) for d in range(2)],
        a_log=[lane_pad(w["ssd_a_log"][l, d], SMALL_DT + d * SSD_HEADS) for d in range(2)],
        d_skip=jnp.repeat(w["ssd_d"][l], SSD_HEAD_DIM)[None],
        gate_w2=[gate_w2(d) for d in range(2)],
        gate_b=[w["gla_gate_b"][l, d][None] for d in range(2)],
        ssd_norm_g=w["ssd_norm_g"][l][None], gla_norm_g=w["gla_norm_g"][l][None],
        w_br_lru=w["w_br_lru"][l].astype(BF16), w_br_ssd=w["w_br_ssd"][l].astype(BF16),
        w_br_gla=w["w_br_gla"][l].astype(BF16), w_out=w["w_out"][l].astype(BF16),
        w_ffn_g=w["w_ffn_in"][l][:, :D_FF].astype(BF16), w_ffn_u=w["w_ffn_in"][l][:, D_FF:].astype(BF16),
        w_ffn_out=w["w_ffn_out"][l].astype(BF16),
    )


def _ssd_state_in(s):
    b = s.shape[0]
    s = s.reshape(b, SSD_GROUPS, 4, SSD_STATE, SSD_HEAD_DIM).transpose(0, 1, 3, 2, 4)
    return s.reshape(b, SSD_GROUPS * SSD_STATE, 4 * SSD_HEAD_DIM)


def _ssd_state_out(s):
    b = s.shape[0]
    s = s.reshape(b, SSD_GROUPS, SSD_STATE, 4, SSD_HEAD_DIM).transpose(0, 1, 3, 2, 4)
    return s.reshape(b, SSD_HEADS, SSD_STATE, SSD_HEAD_DIM)


def kernel(x_prompt, x_sample, state_lru, state_ssd, state_gla, c, c_ctx, norm_mix_g, norm_ffn_g, w_ada, b_ada, w_in, lru_conv_w, lru_conv_b, lru_w_a, lru_b_a, lru_w_x, lru_b_x, lru_lambda, ssd_conv_w, ssd_conv_b, ssd_dt_bias, ssd_a_log, ssd_d, ssd_norm_g, gla_gate_w2, gla_gate_b, gla_norm_g, w_br_lru, w_br_ssd, w_br_gla, w_out, w_ffn_in, w_ffn_out, final_norm_g):
    w = dict(norm_mix_g=norm_mix_g, norm_ffn_g=norm_ffn_g, w_in=w_in, lru_conv_w=lru_conv_w,
             lru_conv_b=lru_conv_b, lru_w_a=lru_w_a, lru_b_a=lru_b_a, lru_w_x=lru_w_x, lru_b_x=lru_b_x,
             lru_lambda=lru_lambda, ssd_conv_w=ssd_conv_w, ssd_conv_b=ssd_conv_b, ssd_dt_bias=ssd_dt_bias,
             ssd_a_log=ssd_a_log, ssd_d=ssd_d, ssd_norm_g=ssd_norm_g, gla_gate_w2=gla_gate_w2,
             gla_gate_b=gla_gate_b, gla_norm_g=gla_norm_g, w_br_lru=w_br_lru, w_br_ssd=w_br_ssd,
             w_br_gla=w_br_gla, w_out=w_out, w_ffn_in=w_ffn_in, w_ffn_out=w_ffn_out)
    n_layers = w_in.shape[0]
    b_ctx, t_ctx, dm = x_prompt.shape
    b_lat, t_lat, _ = x_sample.shape
    m_ctx = b_ctx * t_ctx
    m_lat = b_lat * t_lat
    nch_ctx = t_ctx // T_CHUNK
    nch_lat = t_lat // T_CHUNK

    n_rows = -(-(1 + b_lat) // SUBLANES) * SUBLANES
    cond = jnp.zeros((n_rows, dm), F32).at[0].set(c_ctx).at[1:1 + b_lat].set(c)
    mod = _modulation(cond, w_ada, b_ada)

    def row_of_block(i, tm):
        tok = i * tm
        return jnp.where(tok < m_ctx, 0, 1 + (tok - m_ctx) // t_lat)

    x = jnp.concatenate([x_prompt.reshape(m_ctx, dm), _snake(x_sample).reshape(m_lat, dm)], axis=0)

    lru_states, ssd_states, gla_states = [], [], []
    for l in range(n_layers):
        lw = _layer_weights(l, w)
        mod_l = mod[l][:, None, :]
        p1 = _inproj(x, mod_l, row_of_block, lw["norm_mix_g"], lw["w1"], 256, P1_COLS)
        p2 = _inproj(x, mod_l, row_of_block, lw["norm_mix_g"], lw["w2"], 256, P2_COLS // 2)
        outs = []
        st_l = []
        for d in range(2):
            o_ctx, s_lru, s_ssd, s_gla = _scan(p1, 0, b_ctx, nch_ctx, lw, d, None, True)
            init = (state_lru[:, l, d][:, None, :], _ssd_state_in(state_ssd[:, l, d]),
                    state_gla[:, l, d].reshape(b_lat, GLA_HEADS * GLA_HEAD_K, GLA_HEAD_V))
            (o_lat,) = _scan(p1, m_ctx, b_lat, nch_lat, lw, d, init, False)
            outs.append(jnp.concatenate([o_ctx, o_lat], axis=0))
            st_l.append((s_lru[:, 0, :], _ssd_state_out(s_ssd),
                         s_gla.reshape(b_ctx, GLA_HEADS, GLA_HEAD_K, GLA_HEAD_V)))
        lru_states.append(jnp.stack([st_l[0][0], st_l[1][0]], axis=1))
        ssd_states.append(jnp.stack([st_l[0][1], st_l[1][1]], axis=1))
        gla_states.append(jnp.stack([st_l[0][2], st_l[1][2]], axis=1))
        x = _post(x, outs[0], outs[1], p2, mod_l, row_of_block, lw, 256)
        x = _ffn(x, mod_l, row_of_block, lw, final_norm_g[None], l == n_layers - 1, 256)

    y_prompt = x[:m_ctx].reshape(b_ctx, t_ctx, dm)
    y_sample = _snake(x[m_ctx:].reshape(b_lat, t_lat, dm))
    return (y_prompt, y_sample, jnp.stack(lru_states, axis=1), jnp.stack(ssd_states, axis=1),
            jnp.stack(gla_states, axis=1))
```

```python
import functools

import jax
import jax.numpy as jnp
from jax import lax
from jax.experimental import pallas as pl
from jax.experimental.pallas import tpu as pltpu

F32 = jnp.float32
BF16 = jnp.bfloat16

D_MODEL = 1024
GRID_W = 64
CONV_W = 4
CONV_LEFT = 2
LRU_BLOCK_W = 64
LRU_C = 8.0
SSD_HEAD_DIM = 64
SSD_HEADS = 16
SSD_GROUPS = 4
SSD_STATE = 64
GLA_HEADS = 4
GLA_HEAD_K = 128
GLA_HEAD_V = 256
GLA_DK = GLA_HEADS * GLA_HEAD_K
GLA_GATE_RANK = 16
GLA_GATE_NORM = 16.0
D_FF = 2816
EPS = 1e-6

LANES = 128
SUBLANES = 8
MXU_DIM = 256
VMEM_LIMIT_BYTES = 60 * 1024 * 1024

CONV_COLS = D_MODEL + D_MODEL + 2 * SSD_GROUPS * SSD_STATE
Q_OFF = CONV_COLS
K_OFF = Q_OFF + GLA_DK
V_OFF = K_OFF + GLA_DK
SMALL_OFF = V_OFF + D_MODEL
P1_COLS = SMALL_OFF + LANES
P2_COLS = 6 * D_MODEL
SMALL_DT = 0
SMALL_LR = 2 * SSD_HEADS

T_CHUNK = 256
TM_POST = 256
GLA_SUB = 64
GLA_BLK = 16
NEG_BIG = -1e30
LOG2E = 1.4426950408889634
SSD_ROWS = SSD_GROUPS * SSD_STATE
SSD_LANES = (SSD_HEADS // SSD_GROUPS) * SSD_HEAD_DIM
GLA_ROWS = GLA_HEADS * GLA_HEAD_K


def _params(sem):
    return pltpu.CompilerParams(dimension_semantics=sem, vmem_limit_bytes=VMEM_LIMIT_BYTES)


def _softplus(x):
    return jnp.maximum(x, 0.0) + jnp.log1p(jnp.exp(-jnp.abs(x)))


def _softplus_log(x):
    return jnp.maximum(x, 0.0) + jnp.log(1.0 + jnp.exp(-jnp.abs(x)))


def _sigmoid(x):
    return 0.5 * (1.0 + jnp.tanh(0.5 * x))


def _silu(x):
    return x * _sigmoid(x)


def _gelu_tanh(x):
    c = 0.7978845608028654
    return 0.5 * x * (1.0 + jnp.tanh(c * (x + 0.044715 * (x * x * x))))


def _rms(x, g):
    return x * lax.rsqrt(jnp.mean(x * x, axis=-1, keepdims=True) + EPS) * g


def _dot(a, b):
    return jnp.dot(a, b, preferred_element_type=F32)


def _dot_nt(a, b):
    return lax.dot_general(a, b, (((1,), (1,)), ((), ())), preferred_element_type=F32)


def _dot_tn(a, b):
    return lax.dot_general(a, b, (((0,), (0,)), ((), ())), preferred_element_type=F32)


def _mod_kernel(c_ref, w_ref, b_ref, o_ref):
    c = _silu(c_ref[...]).astype(BF16)
    o_ref[...] = _dot(c, w_ref[...].astype(BF16)) + b_ref[...]


def _modulation(cond, w_ada, b_ada):
    n_layers = w_ada.shape[0]
    rows = cond.shape[0]
    tn = D_MODEL
    return pl.pallas_call(
        _mod_kernel,
        grid=(n_layers, 6 * D_MODEL // tn),
        in_specs=[
            pl.BlockSpec((rows, D_MODEL), lambda l, j: (0, 0)),
            pl.BlockSpec((None, D_MODEL, tn), lambda l, j: (l, 0, j)),
            pl.BlockSpec((None, 1, tn), lambda l, j: (l, 0, j)),
        ],
        out_specs=pl.BlockSpec((None, rows, tn), lambda l, j: (l, 0, j)),
        out_shape=jax.ShapeDtypeStruct((n_layers, rows, 6 * D_MODEL), F32),
        compiler_params=_params(("parallel", "parallel")),
        name="modulation",
    )(cond, w_ada, b_ada.reshape(n_layers, 1, 6 * D_MODEL))


def _inproj_kernel(x_ref, xp_ref, xn_ref, mod_ref, g_ref, w1_ref, w2_ref, cw_ref, cb_ref,
                   p1_ref, p2_ref, hext, cbuf, *, seq_pos):
    tc = T_CHUNK
    sh = mod_ref[:, 0:D_MODEL]
    sc1 = 1.0 + mod_ref[:, D_MODEL:2 * D_MODEL]
    g = g_ref[...]
    h_main = _rms(x_ref[...], g) * sc1 + sh
    hext[SUBLANES:SUBLANES + tc, :] = h_main
    pos, seq_len = seq_pos(pl.program_id(0) * tc)
    head = jnp.where(pos == 0, 0.0, 1.0)
    tail = jnp.where(pos + tc == seq_len, 0.0, 1.0)
    hext[0:SUBLANES, :] = (_rms(xp_ref[...], g) * sc1 + sh) * head
    hext[SUBLANES + tc:, :] = (_rms(xn_ref[...], g) * sc1 + sh) * tail
    cbuf[...] = _dot(hext[...].astype(BF16), w1_ref[:, 0:CONV_COLS])
    xc = cb_ref[...]
    for j in range(CONV_W):
        xc = xc + cw_ref[j:j + 1, :] * cbuf[pl.ds(SUBLANES - CONV_LEFT + j, tc), :]
    p1_ref[:, 0:D_MODEL] = xc[:, 0:D_MODEL]
    p1_ref[:, D_MODEL:CONV_COLS] = _silu(xc[:, D_MODEL:CONV_COLS])
    hb = h_main.astype(BF16)
    p1_ref[:, CONV_COLS:] = _dot(hb, w1_ref[:, CONV_COLS:])
    p2_ref[...] = _dot(hb, w2_ref[...])


def _inproj(x, mod, row_of_block, seq_pos, lw):
    m = x.shape[0]
    tc = T_CHUNK
    rows8 = tc // SUBLANES
    last8 = m // SUBLANES - 1
    single = dict(pipeline_mode=pl.Buffered(1))
    return pl.pallas_call(
        functools.partial(_inproj_kernel, seq_pos=seq_pos),
        grid=(m // tc,),
        in_specs=[
            pl.BlockSpec((tc, D_MODEL), lambda i: (i, 0)),
            pl.BlockSpec((SUBLANES, D_MODEL), lambda i: (jnp.maximum(i * rows8 - 1, 0), 0)),
            pl.BlockSpec((SUBLANES, D_MODEL), lambda i: (jnp.minimum((i + 1) * rows8, last8), 0)),
            pl.BlockSpec((None, 1, 6 * D_MODEL), lambda i: (row_of_block(i, tc), 0, 0)),
            pl.BlockSpec((1, D_MODEL), lambda i: (0, 0)),
            pl.BlockSpec((D_MODEL, P1_COLS), lambda i: (0, 0), **single),
            pl.BlockSpec((D_MODEL, P2_COLS), lambda i: (0, 0), **single),
            pl.BlockSpec((CONV_W, CONV_COLS), lambda i: (0, 0)),
            pl.BlockSpec((1, CONV_COLS), lambda i: (0, 0)),
        ],
        out_specs=[pl.BlockSpec((tc, P1_COLS), lambda i: (i, 0)),
                   pl.BlockSpec((tc, P2_COLS), lambda i: (i, 0))],
        out_shape=[jax.ShapeDtypeStruct((m, P1_COLS), F32), jax.ShapeDtypeStruct((m, P2_COLS), F32)],
        scratch_shapes=[pltpu.VMEM((tc + 2 * SUBLANES, D_MODEL), F32),
                        pltpu.VMEM((tc + 2 * SUBLANES, CONV_COLS), F32)],
        compiler_params=_params(("parallel",)),
        name="inproj",
    )(x, x, x, mod, lw["norm_mix_g"], lw["w1"], lw["w2"], lw["conv_w"], lw["conv_b"])


def _tile_scan(a, b, rev):
    t, w = a.shape
    a = a.reshape(t // SUBLANES, SUBLANES, w)
    b = b.reshape(t // SUBLANES, SUBLANES, w)
    pos = lax.broadcasted_iota(jnp.int32, (1, SUBLANES, w), 1)
    s = 1
    while s < SUBLANES:
        shift = (SUBLANES - s) if rev else s
        valid = (pos < SUBLANES - s) if rev else (pos >= s)
        a_sh = jnp.where(valid, pltpu.roll(a, shift, 1), 1.0)
        b_sh = jnp.where(valid, pltpu.roll(b, shift, 1), 0.0)
        yield
        b = a * b_sh + b
        a = a * a_sh
        yield
        s *= 2
    return a, b


def _round_robin(tasks):
    tasks = list(tasks)
    while tasks:
        for task in list(tasks):
            gen, stages = task
            try:
                for _ in range(stages):
                    next(gen)
            except StopIteration:
                tasks.remove(task)


def _seg_cumsum(x, seg, rev):
    t, w = x.shape
    ntile = t // SUBLANES
    per_seg = seg // SUBLANES
    x = x.reshape(ntile, SUBLANES, w)
    pos = lax.broadcasted_iota(jnp.int32, (1, SUBLANES, w), 1)
    s = 1
    while s < SUBLANES:
        shift = (SUBLANES - s) if rev else s
        valid = (pos < SUBLANES - s) if rev else (pos >= s)
        x = x + jnp.where(valid, pltpu.roll(x, shift, 1), 0.0)
        s *= 2
    tiles = [None] * ntile
    for s0 in range(0, ntile, per_seg):
        carry = None
        for j in (range(s0 + per_seg - 1, s0 - 1, -1) if rev else range(s0, s0 + per_seg)):
            tiles[j] = x[j] if carry is None else x[j] + carry
            carry = tiles[j][0:1, :] if rev else tiles[j][SUBLANES - 1:SUBLANES, :]
    return jnp.stack(tiles, axis=0).reshape(t, w)


def _ssd_intra(gmat, col, rowv, xhat_bf, tri, rev):
    half = tri.shape[0]
    lo, hi = slice(0, half), slice(half, 2 * half)

    def blk(rs, cs, masked):
        e = col[rs, :] - rowv[:, cs]
        if masked:
            e = jnp.where(tri, e, NEG_BIG)
        return (gmat[rs, cs] * jnp.exp2(e)).astype(BF16)

    if rev:
        out_top = _dot(jnp.concatenate([blk(lo, lo, True), blk(lo, hi, False)], axis=1), xhat_bf)
        yield
        out_bot = _dot(blk(hi, hi, True), xhat_bf[hi, :])
    else:
        out_top = _dot(blk(lo, lo, True), xhat_bf[lo, :])
        yield
        out_bot = _dot(jnp.concatenate([blk(hi, lo, False), blk(hi, hi, True)], axis=1), xhat_bf)
    yield
    return jnp.concatenate([out_top, out_bot], axis=0)


def _ssd_state_slices(h):
    g, hl = divmod(h, SSD_HEADS // SSD_GROUPS)
    return (slice(g * SSD_STATE, (g + 1) * SSD_STATE),
            slice(hl * SSD_HEAD_DIM, (hl + 1) * SSD_HEAD_DIM))


def _scan_kernel(*refs, rev, zero_init, emit_state, alias_out, alias_state, nchunks):
    it = iter(refs)
    p_ref = next(it)
    lruw_ref, lruba_ref, lrubx_ref, lam_ref = next(it), next(it), next(it), next(it)
    dtb_ref, alog_ref, dskip_ref = next(it), next(it), next(it)
    w2_ref, gb_ref = next(it), next(it)
    if not zero_init:
        i_lru, i_ssd, i_gla = next(it), next(it), next(it)
    if alias_out:
        next(it)
    if alias_state:
        next(it), next(it), next(it)
    o_ref = next(it)
    if emit_state:
        s_lru, s_ssd, s_gla = next(it), next(it), next(it)
    h_scr, ssd_scr, gla_scr = next(it), next(it), next(it)

    tc = T_CHUNK
    c = pl.program_id(1)

    @pl.when(c == 0)
    def _():
        if zero_init:
            h_scr[...] = jnp.zeros_like(h_scr)
            ssd_scr[...] = jnp.zeros_like(ssd_scr)
            gla_scr[...] = jnp.zeros_like(gla_scr)
        else:
            h_scr[...] = jnp.broadcast_to(i_lru[...], h_scr.shape)
            for h in range(SSD_HEADS):
                rs, ls = _ssd_state_slices(h)
                ssd_scr[rs, ls] = i_ssd[h]
            gla_scr[...] = i_gla[...]

    small = p_ref[:, SMALL_OFF:SMALL_OFF + LANES]

    c8h = (-0.5 * LRU_C) * _softplus(-lam_ref[...])
    ntile = tc // SUBLANES

    def lru_group(g):
        sl = slice(g * MXU_DIM, (g + 1) * MXU_DIM)
        xl = p_ref[:, sl]
        pre = _dot(xl.astype(BF16), lruw_ref[g])
        yield
        t_r = jnp.tanh(pre[:, :MXU_DIM] + lruba_ref[:, sl])
        yield
        t_i = jnp.tanh(pre[:, MXU_DIM:] + lrubx_ref[:, sl])
        yield
        log_a = c8h[:, sl] * t_r + c8h[:, sl]
        xlh = 0.5 * xl
        ix = xlh * t_i + xlh
        yield
        a = jnp.exp(log_a)
        th = jnp.tanh(log_a)
        yield
        b = jnp.sqrt(-2.0 * th / (1.0 - th)) * ix
        yield
        a_cum, b_cum = yield from _tile_scan(a, b, rev)
        carry = h_scr[0:1, sl]
        for n, ti in enumerate(range(ntile - 1, -1, -1) if rev else range(ntile)):
            h = a_cum[ti] * carry + b_cum[ti]
            o_ref[ti * SUBLANES:(ti + 1) * SUBLANES, sl] = h
            carry = h[0:1, :] if rev else h[SUBLANES - 1:SUBLANES, :]
            if n % 4 == 3:
                yield
        h_scr[:, sl] = jnp.broadcast_to(carry, (SUBLANES, MXU_DIM))

    lane = lax.broadcasted_iota(jnp.int32, (1, LANES), 1)
    dt_lo = SMALL_DT + (SSD_HEADS if rev else 0)
    dt_mask = (lane >= dt_lo) & (lane < dt_lo + SSD_HEADS)
    dt = jnp.where(dt_mask, _softplus_log(small + dtb_ref[...]), 0.0)
    la = dt * (-LOG2E * jnp.exp(alog_ref[...]))
    bcum = _seg_cumsum(la, tc, rev)
    bcum_t = bcum.T
    tot = bcum[0:1, :] if rev else bcum[tc - 1:tc, :]

    half = tc // 2
    ri = lax.broadcasted_iota(jnp.int32, (half, half), 0)
    ci = lax.broadcasted_iota(jnp.int32, (half, half), 1)
    tri = (ci >= ri) if rev else (ci <= ri)
    lane_t = lax.broadcasted_iota(jnp.int32, (tc, LANES), 1)
    lo_half = lane_t < SSD_HEAD_DIM
    row_s = lax.broadcasted_iota(jnp.int32, (2 * SSD_STATE, SSD_LANES), 0)
    lane_s = lax.broadcasted_iota(jnp.int32, (1, SSD_LANES), 1) // SSD_HEAD_DIM
    b_off = 2 * D_MODEL
    c_off = b_off + SSD_GROUPS * SSD_STATE
    ssd_upd = {}

    def ssd_group(g):
        pair, gl = divmod(g, 2)
        b_tile = p_ref[:, b_off + pair * LANES:b_off + (pair + 1) * LANES].astype(BF16)
        c_tile = p_ref[:, c_off + pair * LANES:c_off + (pair + 1) * LANES]
        s_pair = ssd_scr[pair * LANES:(pair + 1) * LANES, :]
        gmask = lo_half if gl == 0 else jnp.logical_not(lo_half)
        c_g = jnp.where(gmask, c_tile, 0.0).astype(BF16)
        gmat = _dot_nt(c_g, b_tile)
        inter = _dot(c_g, s_pair.astype(BF16))
        yield
        wx_parts = []
        for hp in range(2):
            h0 = g * 4 + hp * 2
            col0 = bcum[:, dt_lo + h0:dt_lo + h0 + 1]
            col1 = bcum[:, dt_lo + h0 + 1:dt_lo + h0 + 2]
            dtc0 = dt[:, dt_lo + h0:dt_lo + h0 + 1]
            dtc1 = dt[:, dt_lo + h0 + 1:dt_lo + h0 + 2]
            xsl = slice(D_MODEL + h0 * SSD_HEAD_DIM, D_MODEL + (h0 + 2) * SSD_HEAD_DIM)
            xs = p_ref[:, xsl]
            xhat = xs * jnp.where(lo_half, dtc0, dtc1)
            xhat_bf = xhat.astype(BF16)
            yield
            outs = []
            for hh, col in ((0, col0), (1, col1)):
                rowv = bcum_t[dt_lo + h0 + hh:dt_lo + h0 + hh + 1, :]
                outs.append((yield from _ssd_intra(gmat, col, rowv, xhat_bf, tri, rev)))
            o_pair = jnp.where(lo_half, outs[0], outs[1])
            colp = jnp.where(lo_half, col0, col1)
            isl = slice(hp * LANES, (hp + 1) * LANES)
            o_pair = o_pair + jnp.exp2(colp) * inter[:, isl]
            if not rev:
                o_pair = o_pair + dskip_ref[:, h0 * SSD_HEAD_DIM:(h0 + 2) * SSD_HEAD_DIM] * xs
            o_ref[:, xsl] = o_pair
            yield
            t0 = tot[:, dt_lo + h0:dt_lo + h0 + 1]
            t1 = tot[:, dt_lo + h0 + 1:dt_lo + h0 + 2]
            totp = jnp.where(lane < SSD_HEAD_DIM, t0, t1)
            wx_parts.append((jnp.exp2(totp - colp) * xhat).astype(BF16))
        wx = jnp.concatenate(wx_parts, axis=1)
        ssd_upd[g] = _dot_tn(b_tile, wx)
        yield
        d = jnp.zeros((1, SSD_LANES), F32)
        for hl in range(4):
            th = tot[:, dt_lo + g * 4 + hl:dt_lo + g * 4 + hl + 1]
            d = jnp.where(lane_s == hl, jnp.exp2(th), d)
        ssd_upd[("dec", g)] = d
        if gl == 1:
            first = row_s < SSD_STATE
            new = jnp.where(first, ssd_upd[g - 1], ssd_upd[g])
            dec = jnp.where(first, ssd_upd[("dec", g - 1)], d)
            ssd_scr[pair * LANES:(pair + 1) * LANES, :] = dec * s_pair + new

    small_bf = small.astype(BF16)
    gla_bg = {}

    def gla_log_decay(hd):
        ksl = slice(hd * GLA_HEAD_K, (hd + 1) * GLA_HEAD_K)
        z = _dot(small_bf, w2_ref[:, ksl]) + gb_ref[:, ksl]
        yield
        ld = -_softplus_log(-z) * (LOG2E / GLA_GATE_NORM)
        yield
        gla_bg[hd] = _seg_cumsum(ld, GLA_SUB, rev)
        yield

    nsub = tc // GLA_SUB
    nblk = GLA_SUB // GLA_BLK
    scale = GLA_HEAD_K ** -0.5
    gla_st = [gla_scr[hd * GLA_HEAD_K:(hd + 1) * GLA_HEAD_K, :] for hd in range(GLA_HEADS)]

    def gla_subchunk(sc_i, hd):
        ksl = slice(hd * GLA_HEAD_K, (hd + 1) * GLA_HEAD_K)
        st = gla_st[hd]
        rsl = slice(sc_i * GLA_SUB, (sc_i + 1) * GLA_SUB)
        q = p_ref[rsl, Q_OFF + hd * GLA_HEAD_K:Q_OFF + (hd + 1) * GLA_HEAD_K] * scale
        k = p_ref[rsl, K_OFF + hd * GLA_HEAD_K:K_OFF + (hd + 1) * GLA_HEAD_K]
        v = p_ref[rsl, V_OFF + hd * GLA_HEAD_V:V_OFF + (hd + 1) * GLA_HEAD_V].astype(BF16)
        b = gla_bg[hd][rsl, :]
        inter = _dot((q * jnp.exp2(b)).astype(BF16), st.astype(BF16))
        yield
        o_rows = []
        for blk in range(nblk):
            bsl = slice(blk * GLA_BLK, (blk + 1) * GLA_BLK)
            if rev:
                keys = slice(blk * GLA_BLK, GLA_SUB)
                ref_row = b[(blk + 1) * GLA_BLK - 1:(blk + 1) * GLA_BLK, :]
            else:
                keys = slice(0, (blk + 1) * GLA_BLK)
                ref_row = b[blk * GLA_BLK:blk * GLA_BLK + 1, :]
            nk = keys.stop - keys.start
            qt = (q[bsl, :] * jnp.exp2(b[bsl, :] - ref_row)).astype(BF16)
            kt = (k[keys, :] * jnp.exp2(ref_row - b[keys, :])).astype(BF16)
            a_blk = _dot_nt(qt, kt)
            yield
            ai = lax.broadcasted_iota(jnp.int32, (GLA_BLK, nk), 0)
            aj = lax.broadcasted_iota(jnp.int32, (GLA_BLK, nk), 1)
            vis = (aj >= ai) if rev else (aj <= ai + blk * GLA_BLK)
            a_blk = jnp.where(vis, a_blk, 0.0).astype(BF16)
            o_rows.append(inter[bsl, :] + _dot(a_blk, v[keys, :]))
            yield
        o = jnp.concatenate(o_rows, axis=0)
        o_ref[rsl, 2 * D_MODEL + hd * GLA_HEAD_V:2 * D_MODEL + (hd + 1) * GLA_HEAD_V] = o
        end = b[0:1, :] if rev else b[GLA_SUB - 1:GLA_SUB, :]
        k_dec = (k * jnp.exp2(end - b)).astype(BF16)
        dec_col = jnp.broadcast_to(jnp.exp2(end), (GLA_HEAD_K, GLA_HEAD_K)).T[:, 0:1]
        gla_st[hd] = dec_col * st + _dot_tn(k_dec, v)
        yield

    def gla_head(hd):
        yield from gla_log_decay(hd)
        for step in range(nsub):
            yield from gla_subchunk((nsub - 1 - step) if rev else step, hd)

    def lru_all():
        for g in range(D_MODEL // MXU_DIM):
            yield from lru_group(g)

    def ssd_all():
        for g in range(SSD_GROUPS):
            yield from ssd_group(g)

    _round_robin([(ssd_all(), 1), (lru_all(), 1)] + [(gla_head(hd), 1) for hd in range(GLA_HEADS)])
    for hd in range(GLA_HEADS):
        gla_scr[hd * GLA_HEAD_K:(hd + 1) * GLA_HEAD_K, :] = gla_st[hd]

    if emit_state:
        @pl.when(c == nchunks - 1)
        def _():
            s_lru[...] = h_scr[0:1, :]
            for h in range(SSD_HEADS):
                rs, ls = _ssd_state_slices(h)
                s_ssd[h] = ssd_scr[rs, ls]
            s_gla[...] = gla_scr[...]


def _scan(p1, tok0, nseq, nchunks, lw, l, d, init, o_prev, st_prev, st_shapes):
    rev = d == 1
    tc = T_CHUNK
    blk0 = tok0 // tc
    m = p1.shape[0]

    def chunk_of(c):
        return (nchunks - 1 - c) if rev else c

    const2 = lambda b, c: (0, 0)
    const3 = lambda b, c: (0, 0, 0)
    in_specs = [
        pl.BlockSpec((tc, P1_COLS), lambda b, c: (blk0 + b * nchunks + chunk_of(c), 0)),
        pl.BlockSpec((D_MODEL // MXU_DIM, MXU_DIM, 2 * MXU_DIM), const3),
        pl.BlockSpec((1, D_MODEL), const2),
        pl.BlockSpec((1, D_MODEL), const2),
        pl.BlockSpec((1, D_MODEL), const2),
        pl.BlockSpec((1, LANES), const2),
        pl.BlockSpec((1, LANES), const2),
        pl.BlockSpec((1, D_MODEL), const2),
        pl.BlockSpec((LANES, GLA_DK), const2),
        pl.BlockSpec((1, GLA_DK), const2),
    ]
    args = [p1, lw["lru_w"][d], lw["lru_ba"][d], lw["lru_bx"][d], lw["lam"][d], lw["dt_bias"][d],
            lw["a_log"][d], lw["d_skip"], lw["gate_w2"][d], lw["gate_b"][d]]
    state_specs = [
        pl.BlockSpec((None, None, None, 1, D_MODEL), lambda b, c: (b, l, d, 0, 0)),
        pl.BlockSpec((None, None, None, SSD_HEADS, SSD_STATE, SSD_HEAD_DIM), lambda b, c: (b, l, d, 0, 0, 0)),
        pl.BlockSpec((None, None, None, GLA_ROWS, GLA_HEAD_V), lambda b, c: (b, l, d, 0, 0)),
    ]
    if init is not None:
        in_specs += state_specs
        args += list(init)
    aliases = {}
    if o_prev is not None:
        aliases[len(args)] = 0
        in_specs.append(pl.BlockSpec(memory_space=pl.ANY))
        args.append(o_prev)
    if st_prev is not None:
        for k, arr in enumerate(st_prev):
            aliases[len(args)] = 1 + k
            in_specs.append(pl.BlockSpec(memory_space=pl.ANY))
            args.append(arr)
    out_specs = [pl.BlockSpec((tc, 3 * D_MODEL), lambda b, c: (blk0 + b * nchunks + chunk_of(c), 0))]
    out_shape = [jax.ShapeDtypeStruct((m, 3 * D_MODEL), F32)]
    if st_shapes is not None:
        out_specs += state_specs
        out_shape += [jax.ShapeDtypeStruct(s, F32) for s in st_shapes]
    kern = functools.partial(_scan_kernel, rev=rev, zero_init=init is None, emit_state=st_shapes is not None,
                             alias_out=o_prev is not None, alias_state=st_prev is not None, nchunks=nchunks)
    return pl.pallas_call(
        kern,
        grid=(nseq, nchunks),
        in_specs=in_specs,
        out_specs=out_specs,
        out_shape=out_shape,
        input_output_aliases=aliases,
        scratch_shapes=[
            pltpu.VMEM((SUBLANES, D_MODEL), F32),
            pltpu.VMEM((SSD_ROWS, SSD_LANES), F32),
            pltpu.VMEM((GLA_ROWS, GLA_HEAD_V), F32),
        ],
        compiler_params=_params(("parallel", "arbitrary")),
        name="scan_rev" if rev else "scan_fwd",
    )(*args)


def _post_kernel(x_ref, of_ref, ob_ref, p2_ref, mod_ref, ssdg_ref, glag_ref,
                 wl_ref, ws_ref, wg_ref, wo_ref, o_ref):
    dm = D_MODEL
    hl = of_ref[:, 0:dm] + ob_ref[:, 0:dm]
    y_lru = hl * _gelu_tanh(p2_ref[:, 0:dm])
    ys = of_ref[:, dm:2 * dm] + ob_ref[:, dm:2 * dm]
    y_ssd = _rms(ys * _silu(p2_ref[:, dm:2 * dm]), ssdg_ref[...])
    parts = []
    for hd in range(GLA_HEADS):
        sl = slice(2 * dm + hd * GLA_HEAD_V, 2 * dm + (hd + 1) * GLA_HEAD_V)
        parts.append(_rms(of_ref[:, sl] + ob_ref[:, sl], glag_ref[...]))
    y_gla = jnp.concatenate(parts, axis=1) * _silu(p2_ref[:, 2 * dm:3 * dm])
    merged = (_sigmoid(p2_ref[:, 3 * dm:4 * dm]) * _dot(y_lru.astype(BF16), wl_ref[...])
              + _sigmoid(p2_ref[:, 4 * dm:5 * dm]) * _dot(y_ssd.astype(BF16), ws_ref[...])
              + _sigmoid(p2_ref[:, 5 * dm:6 * dm]) * _dot(y_gla.astype(BF16), wg_ref[...]))
    out = _dot(merged.astype(BF16), wo_ref[...])
    o_ref[...] = x_ref[...] + mod_ref[:, 2 * dm:3 * dm] * out


def _post(x, o_f, o_b, p2, mod, row_of_block, lw):
    m = x.shape[0]
    tm = TM_POST
    wspec = pl.BlockSpec((D_MODEL, D_MODEL), lambda i: (0, 0))
    return pl.pallas_call(
        _post_kernel,
        grid=(m // tm,),
        in_specs=[
            pl.BlockSpec((tm, D_MODEL), lambda i: (i, 0)),
            pl.BlockSpec((tm, 3 * D_MODEL), lambda i: (i, 0)),
            pl.BlockSpec((tm, 3 * D_MODEL), lambda i: (i, 0)),
            pl.BlockSpec((tm, P2_COLS), lambda i: (i, 0)),
            pl.BlockSpec((None, 1, 6 * D_MODEL), lambda i: (row_of_block(i, tm), 0, 0)),
            pl.BlockSpec((1, D_MODEL), lambda i: (0, 0)),
            pl.BlockSpec((1, GLA_HEAD_V), lambda i: (0, 0)),
            wspec, wspec, wspec, wspec,
        ],
        out_specs=pl.BlockSpec((tm, D_MODEL), lambda i: (i, 0)),
        out_shape=jax.ShapeDtypeStruct((m, D_MODEL), F32),
        compiler_params=_params(("parallel",)),
        name="post",
    )(x, o_f, o_b, p2, mod, lw["ssd_norm_g"], lw["gla_norm_g"],
      lw["w_br_lru"], lw["w_br_ssd"], lw["w_br_gla"], lw["w_out"])


def _ffn_kernel(x_ref, mod_ref, g_ref, wg_ref, wu_ref, wo_ref, fg_ref, o_ref, *, final_norm, nsplit):
    dm = D_MODEL
    x = x_ref[...]
    h = (_rms(x, g_ref[...]) * (1.0 + mod_ref[:, 4 * dm:5 * dm]) + mod_ref[:, 3 * dm:4 * dm]).astype(BF16)
    tf = D_FF // nsplit
    acc = jnp.zeros(x.shape, F32)
    for s in range(nsplit):
        fs = slice(s * tf, (s + 1) * tf)
        gate = _dot(h, wg_ref[:, fs])
        up = _dot(h, wu_ref[:, fs])
        acc = acc + _dot((_silu(gate) * up).astype(BF16), wo_ref[fs, :])
    y = x + mod_ref[:, 5 * dm:6 * dm] * acc
    if final_norm:
        y = _rms(y, fg_ref[...])
    o_ref[...] = y


def _ffn(x, mod, row_of_block, lw, final_g, final_norm):
    m = x.shape[0]
    tm = TM_POST
    single = dict(pipeline_mode=pl.Buffered(1))
    kern = functools.partial(_ffn_kernel, final_norm=final_norm, nsplit=2)
    return pl.pallas_call(
        kern,
        grid=(m // tm,),
        in_specs=[
            pl.BlockSpec((tm, D_MODEL), lambda i: (i, 0)),
            pl.BlockSpec((None, 1, 6 * D_MODEL), lambda i: (row_of_block(i, tm), 0, 0)),
            pl.BlockSpec((1, D_MODEL), lambda i: (0, 0)),
            pl.BlockSpec((D_MODEL, D_FF), lambda i: (0, 0), **single),
            pl.BlockSpec((D_MODEL, D_FF), lambda i: (0, 0), **single),
            pl.BlockSpec((D_FF, D_MODEL), lambda i: (0, 0), **single),
            pl.BlockSpec((1, D_MODEL), lambda i: (0, 0)),
        ],
        out_specs=pl.BlockSpec((tm, D_MODEL), lambda i: (i, 0)),
        out_shape=jax.ShapeDtypeStruct((m, D_MODEL), F32),
        compiler_params=_params(("parallel",)),
        name="ffn",
    )(x, mod, lw["norm_ffn_g"], lw["w_ffn_g"], lw["w_ffn_u"], lw["w_ffn_out"], final_g)


def _snake(x):
    bsz, t, d = x.shape
    grid = x.reshape(bsz, t // (2 * GRID_W), 2, GRID_W, d)
    return jnp.concatenate([grid[:, :, :1], jnp.flip(grid[:, :, 1:], 3)], axis=2).reshape(bsz, t, d)


def _layer_weights(l, w):
    dm = D_MODEL
    wi = w["w_in"][l]
    o_gate, o_z, o_xbc = dm, 2 * dm, 3 * dm
    o_dt = o_xbc + CONV_COLS - dm
    o_q = o_dt + 2 * SSD_HEADS
    o_k = o_q + GLA_DK
    o_v = o_k + GLA_DK
    o_g = o_v + dm
    o_lr = o_g + dm
    o_m = o_lr + 2 * GLA_GATE_RANK
    pad = jnp.zeros((dm, LANES - 2 * SSD_HEADS - 2 * GLA_GATE_RANK), wi.dtype)
    w1 = jnp.concatenate([wi[:, 0:dm], wi[:, o_xbc:o_dt], wi[:, o_q:o_g],
                          wi[:, o_dt:o_q], wi[:, o_lr:o_m], pad], axis=1).astype(BF16)
    w2 = jnp.concatenate([wi[:, o_gate:o_xbc], wi[:, o_g:o_lr], wi[:, o_m:]], axis=1).astype(BF16)

    def block_diag_tiles(wa, wx):
        per = MXU_DIM // LRU_BLOCK_W
        eye = jnp.eye(per, dtype=wa.dtype)
        def tiles(wb):
            wb = wb.reshape(dm // MXU_DIM, per, LRU_BLOCK_W, LRU_BLOCK_W)
            t = jnp.einsum("gpwv,pq->gpwqv", wb, eye)
            return t.reshape(dm // MXU_DIM, MXU_DIM, MXU_DIM)
        return jnp.concatenate([tiles(wa), tiles(wx)], axis=2).astype(BF16)

    def lane_pad(v, lo):
        return jnp.zeros((1, LANES), F32).at[0, lo:lo + v.shape[0]].set(v)

    def gate_w2(d):
        lo = SMALL_LR + d * GLA_GATE_RANK
        return jnp.zeros((LANES, GLA_DK), F32).at[lo:lo + GLA_GATE_RANK].set(w["gla_gate_w2"][l, d]).astype(BF16)

    return dict(
        w1=w1, w2=w2,
        norm_mix_g=w["norm_mix_g"][l][None], norm_ffn_g=w["norm_ffn_g"][l][None],
        conv_w=jnp.concatenate([w["lru_conv_w"][l], w["ssd_conv_w"][l]], axis=1),
        conv_b=jnp.concatenate([w["lru_conv_b"][l], w["ssd_conv_b"][l]])[None],
        lru_w=[block_diag_tiles(0.5 * w["lru_w_a"][l, d], 0.5 * w["lru_w_x"][l, d]) for d in range(2)],
        lru_ba=[0.5 * w["lru_b_a"][l, d][None] for d in range(2)],
        lru_bx=[0.5 * w["lru_b_x"][l, d][None] for d in range(2)],
        lam=[w["lru_lambda"][l, d][None] for d in range(2)],
        dt_bias=[lane_pad(w["ssd_dt_bias"][l, d], SMALL_DT + d * SSD_HEADS) for d in range(2)],
        a_log=[lane_pad(w["ssd_a_log"][l, d], SMALL_DT + d * SSD_HEADS) for d in range(2)],
        d_skip=jnp.repeat(w["ssd_d"][l], SSD_HEAD_DIM)[None],
        gate_w2=[gate_w2(d) for d in range(2)],
        gate_b=[w["gla_gate_b"][l, d][None] for d in range(2)],
        ssd_norm_g=w["ssd_norm_g"][l][None], gla_norm_g=w["gla_norm_g"][l][None],
        w_br_lru=w["w_br_lru"][l].astype(BF16), w_br_ssd=w["w_br_ssd"][l].astype(BF16),
        w_br_gla=w["w_br_gla"][l].astype(BF16), w_out=w["w_out"][l].astype(BF16),
        w_ffn_g=w["w_ffn_in"][l][:, :D_FF].astype(BF16), w_ffn_u=w["w_ffn_in"][l][:, D_FF:].astype(BF16),
        w_ffn_out=w["w_ffn_out"][l].astype(BF16),
    )


def kernel(x_prompt, x_sample, state_lru, state_ssd, state_gla, c, c_ctx, norm_mix_g, norm_ffn_g, w_ada, b_ada, w_in, lru_conv_w, lru_conv_b, lru_w_a, lru_b_a, lru_w_x, lru_b_x, lru_lambda, ssd_conv_w, ssd_conv_b, ssd_dt_bias, ssd_a_log, ssd_d, ssd_norm_g, gla_gate_w2, gla_gate_b, gla_norm_g, w_br_lru, w_br_ssd, w_br_gla, w_out, w_ffn_in, w_ffn_out, final_norm_g):
    w = dict(norm_mix_g=norm_mix_g, norm_ffn_g=norm_ffn_g, w_in=w_in, lru_conv_w=lru_conv_w,
             lru_conv_b=lru_conv_b, lru_w_a=lru_w_a, lru_b_a=lru_b_a, lru_w_x=lru_w_x, lru_b_x=lru_b_x,
             lru_lambda=lru_lambda, ssd_conv_w=ssd_conv_w, ssd_conv_b=ssd_conv_b, ssd_dt_bias=ssd_dt_bias,
             ssd_a_log=ssd_a_log, ssd_d=ssd_d, ssd_norm_g=ssd_norm_g, gla_gate_w2=gla_gate_w2,
             gla_gate_b=gla_gate_b, gla_norm_g=gla_norm_g, w_br_lru=w_br_lru, w_br_ssd=w_br_ssd,
             w_br_gla=w_br_gla, w_out=w_out, w_ffn_in=w_ffn_in, w_ffn_out=w_ffn_out)
    n_layers = w_in.shape[0]
    b_ctx, t_ctx, dm = x_prompt.shape
    b_lat, t_lat, _ = x_sample.shape
    m_ctx = b_ctx * t_ctx
    m_lat = b_lat * t_lat
    nch_ctx = t_ctx // T_CHUNK
    nch_lat = t_lat // T_CHUNK

    n_rows = -(-(1 + b_lat) // SUBLANES) * SUBLANES
    cond = jnp.zeros((n_rows, dm), F32).at[0].set(c_ctx).at[1:1 + b_lat].set(c)
    mod = _modulation(cond, w_ada, b_ada)

    def row_of_block(i, tm):
        tok = i * tm
        return jnp.where(tok < m_ctx, 0, 1 + (tok - m_ctx) // t_lat)

    def seq_pos(tok):
        in_ctx = tok < m_ctx
        pos = jnp.where(in_ctx, lax.rem(tok, t_ctx), lax.rem(tok - m_ctx, t_lat))
        return pos, jnp.where(in_ctx, t_ctx, t_lat)

    x = jnp.concatenate([x_prompt.reshape(m_ctx, dm), _snake(x_sample).reshape(m_lat, dm)], axis=0)

    st_shapes = [(b_ctx, n_layers, 2, 1, dm),
                 (b_ctx, n_layers, 2, SSD_HEADS, SSD_STATE, SSD_HEAD_DIM),
                 (b_ctx, n_layers, 2, GLA_ROWS, GLA_HEAD_V)]
    init = (state_lru.reshape(b_lat, n_layers, 2, 1, dm), state_ssd,
            state_gla.reshape(b_lat, n_layers, 2, GLA_ROWS, GLA_HEAD_V))
    states = None
    for l in range(n_layers):
        lw = _layer_weights(l, w)
        mod_l = mod[l][:, None, :]
        p1, p2 = _inproj(x, mod_l, row_of_block, seq_pos, lw)
        outs = []
        for d in range(2):
            o, *states = _scan(p1, 0, b_ctx, nch_ctx, lw, l, d, None, None, states, st_shapes)
            (o,) = _scan(p1, m_ctx, b_lat, nch_lat, lw, l, d, init, o, None, None)
            outs.append(o)
        x = _post(x, outs[0], outs[1], p2, mod_l, row_of_block, lw)
        x = _ffn(x, mod_l, row_of_block, lw, final_norm_g[None], l == n_layers - 1)

    y_prompt = x[:m_ctx].reshape(b_ctx, t_ctx, dm)
    y_sample = _snake(x[m_ctx:].reshape(b_lat, t_lat, dm))
    return (y_prompt, y_sample, states[0].reshape(b_ctx, n_layers, 2, dm), states[1],
            states[2].reshape(b_ctx, n_layers, 2, GLA_HEADS, GLA_HEAD_K, GLA_HEAD_V))
```

```python
import functools

import jax
import jax.numpy as jnp
from jax import lax
from jax.experimental import pallas as pl
from jax.experimental.pallas import tpu as pltpu

F32 = jnp.float32
BF16 = jnp.bfloat16

D_MODEL = 1024
GRID_W = 64
CONV_W = 4
CONV_LEFT = 2
LRU_BLOCK_W = 64
LRU_C = 8.0
SSD_HEAD_DIM = 64
SSD_HEADS = 16
SSD_GROUPS = 4
SSD_STATE = 64
GLA_HEADS = 4
GLA_HEAD_K = 128
GLA_HEAD_V = 256
GLA_DK = GLA_HEADS * GLA_HEAD_K
GLA_GATE_RANK = 16
GLA_GATE_NORM = 16.0
D_FF = 2816
EPS = 1e-6

LANES = 128
SUBLANES = 8
MXU_DIM = 256
VMEM_LIMIT_BYTES = 60 * 1024 * 1024

CONV_COLS = D_MODEL + D_MODEL + 2 * SSD_GROUPS * SSD_STATE
Q_OFF = CONV_COLS
K_OFF = Q_OFF + GLA_DK
V_OFF = K_OFF + GLA_DK
SMALL_OFF = V_OFF + D_MODEL
P1_COLS = SMALL_OFF + LANES
P2_COLS = 6 * D_MODEL
SMALL_DT = 0
SMALL_LR = 2 * SSD_HEADS

T_CHUNK = 256
IN_CHUNK = 512
TM_POST = 256
GLA_SUB = 64
GLA_BLK = 16
NEG_BIG = -1e30
LOG2E = 1.4426950408889634
SSD_ROWS = SSD_GROUPS * SSD_STATE
SSD_LANES = (SSD_HEADS // SSD_GROUPS) * SSD_HEAD_DIM
GLA_ROWS = GLA_HEADS * GLA_HEAD_K


def _params(sem, **kw):
    return pltpu.CompilerParams(dimension_semantics=sem, vmem_limit_bytes=VMEM_LIMIT_BYTES, **kw)


def _softplus(x):
    return jnp.maximum(x, 0.0) + jnp.log1p(jnp.exp(-jnp.abs(x)))


def _softplus_log(x):
    return jnp.maximum(x, 0.0) + jnp.log(1.0 + jnp.exp(-jnp.abs(x)))


def _sigmoid(x):
    return 0.5 * (1.0 + jnp.tanh(0.5 * x))


def _silu(x):
    return x * _sigmoid(x)


def _gelu_tanh(x):
    c = 0.7978845608028654
    return 0.5 * x * (1.0 + jnp.tanh(c * (x + 0.044715 * (x * x * x))))


def _rms(x, g):
    return x * lax.rsqrt(jnp.mean(x * x, axis=-1, keepdims=True) + EPS) * g


def _dot(a, b):
    return jnp.dot(a, b, preferred_element_type=F32)


def _dot_nt(a, b):
    return lax.dot_general(a, b, (((1,), (1,)), ((), ())), preferred_element_type=F32)


def _dot_tn(a, b):
    return lax.dot_general(a, b, (((0,), (0,)), ((), ())), preferred_element_type=F32)


def _mod_kernel(c_ref, w_ref, b_ref, o_ref):
    c = _silu(c_ref[...]).astype(BF16)
    o_ref[...] = _dot(c, w_ref[...].astype(BF16)) + b_ref[...]


def _modulation(cond, w_ada, b_ada):
    n_layers = w_ada.shape[0]
    rows = cond.shape[0]
    tn = D_MODEL
    return pl.pallas_call(
        _mod_kernel,
        grid=(n_layers, 6 * D_MODEL // tn),
        in_specs=[
            pl.BlockSpec((rows, D_MODEL), lambda l, j: (0, 0)),
            pl.BlockSpec((None, D_MODEL, tn), lambda l, j: (l, 0, j)),
            pl.BlockSpec((None, 1, tn), lambda l, j: (l, 0, j)),
        ],
        out_specs=pl.BlockSpec((None, rows, tn), lambda l, j: (l, 0, j)),
        out_shape=jax.ShapeDtypeStruct((n_layers, rows, 6 * D_MODEL), F32),
        compiler_params=_params(("parallel", "parallel")),
        name="modulation",
    )(cond, w_ada, b_ada.reshape(n_layers, 1, 6 * D_MODEL))


def _inproj_kernel(x_ref, xp_ref, xn_ref, mod_ref, g_ref, w1_ref, w2_ref, cw_ref, cb_ref,
                   p1_ref, p2_ref, hext, cbuf, obuf, *, seq_pos):
    tc = T_CHUNK
    sh = mod_ref[:, 0:D_MODEL]
    sc1 = 1.0 + mod_ref[:, D_MODEL:2 * D_MODEL]
    g = g_ref[...]
    h_main = _rms(x_ref[...], g) * sc1 + sh
    hext[SUBLANES:SUBLANES + tc, :] = h_main
    pos, seq_len = seq_pos(pl.program_id(0) * tc)
    head = jnp.where(pos == 0, 0.0, 1.0)
    tail = jnp.where(pos + tc == seq_len, 0.0, 1.0)
    hext[0:SUBLANES, :] = (_rms(xp_ref[...], g) * sc1 + sh) * head
    hext[SUBLANES + tc:, :] = (_rms(xn_ref[...], g) * sc1 + sh) * tail
    he = hext[...].astype(BF16)
    hb = h_main.astype(BF16)

    def conv_columns():
        half = tc // 2
        for c0 in range(0, CONV_COLS, IN_CHUNK):
            res = _dot(he, w1_ref[:, c0:c0 + IN_CHUNK])
            for k in range(IN_CHUNK // LANES):
                cbuf[c0 // LANES + k] = res[:, k * LANES:(k + 1) * LANES]
            yield
            for k in range(IN_CHUNK // LANES):
                slab = c0 // LANES + k
                ls = slice(slab * LANES, (slab + 1) * LANES)
                for parity in range(2):
                    xc = cb_ref[:, ls]
                    for j in range(CONV_W):
                        row0 = SUBLANES - CONV_LEFT + j + parity
                        xc = xc + cw_ref[j:j + 1, ls] * cbuf[slab, pl.ds(row0, half, stride=2), :]
                    obuf[slab, pl.ds(parity, half, stride=2), :] = xc if c0 < D_MODEL else _silu(xc)
                if k % 2 == 1:
                    yield
            for k in range(IN_CHUNK // LANES):
                slab = c0 // LANES + k
                p1_ref[:, slab * LANES:(slab + 1) * LANES] = obuf[slab]
            yield

    def plain_columns():
        for c0 in range(CONV_COLS, P1_COLS, IN_CHUNK):
            cs = slice(c0, min(c0 + IN_CHUNK, P1_COLS))
            p1_ref[:, cs] = _dot(hb, w1_ref[:, cs])
            yield
        for c0 in range(0, P2_COLS, IN_CHUNK):
            cs = slice(c0, c0 + IN_CHUNK)
            p2_ref[:, cs] = _dot(hb, w2_ref[:, cs]).astype(p2_ref.dtype)
            yield

    _round_robin([(conv_columns(), 1), (plain_columns(), 1)])


def _inproj(x, mod, row_of_block, seq_pos, lw):
    m = x.shape[0]
    tc = T_CHUNK
    rows8 = tc // SUBLANES
    last8 = m // SUBLANES - 1
    single = dict(pipeline_mode=pl.Buffered(1))
    return pl.pallas_call(
        functools.partial(_inproj_kernel, seq_pos=seq_pos),
        grid=(m // tc,),
        in_specs=[
            pl.BlockSpec((tc, D_MODEL), lambda i: (i, 0)),
            pl.BlockSpec((SUBLANES, D_MODEL), lambda i: (jnp.maximum(i * rows8 - 1, 0), 0)),
            pl.BlockSpec((SUBLANES, D_MODEL), lambda i: (jnp.minimum((i + 1) * rows8, last8), 0)),
            pl.BlockSpec((None, 1, 6 * D_MODEL), lambda i: (row_of_block(i, tc), 0, 0)),
            pl.BlockSpec((1, D_MODEL), lambda i: (0, 0)),
            pl.BlockSpec((D_MODEL, P1_COLS), lambda i: (0, 0), **single),
            pl.BlockSpec((D_MODEL, P2_COLS), lambda i: (0, 0), **single),
            pl.BlockSpec((CONV_W, CONV_COLS), lambda i: (0, 0)),
            pl.BlockSpec((1, CONV_COLS), lambda i: (0, 0)),
        ],
        out_specs=[pl.BlockSpec((tc, P1_COLS), lambda i: (i, 0)),
                   pl.BlockSpec((tc, P2_COLS), lambda i: (i, 0))],
        out_shape=[jax.ShapeDtypeStruct((m, P1_COLS), F32), jax.ShapeDtypeStruct((m, P2_COLS), BF16)],
        scratch_shapes=[pltpu.VMEM((tc + 2 * SUBLANES, D_MODEL), F32),
                        pltpu.VMEM((CONV_COLS // LANES, tc + 2 * SUBLANES, LANES), F32),
                        pltpu.VMEM((CONV_COLS // LANES, tc, LANES), F32)],
        compiler_params=_params(("parallel",)),
        name="inproj",
    )(x, x, x, mod, lw["norm_mix_g"], lw["w1"], lw["w2"], lw["conv_w"], lw["conv_b"])


def _tile_scan(a, b, rev):
    t, w = a.shape
    a = a.reshape(t // SUBLANES, SUBLANES, w)
    b = b.reshape(t // SUBLANES, SUBLANES, w)
    pos = lax.broadcasted_iota(jnp.int32, (1, SUBLANES, w), 1)
    s = 1
    while s < SUBLANES:
        shift = (SUBLANES - s) if rev else s
        valid = (pos < SUBLANES - s) if rev else (pos >= s)
        a_sh = jnp.where(valid, pltpu.roll(a, shift, 1), 1.0)
        b_sh = jnp.where(valid, pltpu.roll(b, shift, 1), 0.0)
        yield
        b = a * b_sh + b
        a = a * a_sh
        yield
        s *= 2
    return a, b


def _round_robin(tasks):
    tasks = list(tasks)
    while tasks:
        for task in list(tasks):
            gen, stages = task
            try:
                for _ in range(stages):
                    next(gen)
            except StopIteration:
                tasks.remove(task)


def _seg_cumsum(x, seg, rev):
    t, w = x.shape
    ntile = t // SUBLANES
    per_seg = seg // SUBLANES
    x = x.reshape(ntile, SUBLANES, w)
    pos = lax.broadcasted_iota(jnp.int32, (1, SUBLANES, w), 1)
    s = 1
    while s < SUBLANES:
        shift = (SUBLANES - s) if rev else s
        valid = (pos < SUBLANES - s) if rev else (pos >= s)
        x = x + jnp.where(valid, pltpu.roll(x, shift, 1), 0.0)
        s *= 2
    tiles = [None] * ntile
    for s0 in range(0, ntile, per_seg):
        carry = None
        for j in (range(s0 + per_seg - 1, s0 - 1, -1) if rev else range(s0, s0 + per_seg)):
            tiles[j] = x[j] if carry is None else x[j] + carry
            carry = tiles[j][0:1, :] if rev else tiles[j][SUBLANES - 1:SUBLANES, :]
    return jnp.stack(tiles, axis=0).reshape(t, w)


def _ssd_intra(gmat, col, rowv, xhat_bf, tri, rev):
    half = tri.shape[0]
    lo, hi = slice(0, half), slice(half, 2 * half)

    def blk(rs, cs, masked):
        e = col[rs, :] - rowv[:, cs]
        if masked:
            e = jnp.where(tri, e, NEG_BIG)
        return (gmat[rs, cs] * jnp.exp2(e)).astype(BF16)

    if rev:
        out_top = _dot(jnp.concatenate([blk(lo, lo, True), blk(lo, hi, False)], axis=1), xhat_bf)
        yield
        out_bot = _dot(blk(hi, hi, True), xhat_bf[hi, :])
    else:
        out_top = _dot(blk(lo, lo, True), xhat_bf[lo, :])
        yield
        out_bot = _dot(jnp.concatenate([blk(hi, lo, False), blk(hi, hi, True)], axis=1), xhat_bf)
    yield
    return jnp.concatenate([out_top, out_bot], axis=0)


def _ssd_state_slices(h):
    g, hl = divmod(h, SSD_HEADS // SSD_GROUPS)
    return (slice(g * SSD_STATE, (g + 1) * SSD_STATE),
            slice(hl * SSD_HEAD_DIM, (hl + 1) * SSD_HEAD_DIM))


def _scan_kernel(*refs, rev, zero_init, emit_state, alias_out, alias_state, nchunks):
    it = iter(refs)
    p_ref = next(it)
    lruw_ref, lruba_ref, lrubx_ref, lam_ref = next(it), next(it), next(it), next(it)
    dtb_ref, alog_ref, dskip_ref = next(it), next(it), next(it)
    w2_ref, gb_ref = next(it), next(it)
    if not zero_init:
        i_lru, i_ssd, i_gla = next(it), next(it), next(it)
    if alias_out:
        next(it)
    if alias_state:
        next(it), next(it), next(it)
    o_ref = next(it)
    if emit_state:
        s_lru, s_ssd, s_gla = next(it), next(it), next(it)
    h_scr, ssd_scr, gla_scr = next(it), next(it), next(it)

    tc = T_CHUNK
    c = pl.program_id(1)

    @pl.when(c == 0)
    def _():
        if zero_init:
            h_scr[...] = jnp.zeros_like(h_scr)
            ssd_scr[...] = jnp.zeros_like(ssd_scr)
            gla_scr[...] = jnp.zeros_like(gla_scr)
        else:
            h_scr[...] = jnp.broadcast_to(i_lru[...], h_scr.shape)
            for h in range(SSD_HEADS):
                rs, ls = _ssd_state_slices(h)
                ssd_scr[rs, ls] = i_ssd[h]
            gla_scr[...] = i_gla[...]

    small = p_ref[:, SMALL_OFF:SMALL_OFF + LANES]

    c8h = (-0.5 * LRU_C) * _softplus(-lam_ref[...])
    ntile = tc // SUBLANES

    def lru_group(g):
        sl = slice(g * MXU_DIM, (g + 1) * MXU_DIM)
        xl = p_ref[:, sl]
        pre = _dot(xl.astype(BF16), lruw_ref[g])
        yield
        t_r = jnp.tanh(pre[:, :MXU_DIM] + lruba_ref[:, sl])
        yield
        t_i = jnp.tanh(pre[:, MXU_DIM:] + lrubx_ref[:, sl])
        yield
        log_a = c8h[:, sl] * t_r + c8h[:, sl]
        xlh = 0.5 * xl
        ix = xlh * t_i + xlh
        yield
        a = jnp.exp(log_a)
        th = jnp.tanh(log_a)
        yield
        b = jnp.sqrt(-th * (1.0 + a * a)) * ix
        yield
        a_cum, b_cum = yield from _tile_scan(a, b, rev)
        carry = h_scr[0:1, sl]
        hs = [None] * ntile
        for n, ti in enumerate(range(ntile - 1, -1, -1) if rev else range(ntile)):
            hs[ti] = a_cum[ti] * carry + b_cum[ti]
            carry = hs[ti][0:1, :] if rev else hs[ti][SUBLANES - 1:SUBLANES, :]
            if n % 4 == 3:
                yield
        h_scr[:, sl] = jnp.broadcast_to(carry, (SUBLANES, MXU_DIM))
        o_ref[:, sl] = jnp.stack(hs, axis=0).reshape(tc, MXU_DIM).astype(o_ref.dtype)

    lane = lax.broadcasted_iota(jnp.int32, (1, LANES), 1)
    dt_lo = SMALL_DT + (SSD_HEADS if rev else 0)
    dt_mask = (lane >= dt_lo) & (lane < dt_lo + SSD_HEADS)
    dt = jnp.where(dt_mask, _softplus_log(small + dtb_ref[...]), 0.0)
    la = dt * (-LOG2E * jnp.exp(alog_ref[...]))
    bcum = _seg_cumsum(la, tc, rev)
    bcum_t = bcum.T
    tot = bcum[0:1, :] if rev else bcum[tc - 1:tc, :]

    half = tc // 2
    ri = lax.broadcasted_iota(jnp.int32, (half, half), 0)
    ci = lax.broadcasted_iota(jnp.int32, (half, half), 1)
    tri = (ci >= ri) if rev else (ci <= ri)
    lane_t = lax.broadcasted_iota(jnp.int32, (tc, LANES), 1)
    lo_half = lane_t < SSD_HEAD_DIM
    row_s = lax.broadcasted_iota(jnp.int32, (2 * SSD_STATE, SSD_LANES), 0)
    lane_s = lax.broadcasted_iota(jnp.int32, (1, SSD_LANES), 1) // SSD_HEAD_DIM
    b_off = 2 * D_MODEL
    c_off = b_off + SSD_GROUPS * SSD_STATE
    ssd_upd = {}

    def ssd_group(g):
        pair, gl = divmod(g, 2)
        b_tile = p_ref[:, b_off + pair * LANES:b_off + (pair + 1) * LANES].astype(BF16)
        c_tile = p_ref[:, c_off + pair * LANES:c_off + (pair + 1) * LANES]
        s_pair = ssd_scr[pair * LANES:(pair + 1) * LANES, :]
        gmask = lo_half if gl == 0 else jnp.logical_not(lo_half)
        c_g = jnp.where(gmask, c_tile, 0.0).astype(BF16)
        gmat = _dot_nt(c_g, b_tile)
        inter = _dot(c_g, s_pair.astype(BF16))
        yield
        wx_parts = []
        for hp in range(2):
            h0 = g * 4 + hp * 2
            col0 = bcum[:, dt_lo + h0:dt_lo + h0 + 1]
            col1 = bcum[:, dt_lo + h0 + 1:dt_lo + h0 + 2]
            dtc0 = dt[:, dt_lo + h0:dt_lo + h0 + 1]
            dtc1 = dt[:, dt_lo + h0 + 1:dt_lo + h0 + 2]
            xsl = slice(D_MODEL + h0 * SSD_HEAD_DIM, D_MODEL + (h0 + 2) * SSD_HEAD_DIM)
            xs = p_ref[:, xsl]
            xhat = xs * jnp.where(lo_half, dtc0, dtc1)
            xhat_bf = xhat.astype(BF16)
            yield
            outs = []
            for hh, col in ((0, col0), (1, col1)):
                rowv = bcum_t[dt_lo + h0 + hh:dt_lo + h0 + hh + 1, :]
                outs.append((yield from _ssd_intra(gmat, col, rowv, xhat_bf, tri, rev)))
            o_pair = jnp.where(lo_half, outs[0], outs[1])
            colp = jnp.where(lo_half, col0, col1)
            isl = slice(hp * LANES, (hp + 1) * LANES)
            o_pair = o_pair + jnp.exp2(colp) * inter[:, isl]
            if not rev:
                o_pair = o_pair + dskip_ref[:, h0 * SSD_HEAD_DIM:(h0 + 2) * SSD_HEAD_DIM] * xs
            o_ref[:, xsl] = o_pair.astype(o_ref.dtype)
            yield
            t0 = tot[:, dt_lo + h0:dt_lo + h0 + 1]
            t1 = tot[:, dt_lo + h0 + 1:dt_lo + h0 + 2]
            totp = jnp.where(lane < SSD_HEAD_DIM, t0, t1)
            wx_parts.append((jnp.exp2(totp - colp) * xhat).astype(BF16))
        wx = jnp.concatenate(wx_parts, axis=1)
        ssd_upd[g] = _dot_tn(b_tile, wx)
        yield
        d = jnp.zeros((1, SSD_LANES), F32)
        for hl in range(4):
            th = tot[:, dt_lo + g * 4 + hl:dt_lo + g * 4 + hl + 1]
            d = jnp.where(lane_s == hl, jnp.exp2(th), d)
        ssd_upd[("dec", g)] = d
        if gl == 1:
            first = row_s < SSD_STATE
            new = jnp.where(first, ssd_upd[g - 1], ssd_upd[g])
            dec = jnp.where(first, ssd_upd[("dec", g - 1)], d)
            ssd_scr[pair * LANES:(pair + 1) * LANES, :] = dec * s_pair + new

    small_bf = small.astype(BF16)
    gla_bg = {}

    def gla_log_decay(hd):
        ksl = slice(hd * GLA_HEAD_K, (hd + 1) * GLA_HEAD_K)
        z = _dot(small_bf, w2_ref[:, ksl]) + gb_ref[:, ksl]
        yield
        ld = -_softplus_log(-z) * (LOG2E / GLA_GATE_NORM)
        yield
        gla_bg[hd] = _seg_cumsum(ld, GLA_SUB, rev)
        yield

    nsub = tc // GLA_SUB
    nblk = GLA_SUB // GLA_BLK
    scale = GLA_HEAD_K ** -0.5
    gla_st = [gla_scr[hd * GLA_HEAD_K:(hd + 1) * GLA_HEAD_K, :] for hd in range(GLA_HEADS)]

    def gla_subchunk(sc_i, hd):
        ksl = slice(hd * GLA_HEAD_K, (hd + 1) * GLA_HEAD_K)
        st = gla_st[hd]
        rsl = slice(sc_i * GLA_SUB, (sc_i + 1) * GLA_SUB)
        q = p_ref[rsl, Q_OFF + hd * GLA_HEAD_K:Q_OFF + (hd + 1) * GLA_HEAD_K] * scale
        k = p_ref[rsl, K_OFF + hd * GLA_HEAD_K:K_OFF + (hd + 1) * GLA_HEAD_K]
        v = p_ref[rsl, V_OFF + hd * GLA_HEAD_V:V_OFF + (hd + 1) * GLA_HEAD_V].astype(BF16)
        b = gla_bg[hd][rsl, :]
        inter = _dot((q * jnp.exp2(b)).astype(BF16), st.astype(BF16))
        yield
        o_rows = []
        for blk in range(nblk):
            bsl = slice(blk * GLA_BLK, (blk + 1) * GLA_BLK)
            if rev:
                keys = slice(blk * GLA_BLK, GLA_SUB)
                ref_row = b[(blk + 1) * GLA_BLK - 1:(blk + 1) * GLA_BLK, :]
            else:
                keys = slice(0, (blk + 1) * GLA_BLK)
                ref_row = b[blk * GLA_BLK:blk * GLA_BLK + 1, :]
            nk = keys.stop - keys.start
            qt = (q[bsl, :] * jnp.exp2(b[bsl, :] - ref_row)).astype(BF16)
            kt = (k[keys, :] * jnp.exp2(ref_row - b[keys, :])).astype(BF16)
            a_blk = _dot_nt(qt, kt)
            yield
            ai = lax.broadcasted_iota(jnp.int32, (GLA_BLK, nk), 0)
            aj = lax.broadcasted_iota(jnp.int32, (GLA_BLK, nk), 1)
            vis = (aj >= ai) if rev else (aj <= ai + blk * GLA_BLK)
            a_blk = jnp.where(vis, a_blk, 0.0).astype(BF16)
            o_rows.append(inter[bsl, :] + _dot(a_blk, v[keys, :]))
            yield
        o = jnp.concatenate(o_rows, axis=0)
        o_ref[rsl, 2 * D_MODEL + hd * GLA_HEAD_V:2 * D_MODEL + (hd + 1) * GLA_HEAD_V] = o.astype(o_ref.dtype)
        end = b[0:1, :] if rev else b[GLA_SUB - 1:GLA_SUB, :]
        k_dec = (k * jnp.exp2(end - b)).astype(BF16)
        dec_col = jnp.broadcast_to(jnp.exp2(end), (GLA_HEAD_K, GLA_HEAD_K)).T[:, 0:1]
        gla_st[hd] = dec_col * st + _dot_tn(k_dec, v)
        yield

    def gla_head(hd):
        yield from gla_log_decay(hd)
        for step in range(nsub):
            yield from gla_subchunk((nsub - 1 - step) if rev else step, hd)

    def lru_all():
        for g in range(D_MODEL // MXU_DIM):
            yield from lru_group(g)

    def ssd_all():
        for g in range(SSD_GROUPS):
            yield from ssd_group(g)

    _round_robin([(gla_head(hd), 1) for hd in range(GLA_HEADS)] + [(ssd_all(), 2), (lru_all(), 2)])
    for hd in range(GLA_HEADS):
        gla_scr[hd * GLA_HEAD_K:(hd + 1) * GLA_HEAD_K, :] = gla_st[hd]

    if emit_state:
        @pl.when(c == nchunks - 1)
        def _():
            s_lru[...] = h_scr[0:1, :]
            for h in range(SSD_HEADS):
                rs, ls = _ssd_state_slices(h)
                s_ssd[h] = ssd_scr[rs, ls]
            s_gla[...] = gla_scr[...]


def _scan(p1, tok0, nseq, nchunks, lw, l, d, init, o_prev, st_prev, st_shapes):
    rev = d == 1
    tc = T_CHUNK
    blk0 = tok0 // tc
    m = p1.shape[0]

    def chunk_of(c):
        return (nchunks - 1 - c) if rev else c

    const2 = lambda b, c: (0, 0)
    const3 = lambda b, c: (0, 0, 0)
    in_specs = [
        pl.BlockSpec((tc, P1_COLS), lambda b, c: (blk0 + b * nchunks + chunk_of(c), 0)),
        pl.BlockSpec((D_MODEL // MXU_DIM, MXU_DIM, 2 * MXU_DIM), const3),
        pl.BlockSpec((1, D_MODEL), const2),
        pl.BlockSpec((1, D_MODEL), const2),
        pl.BlockSpec((1, D_MODEL), const2),
        pl.BlockSpec((1, LANES), const2),
        pl.BlockSpec((1, LANES), const2),
        pl.BlockSpec((1, D_MODEL), const2),
        pl.BlockSpec((LANES, GLA_DK), const2),
        pl.BlockSpec((1, GLA_DK), const2),
    ]
    args = [p1, lw["lru_w"][d], lw["lru_ba"][d], lw["lru_bx"][d], lw["lam"][d], lw["dt_bias"][d],
            lw["a_log"][d], lw["d_skip"], lw["gate_w2"][d], lw["gate_b"][d]]
    state_specs = [
        pl.BlockSpec((None, None, None, 1, D_MODEL), lambda b, c: (b, l, d, 0, 0)),
        pl.BlockSpec((None, None, None, SSD_HEADS, SSD_STATE, SSD_HEAD_DIM), lambda b, c: (b, l, d, 0, 0, 0)),
        pl.BlockSpec((None, None, None, GLA_ROWS, GLA_HEAD_V), lambda b, c: (b, l, d, 0, 0)),
    ]
    if init is not None:
        in_specs += state_specs
        args += list(init)
    aliases = {}
    if o_prev is not None:
        aliases[len(args)] = 0
        in_specs.append(pl.BlockSpec(memory_space=pl.ANY))
        args.append(o_prev)
    if st_prev is not None:
        for k, arr in enumerate(st_prev):
            aliases[len(args)] = 1 + k
            in_specs.append(pl.BlockSpec(memory_space=pl.ANY))
            args.append(arr)
    out_specs = [pl.BlockSpec((tc, 3 * D_MODEL), lambda b, c: (blk0 + b * nchunks + chunk_of(c), 0))]
    out_shape = [jax.ShapeDtypeStruct((m, 3 * D_MODEL), BF16)]
    if st_shapes is not None:
        out_specs += state_specs
        out_shape += [jax.ShapeDtypeStruct(s, F32) for s in st_shapes]
    kern = functools.partial(_scan_kernel, rev=rev, zero_init=init is None, emit_state=st_shapes is not None,
                             alias_out=o_prev is not None, alias_state=st_prev is not None, nchunks=nchunks)
    return pl.pallas_call(
        kern,
        grid=(nseq, nchunks),
        in_specs=in_specs,
        out_specs=out_specs,
        out_shape=out_shape,
        input_output_aliases=aliases,
        scratch_shapes=[
            pltpu.VMEM((SUBLANES, D_MODEL), F32),
            pltpu.VMEM((SSD_ROWS, SSD_LANES), F32),
            pltpu.VMEM((GLA_ROWS, GLA_HEAD_V), F32),
        ],
        compiler_params=_params(("parallel", "arbitrary")),
        name="scan_rev" if rev else "scan_fwd",
    )(*args)


def _post_kernel(x_ref, of_ref, ob_ref, p2_ref, mod_ref, ssdg_ref, glag_ref,
                 wl_ref, ws_ref, wg_ref, wo_ref, o_ref):
    dm = D_MODEL

    def both(sl):
        return of_ref[:, sl].astype(F32) + ob_ref[:, sl].astype(F32)

    def p2(k):
        return p2_ref[:, k * dm:(k + 1) * dm].astype(F32)

    y_lru = both(slice(0, dm)) * _gelu_tanh(p2(0))
    y_ssd = _rms(both(slice(dm, 2 * dm)) * _silu(p2(1)), ssdg_ref[...])
    parts = []
    for hd in range(GLA_HEADS):
        sl = slice(2 * dm + hd * GLA_HEAD_V, 2 * dm + (hd + 1) * GLA_HEAD_V)
        parts.append(_rms(both(sl), glag_ref[...]))
    y_gla = jnp.concatenate(parts, axis=1) * _silu(p2(2))
    merged = (_sigmoid(p2(3)) * _dot(y_lru.astype(BF16), wl_ref[...])
              + _sigmoid(p2(4)) * _dot(y_ssd.astype(BF16), ws_ref[...])
              + _sigmoid(p2(5)) * _dot(y_gla.astype(BF16), wg_ref[...]))
    out = _dot(merged.astype(BF16), wo_ref[...])
    o_ref[...] = x_ref[...] + mod_ref[:, 2 * dm:3 * dm] * out


def _post(x, o_f, o_b, p2, mod, row_of_block, lw):
    m = x.shape[0]
    tm = TM_POST
    wspec = pl.BlockSpec((D_MODEL, D_MODEL), lambda i: (0, 0))
    return pl.pallas_call(
        _post_kernel,
        grid=(m // tm,),
        in_specs=[
            pl.BlockSpec((tm, D_MODEL), lambda i: (i, 0)),
            pl.BlockSpec((tm, 3 * D_MODEL), lambda i: (i, 0)),
            pl.BlockSpec((tm, 3 * D_MODEL), lambda i: (i, 0)),
            pl.BlockSpec((tm, P2_COLS), lambda i: (i, 0)),
            pl.BlockSpec((None, 1, 6 * D_MODEL), lambda i: (row_of_block(i, tm), 0, 0)),
            pl.BlockSpec((1, D_MODEL), lambda i: (0, 0)),
            pl.BlockSpec((1, GLA_HEAD_V), lambda i: (0, 0)),
            wspec, wspec, wspec, wspec,
        ],
        out_specs=pl.BlockSpec((tm, D_MODEL), lambda i: (i, 0)),
        out_shape=jax.ShapeDtypeStruct((m, D_MODEL), F32),
        compiler_params=_params(("parallel",)),
        name="post",
    )(x, o_f, o_b, p2, mod, lw["ssd_norm_g"], lw["gla_norm_g"],
      lw["w_br_lru"], lw["w_br_ssd"], lw["w_br_gla"], lw["w_out"])


def _ffn_kernel(x_ref, mod_ref, g_ref, wg_ref, wu_ref, wo_ref, fg_ref, o_ref, *, final_norm, nsplit):
    dm = D_MODEL
    x = x_ref[...]
    h = (_rms(x, g_ref[...]) * (1.0 + mod_ref[:, 4 * dm:5 * dm]) + mod_ref[:, 3 * dm:4 * dm]).astype(BF16)
    tf = D_FF // nsplit
    acc = jnp.zeros(x.shape, F32)
    for s in range(nsplit):
        fs = slice(s * tf, (s + 1) * tf)
        gate = _dot(h, wg_ref[:, fs])
        up = _dot(h, wu_ref[:, fs])
        acc = acc + _dot((_silu(gate) * up).astype(BF16), wo_ref[fs, :])
    y = x + mod_ref[:, 5 * dm:6 * dm] * acc
    if final_norm:
        y = _rms(y, fg_ref[...])
    o_ref[...] = y


def _ffn(x, mod, row_of_block, lw, final_g, final_norm):
    m = x.shape[0]
    tm = TM_POST
    single = dict(pipeline_mode=pl.Buffered(1))
    kern = functools.partial(_ffn_kernel, final_norm=final_norm, nsplit=2)
    return pl.pallas_call(
        kern,
        grid=(m // tm,),
        in_specs=[
            pl.BlockSpec((tm, D_MODEL), lambda i: (i, 0)),
            pl.BlockSpec((None, 1, 6 * D_MODEL), lambda i: (row_of_block(i, tm), 0, 0)),
            pl.BlockSpec((1, D_MODEL), lambda i: (0, 0)),
            pl.BlockSpec((D_MODEL, D_FF), lambda i: (0, 0), **single),
            pl.BlockSpec((D_MODEL, D_FF), lambda i: (0, 0), **single),
            pl.BlockSpec((D_FF, D_MODEL), lambda i: (0, 0), **single),
            pl.BlockSpec((1, D_MODEL), lambda i: (0, 0)),
        ],
        out_specs=pl.BlockSpec((tm, D_MODEL), lambda i: (i, 0)),
        out_shape=jax.ShapeDtypeStruct((m, D_MODEL), F32),
        compiler_params=_params(("parallel",)),
        name="ffn",
    )(x, mod, lw["norm_ffn_g"], lw["w_ffn_g"], lw["w_ffn_u"], lw["w_ffn_out"], final_g)


def _snake(x):
    bsz, t, d = x.shape
    grid = x.reshape(bsz, t // (2 * GRID_W), 2, GRID_W, d)
    return jnp.concatenate([grid[:, :, :1], jnp.flip(grid[:, :, 1:], 3)], axis=2).reshape(bsz, t, d)


def _layer_weights(l, w):
    dm = D_MODEL
    wi = w["w_in"][l]
    o_gate, o_z, o_xbc = dm, 2 * dm, 3 * dm
    o_dt = o_xbc + CONV_COLS - dm
    o_q = o_dt + 2 * SSD_HEADS
    o_k = o_q + GLA_DK
    o_v = o_k + GLA_DK
    o_g = o_v + dm
    o_lr = o_g + dm
    o_m = o_lr + 2 * GLA_GATE_RANK
    pad = jnp.zeros((dm, LANES - 2 * SSD_HEADS - 2 * GLA_GATE_RANK), wi.dtype)
    w1 = jnp.concatenate([wi[:, 0:dm], wi[:, o_xbc:o_dt], wi[:, o_q:o_g],
                          wi[:, o_dt:o_q], wi[:, o_lr:o_m], pad], axis=1).astype(BF16)
    w2 = jnp.concatenate([wi[:, o_gate:o_xbc], wi[:, o_g:o_lr], wi[:, o_m:]], axis=1).astype(BF16)

    def block_diag_tiles(wa, wx):
        per = MXU_DIM // LRU_BLOCK_W
        eye = jnp.eye(per, dtype=wa.dtype)
        def tiles(wb):
            wb = wb.reshape(dm // MXU_DIM, per, LRU_BLOCK_W, LRU_BLOCK_W)
            t = jnp.einsum("gpwv,pq->gpwqv", wb, eye)
            return t.reshape(dm // MXU_DIM, MXU_DIM, MXU_DIM)
        return jnp.concatenate([tiles(wa), tiles(wx)], axis=2).astype(BF16)

    def lane_pad(v, lo):
        return jnp.zeros((1, LANES), F32).at[0, lo:lo + v.shape[0]].set(v)

    def gate_w2(d):
        lo = SMALL_LR + d * GLA_GATE_RANK
        return jnp.zeros((LANES, GLA_DK), F32).at[lo:lo + GLA_GATE_RANK].set(w["gla_gate_w2"][l, d]).astype(BF16)

    return dict(
        w1=w1, w2=w2,
        norm_mix_g=w["norm_mix_g"][l][None], norm_ffn_g=w["norm_ffn_g"][l][None],
        conv_w=jnp.concatenate([w["lru_conv_w"][l], w["ssd_conv_w"][l]], axis=1),
        conv_b=jnp.concatenate([w["lru_conv_b"][l], w["ssd_conv_b"][l]])[None],
        lru_w=[block_diag_tiles(0.5 * w["lru_w_a"][l, d], 0.5 * w["lru_w_x"][l, d]) for d in range(2)],
        lru_ba=[0.5 * w["lru_b_a"][l, d][None] for d in range(2)],
        lru_bx=[0.5 * w["lru_b_x"][l, d][None] for d in range(2)],
        lam=[w["lru_lambda"][l, d][None] for d in range(2)],
        dt_bias=[lane_pad(w["ssd_dt_bias"][l, d], SMALL_DT + d * SSD_HEADS) for d in range(2)],
        a_log=[lane_pad(w["ssd_a_log"][l, d], SMALL_DT + d * SSD_HEADS) for d in range(2)],
        d_skip=jnp.repeat(w["ssd_d"][l], SSD_HEAD_DIM)[None],
        gate_w2=[gate_w2(d) for d in range(2)],
        gate_b=[w["gla_gate_b"][l, d][None] for d in range(2)],
        ssd_norm_g=w["ssd_norm_g"][l][None], gla_norm_g=w["gla_norm_g"][l][None],
        w_br_lru=w["w_br_lru"][l].astype(BF16), w_br_ssd=w["w_br_ssd"][l].astype(BF16),
        w_br_gla=w["w_br_gla"][l].astype(BF16), w_out=w["w_out"][l].astype(BF16),
        w_ffn_g=w["w_ffn_in"][l][:, :D_FF].astype(BF16), w_ffn_u=w["w_ffn_in"][l][:, D_FF:].astype(BF16),
        w_ffn_out=w["w_ffn_out"][l].astype(BF16),
    )


def kernel(x_prompt, x_sample, state_lru, state_ssd, state_gla, c, c_ctx, norm_mix_g, norm_ffn_g, w_ada, b_ada, w_in, lru_conv_w, lru_conv_b, lru_w_a, lru_b_a, lru_w_x, lru_b_x, lru_lambda, ssd_conv_w, ssd_conv_b, ssd_dt_bias, ssd_a_log, ssd_d, ssd_norm_g, gla_gate_w2, gla_gate_b, gla_norm_g, w_br_lru, w_br_ssd, w_br_gla, w_out, w_ffn_in, w_ffn_out, final_norm_g):
    w = dict(norm_mix_g=norm_mix_g, norm_ffn_g=norm_ffn_g, w_in=w_in, lru_conv_w=lru_conv_w,
             lru_conv_b=lru_conv_b, lru_w_a=lru_w_a, lru_b_a=lru_b_a, lru_w_x=lru_w_x, lru_b_x=lru_b_x,
             lru_lambda=lru_lambda, ssd_conv_w=ssd_conv_w, ssd_conv_b=ssd_conv_b, ssd_dt_bias=ssd_dt_bias,
             ssd_a_log=ssd_a_log, ssd_d=ssd_d, ssd_norm_g=ssd_norm_g, gla_gate_w2=gla_gate_w2,
             gla_gate_b=gla_gate_b, gla_norm_g=gla_norm_g, w_br_lru=w_br_lru, w_br_ssd=w_br_ssd,
             w_br_gla=w_br_gla, w_out=w_out, w_ffn_in=w_ffn_in, w_ffn_out=w_ffn_out)
    n_layers = w_in.shape[0]
    b_ctx, t_ctx, dm = x_prompt.shape
    b_lat, t_lat, _ = x_sample.shape
    m_ctx = b_ctx * t_ctx
    m_lat = b_lat * t_lat
    nch_ctx = t_ctx // T_CHUNK
    nch_lat = t_lat // T_CHUNK

    n_rows = -(-(1 + b_lat) // SUBLANES) * SUBLANES
    cond = jnp.zeros((n_rows, dm), F32).at[0].set(c_ctx).at[1:1 + b_lat].set(c)
    mod = _modulation(cond, w_ada, b_ada)

    def row_of_block(i, tm):
        tok = i * tm
        return jnp.where(tok < m_ctx, 0, 1 + (tok - m_ctx) // t_lat)

    def seq_pos(tok):
        in_ctx = tok < m_ctx
        pos = jnp.where(in_ctx, lax.rem(tok, t_ctx), lax.rem(tok - m_ctx, t_lat))
        return pos, jnp.where(in_ctx, t_ctx, t_lat)

    x = jnp.concatenate([x_prompt.reshape(m_ctx, dm), _snake(x_sample).reshape(m_lat, dm)], axis=0)

    st_shapes = [(b_ctx, n_layers, 2, 1, dm),
                 (b_ctx, n_layers, 2, SSD_HEADS, SSD_STATE, SSD_HEAD_DIM),
                 (b_ctx, n_layers, 2, GLA_ROWS, GLA_HEAD_V)]
    init = (state_lru.reshape(b_lat, n_layers, 2, 1, dm), state_ssd,
            state_gla.reshape(b_lat, n_layers, 2, GLA_ROWS, GLA_HEAD_V))
    states = None
    for l in range(n_layers):
        lw = _layer_weights(l, w)
        mod_l = mod[l][:, None, :]
        p1, p2 = _inproj(x, mod_l, row_of_block, seq_pos, lw)
        outs = []
        for d in range(2):
            o, *states = _scan(p1, 0, b_ctx, nch_ctx, lw, l, d, None, None, states, st_shapes)
            (o,) = _scan(p1, m_ctx, b_lat, nch_lat, lw, l, d, init, o, None, None)
            outs.append(o)
        x = _post(x, outs[0], outs[1], p2, mod_l, row_of_block, lw)
        x = _ffn(x, mod_l, row_of_block, lw, final_norm_g[None], l == n_layers - 1)

    y_prompt = x[:m_ctx].reshape(b_ctx, t_ctx, dm)
    y_sample = _snake(x[m_ctx:].reshape(b_lat, t_lat, dm))
    return (y_prompt, y_sample, states[0].reshape(b_ctx, n_layers, 2, dm), states[1],
            states[2].reshape(b_ctx, n_layers, 2, GLA_HEADS, GLA_HEAD_K, GLA_HEAD_V))
```

```python
import functools

import jax
import jax.numpy as jnp
from jax import lax
from jax.experimental import pallas as pl
from jax.experimental.pallas import tpu as pltpu

F32 = jnp.float32
BF16 = jnp.bfloat16

D_MODEL = 1024
GRID_W = 64
CONV_W = 4
CONV_LEFT = 2
LRU_BLOCK_W = 64
LRU_C = 8.0
SSD_HEAD_DIM = 64
SSD_HEADS = 16
SSD_GROUPS = 4
SSD_STATE = 64
GLA_HEADS = 4
GLA_HEAD_K = 128
GLA_HEAD_V = 256
GLA_DK = GLA_HEADS * GLA_HEAD_K
GLA_GATE_RANK = 16
GLA_GATE_NORM = 16.0
D_FF = 2816
EPS = 1e-6

LANES = 128
SUBLANES = 8
MXU_DIM = 256
VMEM_LIMIT_BYTES = 60 * 1024 * 1024

CONV_COLS = D_MODEL + D_MODEL + 2 * SSD_GROUPS * SSD_STATE
Q_OFF = CONV_COLS
K_OFF = Q_OFF + GLA_DK
V_OFF = K_OFF + GLA_DK
SMALL_OFF = V_OFF + D_MODEL
P1_COLS = SMALL_OFF + LANES
P2_COLS = 6 * D_MODEL
SMALL_DT = 0
SMALL_LR = 2 * SSD_HEADS

T_CHUNK = 256
IN_CHUNK = 512
TM_POST = 256
TM_FFN = 512
LRU_RUN = 4
GLA_SUB = 64
GLA_BLK = 16
NEG_BIG = -1e30
LOG2E = 1.4426950408889634
SSD_ROWS = SSD_GROUPS * SSD_STATE
SSD_LANES = (SSD_HEADS // SSD_GROUPS) * SSD_HEAD_DIM
GLA_ROWS = GLA_HEADS * GLA_HEAD_K


def _params(sem, **kw):
    return pltpu.CompilerParams(dimension_semantics=sem, vmem_limit_bytes=VMEM_LIMIT_BYTES, **kw)


def _softplus(x):
    return jnp.maximum(x, 0.0) + jnp.log1p(jnp.exp(-jnp.abs(x)))


def _softplus_log(x):
    return jnp.maximum(x, 0.0) + jnp.log(1.0 + jnp.exp(-jnp.abs(x)))


def _sigmoid(x):
    return 0.5 * (1.0 + jnp.tanh(0.5 * x))


def _silu(x):
    return x * _sigmoid(x)


def _gelu_tanh(x):
    c = 0.7978845608028654
    return 0.5 * x * (1.0 + jnp.tanh(c * (x + 0.044715 * (x * x * x))))


def _rms(x, g):
    return x * lax.rsqrt(jnp.mean(x * x, axis=-1, keepdims=True) + EPS) * g


def _dot(a, b):
    return jnp.dot(a, b, preferred_element_type=F32)


def _dot_nt(a, b):
    return lax.dot_general(a, b, (((1,), (1,)), ((), ())), preferred_element_type=F32)


def _dot_tn(a, b):
    return lax.dot_general(a, b, (((0,), (0,)), ((), ())), preferred_element_type=F32)


def _mod_kernel(c_ref, w_ref, b_ref, o_ref):
    c = _silu(c_ref[...]).astype(BF16)
    o_ref[...] = _dot(c, w_ref[...].astype(BF16)) + b_ref[...]


def _modulation(cond, w_ada, b_ada):
    n_layers = w_ada.shape[0]
    rows = cond.shape[0]
    tn = D_MODEL
    return pl.pallas_call(
        _mod_kernel,
        grid=(n_layers, 6 * D_MODEL // tn),
        in_specs=[
            pl.BlockSpec((rows, D_MODEL), lambda l, j: (0, 0)),
            pl.BlockSpec((None, D_MODEL, tn), lambda l, j: (l, 0, j)),
            pl.BlockSpec((None, 1, tn), lambda l, j: (l, 0, j)),
        ],
        out_specs=pl.BlockSpec((None, rows, tn), lambda l, j: (l, 0, j)),
        out_shape=jax.ShapeDtypeStruct((n_layers, rows, 6 * D_MODEL), F32),
        compiler_params=_params(("parallel", "parallel")),
        name="modulation",
    )(cond, w_ada, b_ada.reshape(n_layers, 1, 6 * D_MODEL))


def _inproj_kernel(x_ref, xp_ref, xn_ref, mod_ref, g_ref, w1_ref, w2_ref, cw_ref, cb_ref,
                   p1_ref, p2_ref, hext, cbuf, obuf, *, seq_pos):
    tc = T_CHUNK
    sh = mod_ref[:, 0:D_MODEL]
    sc1 = 1.0 + mod_ref[:, D_MODEL:2 * D_MODEL]
    g = g_ref[...]
    h_main = _rms(x_ref[...], g) * sc1 + sh
    hext[SUBLANES:SUBLANES + tc, :] = h_main
    pos, seq_len = seq_pos(pl.program_id(0) * tc)
    head = jnp.where(pos == 0, 0.0, 1.0)
    tail = jnp.where(pos + tc == seq_len, 0.0, 1.0)
    hext[0:SUBLANES, :] = (_rms(xp_ref[...], g) * sc1 + sh) * head
    hext[SUBLANES + tc:, :] = (_rms(xn_ref[...], g) * sc1 + sh) * tail
    he = hext[...].astype(BF16)
    hb = h_main.astype(BF16)

    def conv_columns():
        half = tc // 2
        for c0 in range(0, CONV_COLS, IN_CHUNK):
            res = _dot(he, w1_ref[:, c0:c0 + IN_CHUNK])
            for k in range(IN_CHUNK // LANES):
                cbuf[c0 // LANES + k] = res[:, k * LANES:(k + 1) * LANES]
            yield
            for k in range(IN_CHUNK // LANES):
                slab = c0 // LANES + k
                ls = slice(slab * LANES, (slab + 1) * LANES)
                for parity in range(2):
                    xc = cb_ref[:, ls]
                    for j in range(CONV_W):
                        row0 = SUBLANES - CONV_LEFT + j + parity
                        xc = xc + cw_ref[j:j + 1, ls] * cbuf[slab, pl.ds(row0, half, stride=2), :]
                    obuf[slab, pl.ds(parity, half, stride=2), :] = xc if c0 < D_MODEL else _silu(xc)
                if k % 2 == 1:
                    yield
            for k in range(IN_CHUNK // LANES):
                slab = c0 // LANES + k
                p1_ref[:, slab * LANES:(slab + 1) * LANES] = obuf[slab]
            yield

    def plain_columns():
        for c0 in range(CONV_COLS, P1_COLS, IN_CHUNK):
            cs = slice(c0, min(c0 + IN_CHUNK, P1_COLS))
            p1_ref[:, cs] = _dot(hb, w1_ref[:, cs])
            yield
        for c0 in range(0, P2_COLS, IN_CHUNK):
            cs = slice(c0, c0 + IN_CHUNK)
            p2_ref[:, cs] = _dot(hb, w2_ref[:, cs]).astype(p2_ref.dtype)
            yield

    _round_robin([(conv_columns(), 1), (plain_columns(), 1)])


def _inproj(x, mod, row_of_block, seq_pos, lw, l):
    m = x.shape[0]
    tc = T_CHUNK
    rows8 = tc // SUBLANES
    last8 = m // SUBLANES - 1
    single = dict(pipeline_mode=pl.Buffered(1))
    return pl.pallas_call(
        functools.partial(_inproj_kernel, seq_pos=seq_pos),
        grid=(m // tc,),
        in_specs=[
            pl.BlockSpec((tc, D_MODEL), lambda i: (i, 0)),
            pl.BlockSpec((SUBLANES, D_MODEL), lambda i: (jnp.maximum(i * rows8 - 1, 0), 0)),
            pl.BlockSpec((SUBLANES, D_MODEL), lambda i: (jnp.minimum((i + 1) * rows8, last8), 0)),
            pl.BlockSpec((None, 1, 6 * D_MODEL), lambda i: (row_of_block(i, tc), 0, 0)),
            pl.BlockSpec((1, D_MODEL), lambda i: (0, 0)),
            pl.BlockSpec((None, D_MODEL, P1_COLS), lambda i: (l, 0, 0), **single),
            pl.BlockSpec((None, D_MODEL, P2_COLS), lambda i: (l, 0, 0), **single),
            pl.BlockSpec((CONV_W, CONV_COLS), lambda i: (0, 0)),
            pl.BlockSpec((1, CONV_COLS), lambda i: (0, 0)),
        ],
        out_specs=[pl.BlockSpec((tc, P1_COLS), lambda i: (i, 0)),
                   pl.BlockSpec((tc, P2_COLS), lambda i: (i, 0))],
        out_shape=[jax.ShapeDtypeStruct((m, P1_COLS), F32), jax.ShapeDtypeStruct((m, P2_COLS), BF16)],
        scratch_shapes=[pltpu.VMEM((tc + 2 * SUBLANES, D_MODEL), F32),
                        pltpu.VMEM((CONV_COLS // LANES, tc + 2 * SUBLANES, LANES), F32),
                        pltpu.VMEM((CONV_COLS // LANES, tc, LANES), F32)],
        compiler_params=_params(("parallel",)),
        name="inproj",
    )(x, x, x, mod, lw["norm_mix_g"], lw["w1"], lw["w2"], lw["conv_w"], lw["conv_b"])


def _tile_scan(a, b, rev):
    t, w = a.shape
    a = a.reshape(t // SUBLANES, SUBLANES, w)
    b = b.reshape(t // SUBLANES, SUBLANES, w)
    pos = lax.broadcasted_iota(jnp.int32, (1, SUBLANES, w), 1)
    s = 1
    while s < SUBLANES:
        shift = (SUBLANES - s) if rev else s
        valid = (pos < SUBLANES - s) if rev else (pos >= s)
        a_sh = jnp.where(valid, pltpu.roll(a, shift, 1), 1.0)
        b_sh = jnp.where(valid, pltpu.roll(b, shift, 1), 0.0)
        yield
        b = a * b_sh + b
        a = a * a_sh
        yield
        s *= 2
    return a, b


def _round_robin(tasks):
    tasks = list(tasks)
    while tasks:
        for task in list(tasks):
            gen, stages = task
            try:
                for _ in range(stages):
                    next(gen)
            except StopIteration:
                tasks.remove(task)


def _seg_cumsum(x, seg, rev):
    t, w = x.shape
    ntile = t // SUBLANES
    per_seg = seg // SUBLANES
    x = x.reshape(ntile, SUBLANES, w)
    pos = lax.broadcasted_iota(jnp.int32, (1, SUBLANES, w), 1)
    s = 1
    while s < SUBLANES:
        shift = (SUBLANES - s) if rev else s
        valid = (pos < SUBLANES - s) if rev else (pos >= s)
        x = x + jnp.where(valid, pltpu.roll(x, shift, 1), 0.0)
        s *= 2
    tiles = [None] * ntile
    for s0 in range(0, ntile, per_seg):
        carry = None
        for j in (range(s0 + per_seg - 1, s0 - 1, -1) if rev else range(s0, s0 + per_seg)):
            tiles[j] = x[j] if carry is None else x[j] + carry
            carry = tiles[j][0:1, :] if rev else tiles[j][SUBLANES - 1:SUBLANES, :]
    return jnp.stack(tiles, axis=0).reshape(t, w)


def _ssd_intra(gmat, col, rowv, xhat_bf, tri, rev):
    half = tri.shape[0]
    lo, hi = slice(0, half), slice(half, 2 * half)

    def blk(rs, cs, masked):
        e = col[rs, :] - rowv[:, cs]
        if masked:
            e = jnp.where(tri, e, NEG_BIG)
        return (gmat[rs, cs] * jnp.exp2(e)).astype(BF16)

    if rev:
        out_top = _dot(jnp.concatenate([blk(lo, lo, True), blk(lo, hi, False)], axis=1), xhat_bf)
        yield
        out_bot = _dot(blk(hi, hi, True), xhat_bf[hi, :])
    else:
        out_top = _dot(blk(lo, lo, True), xhat_bf[lo, :])
        yield
        out_bot = _dot(jnp.concatenate([blk(hi, lo, False), blk(hi, hi, True)], axis=1), xhat_bf)
    yield
    return jnp.concatenate([out_top, out_bot], axis=0)


def _ssd_state_slices(h):
    g, hl = divmod(h, SSD_HEADS // SSD_GROUPS)
    return (slice(g * SSD_STATE, (g + 1) * SSD_STATE),
            slice(hl * SSD_HEAD_DIM, (hl + 1) * SSD_HEAD_DIM))


def _scan_kernel(*refs, rev, zero_init, emit_state, alias_out, alias_state, nchunks):
    it = iter(refs)
    p_ref = next(it)
    lruw_ref, lruba_ref, lrubx_ref, lam_ref = next(it), next(it), next(it), next(it)
    dtb_ref, alog_ref, dskip_ref = next(it), next(it), next(it)
    w2_ref, gb_ref = next(it), next(it)
    if not zero_init:
        i_lru, i_ssd, i_gla = next(it), next(it), next(it)
    if alias_out:
        next(it)
    if alias_state:
        next(it), next(it), next(it)
    o_ref = next(it)
    if emit_state:
        s_lru, s_ssd, s_gla = next(it), next(it), next(it)
    h_scr, ssd_scr, gla_scr = next(it), next(it), next(it)
    la_scr, lb_scr, lh_scr = next(it), next(it), next(it)

    tc = T_CHUNK
    c = pl.program_id(1)

    @pl.when(c == 0)
    def _():
        if zero_init:
            h_scr[...] = jnp.zeros_like(h_scr)
            ssd_scr[...] = jnp.zeros_like(ssd_scr)
            gla_scr[...] = jnp.zeros_like(gla_scr)
        else:
            h_scr[...] = jnp.broadcast_to(i_lru[...], h_scr.shape)
            for h in range(SSD_HEADS):
                rs, ls = _ssd_state_slices(h)
                ssd_scr[rs, ls] = i_ssd[h]
            gla_scr[...] = i_gla[...]

    small = p_ref[:, SMALL_OFF:SMALL_OFF + LANES]

    c8h = (-0.5 * LRU_C) * _softplus(-lam_ref[...])
    ntile = tc // SUBLANES

    def lru_group(g):
        sl = slice(g * MXU_DIM, (g + 1) * MXU_DIM)
        xl = p_ref[:, sl]
        pre = _dot(xl.astype(BF16), lruw_ref[g])
        yield
        t_r = jnp.tanh(pre[:, :MXU_DIM] + lruba_ref[:, sl])
        yield
        t_i = jnp.tanh(pre[:, MXU_DIM:] + lrubx_ref[:, sl])
        yield
        log_a = c8h[:, sl] * t_r + c8h[:, sl]
        xlh = 0.5 * xl
        ix = xlh * t_i + xlh
        yield
        a = jnp.exp(log_a)
        th = jnp.tanh(log_a)
        yield
        b = jnp.sqrt(-th * (1.0 + a * a)) * ix
        slabs = range(g * MXU_DIM // LANES, (g + 1) * MXU_DIM // LANES)
        for k, slab in enumerate(slabs):
            la_scr[slab] = a[:, k * LANES:(k + 1) * LANES]
            lb_scr[slab] = b[:, k * LANES:(k + 1) * LANES]
        yield
        nrun = tc // LRU_RUN
        order = range(LRU_RUN - 1, -1, -1) if rev else range(LRU_RUN)
        run_a, run_b = {}, {}
        for k, slab in enumerate(slabs):
            pa = pb = None
            for r in order:
                ar = la_scr[slab, pl.ds(r, nrun, stride=LRU_RUN), :]
                br = lb_scr[slab, pl.ds(r, nrun, stride=LRU_RUN), :]
                if pa is not None:
                    br = ar * pb + br
                    ar = ar * pa
                run_a[k, r], run_b[k, r] = pa, pb = ar, br
        yield
        tot_a = jnp.concatenate([run_a[k, order[-1]] for k in range(len(slabs))], axis=1)
        tot_b = jnp.concatenate([run_b[k, order[-1]] for k in range(len(slabs))], axis=1)
        a_cum, b_cum = yield from _tile_scan(tot_a, tot_b, rev)
        carry_in = h_scr[0:1, sl]
        carry = carry_in
        nt = nrun // SUBLANES
        hs = [None] * nt
        for ti in (range(nt - 1, -1, -1) if rev else range(nt)):
            hs[ti] = a_cum[ti] * carry + b_cum[ti]
            carry = hs[ti][0:1, :] if rev else hs[ti][SUBLANES - 1:SUBLANES, :]
        h_scr[:, sl] = jnp.broadcast_to(carry, (SUBLANES, MXU_DIM))
        yield
        h_out = jnp.stack(hs, axis=0).reshape(nrun, MXU_DIM)
        row = lax.broadcasted_iota(jnp.int32, (nrun, MXU_DIM), 0)
        if rev:
            h_in = jnp.where(row == nrun - 1, carry_in, pltpu.roll(h_out, nrun - 1, 0))
        else:
            h_in = jnp.where(row == 0, carry_in, pltpu.roll(h_out, 1, 0))
        for k, slab in enumerate(slabs):
            hk = h_in[:, k * LANES:(k + 1) * LANES]
            for r in order:
                lh_scr[slab, pl.ds(r, nrun, stride=LRU_RUN), :] = run_a[k, r] * hk + run_b[k, r]
        yield
        for k, slab in enumerate(slabs):
            o_ref[:, g * MXU_DIM + k * LANES:g * MXU_DIM + (k + 1) * LANES] = lh_scr[slab].astype(o_ref.dtype)

    lane = lax.broadcasted_iota(jnp.int32, (1, LANES), 1)
    dt_lo = SMALL_DT + (SSD_HEADS if rev else 0)
    dt_mask = (lane >= dt_lo) & (lane < dt_lo + SSD_HEADS)
    dt = jnp.where(dt_mask, _softplus_log(small + dtb_ref[...]), 0.0)
    la = dt * (-LOG2E * jnp.exp(alog_ref[...]))
    bcum = _seg_cumsum(la, tc, rev)
    bcum_t = bcum.T
    tot = bcum[0:1, :] if rev else bcum[tc - 1:tc, :]

    half = tc // 2
    ri = lax.broadcasted_iota(jnp.int32, (half, half), 0)
    ci = lax.broadcasted_iota(jnp.int32, (half, half), 1)
    tri = (ci >= ri) if rev else (ci <= ri)
    lane_t = lax.broadcasted_iota(jnp.int32, (tc, LANES), 1)
    lo_half = lane_t < SSD_HEAD_DIM
    row_s = lax.broadcasted_iota(jnp.int32, (2 * SSD_STATE, SSD_LANES), 0)
    lane_s = lax.broadcasted_iota(jnp.int32, (1, SSD_LANES), 1) // SSD_HEAD_DIM
    b_off = 2 * D_MODEL
    c_off = b_off + SSD_GROUPS * SSD_STATE
    ssd_upd = {}

    def ssd_group(g):
        pair, gl = divmod(g, 2)
        b_tile = p_ref[:, b_off + pair * LANES:b_off + (pair + 1) * LANES].astype(BF16)
        c_tile = p_ref[:, c_off + pair * LANES:c_off + (pair + 1) * LANES]
        s_pair = ssd_scr[pair * LANES:(pair + 1) * LANES, :]
        gmask = lo_half if gl == 0 else jnp.logical_not(lo_half)
        c_g = jnp.where(gmask, c_tile, 0.0).astype(BF16)
        gmat = _dot_nt(c_g, b_tile)
        inter = _dot(c_g, s_pair.astype(BF16))
        yield
        wx_parts = []
        for hp in range(2):
            h0 = g * 4 + hp * 2
            col0 = bcum[:, dt_lo + h0:dt_lo + h0 + 1]
            col1 = bcum[:, dt_lo + h0 + 1:dt_lo + h0 + 2]
            dtc0 = dt[:, dt_lo + h0:dt_lo + h0 + 1]
            dtc1 = dt[:, dt_lo + h0 + 1:dt_lo + h0 + 2]
            xsl = slice(D_MODEL + h0 * SSD_HEAD_DIM, D_MODEL + (h0 + 2) * SSD_HEAD_DIM)
            xs = p_ref[:, xsl]
            xhat = xs * jnp.where(lo_half, dtc0, dtc1)
            xhat_bf = xhat.astype(BF16)
            yield
            outs = []
            for hh, col in ((0, col0), (1, col1)):
                rowv = bcum_t[dt_lo + h0 + hh:dt_lo + h0 + hh + 1, :]
                outs.append((yield from _ssd_intra(gmat, col, rowv, xhat_bf, tri, rev)))
            o_pair = jnp.where(lo_half, outs[0], outs[1])
            colp = jnp.where(lo_half, col0, col1)
            isl = slice(hp * LANES, (hp + 1) * LANES)
            o_pair = o_pair + jnp.exp2(colp) * inter[:, isl]
            if not rev:
                o_pair = o_pair + dskip_ref[:, h0 * SSD_HEAD_DIM:(h0 + 2) * SSD_HEAD_DIM] * xs
            o_ref[:, xsl] = o_pair.astype(o_ref.dtype)
            yield
            t0 = tot[:, dt_lo + h0:dt_lo + h0 + 1]
            t1 = tot[:, dt_lo + h0 + 1:dt_lo + h0 + 2]
            totp = jnp.where(lane < SSD_HEAD_DIM, t0, t1)
            wx_parts.append((jnp.exp2(totp - colp) * xhat).astype(BF16))
        wx = jnp.concatenate(wx_parts, axis=1)
        ssd_upd[g] = _dot_tn(b_tile, wx)
        yield
        d = jnp.zeros((1, SSD_LANES), F32)
        for hl in range(4):
            th = tot[:, dt_lo + g * 4 + hl:dt_lo + g * 4 + hl + 1]
            d = jnp.where(lane_s == hl, jnp.exp2(th), d)
        ssd_upd[("dec", g)] = d
        if gl == 1:
            first = row_s < SSD_STATE
            new = jnp.where(first, ssd_upd[g - 1], ssd_upd[g])
            dec = jnp.where(first, ssd_upd[("dec", g - 1)], d)
            ssd_scr[pair * LANES:(pair + 1) * LANES, :] = dec * s_pair + new

    small_bf = small.astype(BF16)
    gla_bg = {}

    def gla_log_decay(hd):
        ksl = slice(hd * GLA_HEAD_K, (hd + 1) * GLA_HEAD_K)
        z = _dot(small_bf, w2_ref[:, ksl]) + gb_ref[:, ksl]
        yield
        ld = -_softplus_log(-z) * (LOG2E / GLA_GATE_NORM)
        yield
        gla_bg[hd] = _seg_cumsum(ld, GLA_SUB, rev)
        yield

    nsub = tc // GLA_SUB
    nblk = GLA_SUB // GLA_BLK
    scale = GLA_HEAD_K ** -0.5
    gla_st = [gla_scr[hd * GLA_HEAD_K:(hd + 1) * GLA_HEAD_K, :] for hd in range(GLA_HEADS)]

    def gla_subchunk(sc_i, hd):
        ksl = slice(hd * GLA_HEAD_K, (hd + 1) * GLA_HEAD_K)
        st = gla_st[hd]
        rsl = slice(sc_i * GLA_SUB, (sc_i + 1) * GLA_SUB)
        q = p_ref[rsl, Q_OFF + hd * GLA_HEAD_K:Q_OFF + (hd + 1) * GLA_HEAD_K] * scale
        k = p_ref[rsl, K_OFF + hd * GLA_HEAD_K:K_OFF + (hd + 1) * GLA_HEAD_K]
        v = p_ref[rsl, V_OFF + hd * GLA_HEAD_V:V_OFF + (hd + 1) * GLA_HEAD_V].astype(BF16)
        b = gla_bg[hd][rsl, :]
        inter = _dot((q * jnp.exp2(b)).astype(BF16), st.astype(BF16))
        yield
        o_rows = []
        for blk in range(nblk):
            bsl = slice(blk * GLA_BLK, (blk + 1) * GLA_BLK)
            if rev:
                keys = slice(blk * GLA_BLK, GLA_SUB)
                ref_row = b[(blk + 1) * GLA_BLK - 1:(blk + 1) * GLA_BLK, :]
            else:
                keys = slice(0, (blk + 1) * GLA_BLK)
                ref_row = b[blk * GLA_BLK:blk * GLA_BLK + 1, :]
            nk = keys.stop - keys.start
            qt = (q[bsl, :] * jnp.exp2(b[bsl, :] - ref_row)).astype(BF16)
            kt = (k[keys, :] * jnp.exp2(ref_row - b[keys, :])).astype(BF16)
            a_blk = _dot_nt(qt, kt)
            yield
            ai = lax.broadcasted_iota(jnp.int32, (GLA_BLK, nk), 0)
            aj = lax.broadcasted_iota(jnp.int32, (GLA_BLK, nk), 1)
            vis = (aj >= ai) if rev else (aj <= ai + blk * GLA_BLK)
            a_blk = jnp.where(vis, a_blk, 0.0).astype(BF16)
            o_rows.append(inter[bsl, :] + _dot(a_blk, v[keys, :]))
            yield
        o = jnp.concatenate(o_rows, axis=0)
        o_ref[rsl, 2 * D_MODEL + hd * GLA_HEAD_V:2 * D_MODEL + (hd + 1) * GLA_HEAD_V] = o.astype(o_ref.dtype)
        end = b[0:1, :] if rev else b[GLA_SUB - 1:GLA_SUB, :]
        k_dec = (k * jnp.exp2(end - b)).astype(BF16)
        dec_col = jnp.broadcast_to(jnp.exp2(end), (GLA_HEAD_K, GLA_HEAD_K)).T[:, 0:1]
        gla_st[hd] = dec_col * st + _dot_tn(k_dec, v)
        yield

    def gla_head(hd):
        yield from gla_log_decay(hd)
        for step in range(nsub):
            yield from gla_subchunk((nsub - 1 - step) if rev else step, hd)

    def lru_all():
        for g in range(D_MODEL // MXU_DIM):
            yield from lru_group(g)

    def ssd_all():
        for g in range(SSD_GROUPS):
            yield from ssd_group(g)

    _round_robin([(gla_head(hd), 1) for hd in range(GLA_HEADS)] + [(ssd_all(), 2), (lru_all(), 1)])
    for hd in range(GLA_HEADS):
        gla_scr[hd * GLA_HEAD_K:(hd + 1) * GLA_HEAD_K, :] = gla_st[hd]

    if emit_state:
        @pl.when(c == nchunks - 1)
        def _():
            s_lru[...] = h_scr[0:1, :]
            for h in range(SSD_HEADS):
                rs, ls = _ssd_state_slices(h)
                s_ssd[h] = ssd_scr[rs, ls]
            s_gla[...] = gla_scr[...]


def _scan(p1, tok0, nseq, nchunks, lw, l, d, init, o_prev, st_prev, st_shapes):
    rev = d == 1
    tc = T_CHUNK
    blk0 = tok0 // tc
    m = p1.shape[0]

    def chunk_of(c):
        return (nchunks - 1 - c) if rev else c

    const2 = lambda b, c: (0, 0)
    const3 = lambda b, c: (0, 0, 0)
    in_specs = [
        pl.BlockSpec((tc, P1_COLS), lambda b, c: (blk0 + b * nchunks + chunk_of(c), 0)),
        pl.BlockSpec((D_MODEL // MXU_DIM, MXU_DIM, 2 * MXU_DIM), const3),
        pl.BlockSpec((1, D_MODEL), const2),
        pl.BlockSpec((1, D_MODEL), const2),
        pl.BlockSpec((1, D_MODEL), const2),
        pl.BlockSpec((1, LANES), const2),
        pl.BlockSpec((1, LANES), const2),
        pl.BlockSpec((1, D_MODEL), const2),
        pl.BlockSpec((LANES, GLA_DK), const2),
        pl.BlockSpec((1, GLA_DK), const2),
    ]
    args = [p1, lw["lru_w"][d], lw["lru_ba"][d], lw["lru_bx"][d], lw["lam"][d], lw["dt_bias"][d],
            lw["a_log"][d], lw["d_skip"], lw["gate_w2"][d], lw["gate_b"][d]]
    state_specs = [
        pl.BlockSpec((None, None, None, 1, D_MODEL), lambda b, c: (b, l, d, 0, 0)),
        pl.BlockSpec((None, None, None, SSD_HEADS, SSD_STATE, SSD_HEAD_DIM), lambda b, c: (b, l, d, 0, 0, 0)),
        pl.BlockSpec((None, None, None, GLA_ROWS, GLA_HEAD_V), lambda b, c: (b, l, d, 0, 0)),
    ]
    if init is not None:
        in_specs += state_specs
        args += list(init)
    aliases = {}
    if o_prev is not None:
        aliases[len(args)] = 0
        in_specs.append(pl.BlockSpec(memory_space=pl.ANY))
        args.append(o_prev)
    if st_prev is not None:
        for k, arr in enumerate(st_prev):
            aliases[len(args)] = 1 + k
            in_specs.append(pl.BlockSpec(memory_space=pl.ANY))
            args.append(arr)
    out_specs = [pl.BlockSpec((tc, 3 * D_MODEL), lambda b, c: (blk0 + b * nchunks + chunk_of(c), 0))]
    out_shape = [jax.ShapeDtypeStruct((m, 3 * D_MODEL), BF16)]
    if st_shapes is not None:
        out_specs += state_specs
        out_shape += [jax.ShapeDtypeStruct(s, F32) for s in st_shapes]
    kern = functools.partial(_scan_kernel, rev=rev, zero_init=init is None, emit_state=st_shapes is not None,
                             alias_out=o_prev is not None, alias_state=st_prev is not None, nchunks=nchunks)
    return pl.pallas_call(
        kern,
        grid=(nseq, nchunks),
        in_specs=in_specs,
        out_specs=out_specs,
        out_shape=out_shape,
        input_output_aliases=aliases,
        scratch_shapes=[
            pltpu.VMEM((SUBLANES, D_MODEL), F32),
            pltpu.VMEM((SSD_ROWS, SSD_LANES), F32),
            pltpu.VMEM((GLA_ROWS, GLA_HEAD_V), F32),
            pltpu.VMEM((D_MODEL // LANES, tc, LANES), F32),
            pltpu.VMEM((D_MODEL // LANES, tc, LANES), F32),
            pltpu.VMEM((D_MODEL // LANES, tc, LANES), F32),
        ],
        compiler_params=_params(("parallel", "arbitrary")),
        name="scan_rev" if rev else "scan_fwd",
    )(*args)


def _post_kernel(x_ref, of_ref, ob_ref, p2_ref, mod_ref, ssdg_ref, glag_ref,
                 wl_ref, ws_ref, wg_ref, wo_ref, o_ref):
    dm = D_MODEL

    def both(sl):
        return of_ref[:, sl].astype(F32) + ob_ref[:, sl].astype(F32)

    def p2(k):
        return p2_ref[:, k * dm:(k + 1) * dm].astype(F32)

    y_lru = both(slice(0, dm)) * _gelu_tanh(p2(0))
    y_ssd = _rms(both(slice(dm, 2 * dm)) * _silu(p2(1)), ssdg_ref[...])
    parts = []
    for hd in range(GLA_HEADS):
        sl = slice(2 * dm + hd * GLA_HEAD_V, 2 * dm + (hd + 1) * GLA_HEAD_V)
        parts.append(_rms(both(sl), glag_ref[...]))
    y_gla = jnp.concatenate(parts, axis=1) * _silu(p2(2))
    merged = (_sigmoid(p2(3)) * _dot(y_lru.astype(BF16), wl_ref[...])
              + _sigmoid(p2(4)) * _dot(y_ssd.astype(BF16), ws_ref[...])
              + _sigmoid(p2(5)) * _dot(y_gla.astype(BF16), wg_ref[...]))
    out = _dot(merged.astype(BF16), wo_ref[...])
    o_ref[...] = x_ref[...] + mod_ref[:, 2 * dm:3 * dm] * out


def _post(x, o_f, o_b, p2, mod, row_of_block, lw):
    m = x.shape[0]
    tm = TM_POST
    wspec = pl.BlockSpec((D_MODEL, D_MODEL), lambda i: (0, 0))
    return pl.pallas_call(
        _post_kernel,
        grid=(m // tm,),
        in_specs=[
            pl.BlockSpec((tm, D_MODEL), lambda i: (i, 0)),
            pl.BlockSpec((tm, 3 * D_MODEL), lambda i: (i, 0)),
            pl.BlockSpec((tm, 3 * D_MODEL), lambda i: (i, 0)),
            pl.BlockSpec((tm, P2_COLS), lambda i: (i, 0)),
            pl.BlockSpec((None, 1, 6 * D_MODEL), lambda i: (row_of_block(i, tm), 0, 0)),
            pl.BlockSpec((1, D_MODEL), lambda i: (0, 0)),
            pl.BlockSpec((1, GLA_HEAD_V), lambda i: (0, 0)),
            wspec, wspec, wspec, wspec,
        ],
        out_specs=pl.BlockSpec((tm, D_MODEL), lambda i: (i, 0)),
        out_shape=jax.ShapeDtypeStruct((m, D_MODEL), F32),
        compiler_params=_params(("parallel",)),
        name="post",
    )(x, o_f, o_b, p2, mod, lw["ssd_norm_g"], lw["gla_norm_g"],
      lw["w_br_lru"], lw["w_br_ssd"], lw["w_br_gla"], lw["w_out"])


def _ffn_kernel(x_ref, mod_ref, g_ref, wi_ref, wo_ref, fg_ref, o_ref, *, final_norm, nsplit):
    dm = D_MODEL
    x = x_ref[...]
    h = (_rms(x, g_ref[...]) * (1.0 + mod_ref[:, 4 * dm:5 * dm]) + mod_ref[:, 3 * dm:4 * dm]).astype(BF16)
    tf = D_FF // nsplit
    acc = jnp.zeros(x.shape, F32)
    for s in range(nsplit):
        fs = slice(s * tf, (s + 1) * tf)
        gate = _dot(h, wi_ref[:, fs])
        up = _dot(h, wi_ref[:, D_FF + s * tf:D_FF + (s + 1) * tf])
        acc = acc + _dot((_silu(gate) * up).astype(BF16), wo_ref[fs, :])
    y = x + mod_ref[:, 5 * dm:6 * dm] * acc
    if final_norm:
        y = _rms(y, fg_ref[...])
    o_ref[...] = y


def _ffn(x, mod, row_of_block, lw, final_g, final_norm):
    m = x.shape[0]
    tm = TM_FFN
    single = dict(pipeline_mode=pl.Buffered(1))
    kern = functools.partial(_ffn_kernel, final_norm=final_norm, nsplit=2)
    return pl.pallas_call(
        kern,
        grid=(m // tm,),
        in_specs=[
            pl.BlockSpec((tm, D_MODEL), lambda i: (i, 0)),
            pl.BlockSpec((None, 1, 6 * D_MODEL), lambda i: (row_of_block(i, tm), 0, 0)),
            pl.BlockSpec((1, D_MODEL), lambda i: (0, 0)),
            pl.BlockSpec((D_MODEL, 2 * D_FF), lambda i: (0, 0), **single),
            pl.BlockSpec((D_FF, D_MODEL), lambda i: (0, 0), **single),
            pl.BlockSpec((1, D_MODEL), lambda i: (0, 0)),
        ],
        out_specs=pl.BlockSpec((tm, D_MODEL), lambda i: (i, 0)),
        out_shape=jax.ShapeDtypeStruct((m, D_MODEL), F32),
        compiler_params=_params(("parallel",)),
        name="ffn",
    )(x, mod, lw["norm_ffn_g"], lw["w_ffn_in"], lw["w_ffn_out"], final_g)


def _snake(x):
    bsz, t, d = x.shape
    grid = x.reshape(bsz, t // (2 * GRID_W), 2, GRID_W, d)
    return jnp.concatenate([grid[:, :, :1], jnp.flip(grid[:, :, 1:], 3)], axis=2).reshape(bsz, t, d)


_O_GATE, _O_Z, _O_XBC = D_MODEL, 2 * D_MODEL, 3 * D_MODEL
_O_DT = _O_XBC + CONV_COLS - D_MODEL
_O_Q = _O_DT + 2 * SSD_HEADS
_O_G = _O_Q + 2 * GLA_DK + D_MODEL
_O_LR = _O_G + D_MODEL
_O_M = _O_LR + 2 * GLA_GATE_RANK
IN_WIDTH = _O_M + 3 * D_MODEL
W_PREP_ROWS = 128


def _wprep_kernel(w_ref, w1_ref, w2_ref):
    def cols(lo, hi):
        return w_ref[:, lo:hi].astype(BF16)

    w1_ref[:, 0:D_MODEL] = cols(0, D_MODEL)
    w1_ref[:, D_MODEL:CONV_COLS] = cols(_O_XBC, _O_DT)
    w1_ref[:, Q_OFF:SMALL_OFF] = cols(_O_Q, _O_G)
    assert _O_DT % LANES == SMALL_DT and _O_LR % LANES == SMALL_LR
    lane = lax.broadcasted_iota(jnp.int32, (w_ref.shape[0], LANES), 1)
    dt_tile = w_ref[:, _O_DT:_O_DT + LANES]
    lr_tile = w_ref[:, _O_LR - SMALL_LR:_O_LR - SMALL_LR + LANES]
    small = jnp.where(lane < SMALL_LR, dt_tile, jnp.where(lane < 2 * SMALL_LR, lr_tile, 0.0))
    w1_ref[:, SMALL_OFF:] = small.astype(BF16)
    w2_ref[:, 0:2 * D_MODEL] = cols(_O_GATE, _O_XBC)
    w2_ref[:, 2 * D_MODEL:3 * D_MODEL] = cols(_O_G, _O_LR)
    w2_ref[:, 3 * D_MODEL:] = cols(_O_M, IN_WIDTH)


def _prep_w_in(w_in):
    n_layers = w_in.shape[0]
    tr = W_PREP_ROWS
    return pl.pallas_call(
        _wprep_kernel,
        grid=(n_layers, D_MODEL // tr),
        in_specs=[pl.BlockSpec((None, tr, IN_WIDTH), lambda l, r: (l, r, 0))],
        out_specs=[pl.BlockSpec((None, tr, P1_COLS), lambda l, r: (l, r, 0)),
                   pl.BlockSpec((None, tr, P2_COLS), lambda l, r: (l, r, 0))],
        out_shape=[jax.ShapeDtypeStruct((n_layers, D_MODEL, P1_COLS), BF16),
                   jax.ShapeDtypeStruct((n_layers, D_MODEL, P2_COLS), BF16)],
        compiler_params=_params(("parallel", "parallel")),
        name="w_in_relayout",
    )(w_in)


def _layer_weights(l, w):
    dm = D_MODEL

    def block_diag_tiles(wa, wx):
        per = MXU_DIM // LRU_BLOCK_W
        eye = jnp.eye(per, dtype=wa.dtype)
        def tiles(wb):
            wb = wb.reshape(dm // MXU_DIM, per, LRU_BLOCK_W, LRU_BLOCK_W)
            t = jnp.einsum("gpwv,pq->gpwqv", wb, eye)
            return t.reshape(dm // MXU_DIM, MXU_DIM, MXU_DIM)
        return jnp.concatenate([tiles(wa), tiles(wx)], axis=2).astype(BF16)

    def lane_pad(v, lo):
        return jnp.zeros((1, LANES), F32).at[0, lo:lo + v.shape[0]].set(v)

    def gate_w2(d):
        lo = SMALL_LR + d * GLA_GATE_RANK
        return jnp.zeros((LANES, GLA_DK), F32).at[lo:lo + GLA_GATE_RANK].set(w["gla_gate_w2"][l, d]).astype(BF16)

    return dict(
        w1=w["w1"], w2=w["w2"],
        norm_mix_g=w["norm_mix_g"][l][None], norm_ffn_g=w["norm_ffn_g"][l][None],
        conv_w=jnp.concatenate([w["lru_conv_w"][l], w["ssd_conv_w"][l]], axis=1),
        conv_b=jnp.concatenate([w["lru_conv_b"][l], w["ssd_conv_b"][l]])[None],
        lru_w=[block_diag_tiles(0.5 * w["lru_w_a"][l, d], 0.5 * w["lru_w_x"][l, d]) for d in range(2)],
        lru_ba=[0.5 * w["lru_b_a"][l, d][None] for d in range(2)],
        lru_bx=[0.5 * w["lru_b_x"][l, d][None] for d in range(2)],
        lam=[w["lru_lambda"][l, d][None] for d in range(2)],
        dt_bias=[lane_pad(w["ssd_dt_bias"][l, d], SMALL_DT + d * SSD_HEADS) for d in range(2)],
        a_log=[lane_pad(w["ssd_a_log"][l, d], SMALL_DT + d * SSD_HEADS) for d in range(2)],
        d_skip=jnp.repeat(w["ssd_d"][l], SSD_HEAD_DIM)[None],
        gate_w2=[gate_w2(d) for d in range(2)],
        gate_b=[w["gla_gate_b"][l, d][None] for d in range(2)],
        ssd_norm_g=w["ssd_norm_g"][l][None], gla_norm_g=w["gla_norm_g"][l][None],
        w_br_lru=w["w_br_lru"][l].astype(BF16), w_br_ssd=w["w_br_ssd"][l].astype(BF16),
        w_br_gla=w["w_br_gla"][l].astype(BF16), w_out=w["w_out"][l].astype(BF16),
        w_ffn_in=w["w_ffn_in"][l].astype(BF16),
        w_ffn_out=w["w_ffn_out"][l].astype(BF16),
    )


def kernel(x_prompt, x_sample, state_lru, state_ssd, state_gla, c, c_ctx, norm_mix_g, norm_ffn_g, w_ada, b_ada, w_in, lru_conv_w, lru_conv_b, lru_w_a, lru_b_a, lru_w_x, lru_b_x, lru_lambda, ssd_conv_w, ssd_conv_b, ssd_dt_bias, ssd_a_log, ssd_d, ssd_norm_g, gla_gate_w2, gla_gate_b, gla_norm_g, w_br_lru, w_br_ssd, w_br_gla, w_out, w_ffn_in, w_ffn_out, final_norm_g):
    w = dict(norm_mix_g=norm_mix_g, norm_ffn_g=norm_ffn_g, w_in=w_in, lru_conv_w=lru_conv_w,
             lru_conv_b=lru_conv_b, lru_w_a=lru_w_a, lru_b_a=lru_b_a, lru_w_x=lru_w_x, lru_b_x=lru_b_x,
             lru_lambda=lru_lambda, ssd_conv_w=ssd_conv_w, ssd_conv_b=ssd_conv_b, ssd_dt_bias=ssd_dt_bias,
             ssd_a_log=ssd_a_log, ssd_d=ssd_d, ssd_norm_g=ssd_norm_g, gla_gate_w2=gla_gate_w2,
             gla_gate_b=gla_gate_b, gla_norm_g=gla_norm_g, w_br_lru=w_br_lru, w_br_ssd=w_br_ssd,
             w_br_gla=w_br_gla, w_out=w_out, w_ffn_in=w_ffn_in, w_ffn_out=w_ffn_out)
    w["w1"], w["w2"] = _prep_w_in(w_in)
    n_layers = w_in.shape[0]
    b_ctx, t_ctx, dm = x_prompt.shape
    b_lat, t_lat, _ = x_sample.shape
    m_ctx = b_ctx * t_ctx
    m_lat = b_lat * t_lat
    nch_ctx = t_ctx // T_CHUNK
    nch_lat = t_lat // T_CHUNK

    n_rows = -(-(1 + b_lat) // SUBLANES) * SUBLANES
    cond = jnp.zeros((n_rows, dm), F32).at[0].set(c_ctx).at[1:1 + b_lat].set(c)
    mod = _modulation(cond, w_ada, b_ada)

    def row_of_block(i, tm):
        tok = i * tm
        return jnp.where(tok < m_ctx, 0, 1 + (tok - m_ctx) // t_lat)

    def seq_pos(tok):
        in_ctx = tok < m_ctx
        pos = jnp.where(in_ctx, lax.rem(tok, t_ctx), lax.rem(tok - m_ctx, t_lat))
        return pos, jnp.where(in_ctx, t_ctx, t_lat)

    x = jnp.concatenate([x_prompt.reshape(m_ctx, dm), _snake(x_sample).reshape(m_lat, dm)], axis=0)

    st_shapes = [(b_ctx, n_layers, 2, 1, dm),
                 (b_ctx, n_layers, 2, SSD_HEADS, SSD_STATE, SSD_HEAD_DIM),
                 (b_ctx, n_layers, 2, GLA_ROWS, GLA_HEAD_V)]
    init = (state_lru.reshape(b_lat, n_layers, 2, 1, dm), state_ssd,
            state_gla.reshape(b_lat, n_layers, 2, GLA_ROWS, GLA_HEAD_V))
    states = None
    for l in range(n_layers):
        lw = _layer_weights(l, w)
        mod_l = mod[l][:, None, :]
        p1, p2 = _inproj(x, mod_l, row_of_block, seq_pos, lw, l)
        outs = []
        for d in range(2):
            o, *states = _scan(p1, 0, b_ctx, nch_ctx, lw, l, d, None, None, states, st_shapes)
            (o,) = _scan(p1, m_ctx, b_lat, nch_lat, lw, l, d, init, o, None, None)
            outs.append(o)
        x = _post(x, outs[0], outs[1], p2, mod_l, row_of_block, lw)
        x = _ffn(x, mod_l, row_of_block, lw, final_norm_g[None], l == n_layers - 1)

    y_prompt = x[:m_ctx].reshape(b_ctx, t_ctx, dm)
    y_sample = _snake(x[m_ctx:].reshape(b_lat, t_lat, dm))
    return (y_prompt, y_sample, states[0].reshape(b_ctx, n_layers, 2, dm), states[1],
            states[2].reshape(b_ctx, n_layers, 2, GLA_HEADS, GLA_HEAD_K, GLA_HEAD_V))
```

```python
import functools

import jax
import jax.numpy as jnp
from jax import lax
from jax.experimental import pallas as pl
from jax.experimental.pallas import tpu as pltpu

F32 = jnp.float32
BF16 = jnp.bfloat16

D_MODEL = 1024
GRID_W = 64
CONV_W = 4
CONV_LEFT = 2
LRU_BLOCK_W = 64
LRU_C = 8.0
SSD_HEAD_DIM = 64
SSD_HEADS = 16
SSD_GROUPS = 4
SSD_STATE = 64
GLA_HEADS = 4
GLA_HEAD_K = 128
GLA_HEAD_V = 256
GLA_DK = GLA_HEADS * GLA_HEAD_K
GLA_GATE_RANK = 16
GLA_GATE_NORM = 16.0
D_FF = 2816
EPS = 1e-6

LANES = 128
SUBLANES = 8
MXU_DIM = 256
VMEM_LIMIT_BYTES = 60 * 1024 * 1024

CONV_COLS = D_MODEL + D_MODEL + 2 * SSD_GROUPS * SSD_STATE
Q_OFF = CONV_COLS
K_OFF = Q_OFF + GLA_DK
V_OFF = K_OFF + GLA_DK
SMALL_OFF = V_OFF + D_MODEL
P1_COLS = SMALL_OFF + LANES
P2_COLS = 6 * D_MODEL
SMALL_DT = 0
SMALL_LR = 2 * SSD_HEADS

T_CHUNK = 256
IN_CHUNK = 512
TM_POST = 512
TM_FFN = 512
LRU_RUN = 4
GLA_SUB = 64
GLA_BLK = 16
NEG_BIG = -1e30
LOG2E = 1.4426950408889634
SSD_ROWS = SSD_GROUPS * SSD_STATE
SSD_LANES = (SSD_HEADS // SSD_GROUPS) * SSD_HEAD_DIM
GLA_ROWS = GLA_HEADS * GLA_HEAD_K


def _params(sem, **kw):
    return pltpu.CompilerParams(dimension_semantics=sem, vmem_limit_bytes=VMEM_LIMIT_BYTES, **kw)


def _softplus(x):
    return jnp.maximum(x, 0.0) + jnp.log1p(jnp.exp(-jnp.abs(x)))


def _softplus_log(x):
    return jnp.maximum(x, 0.0) + jnp.log(1.0 + jnp.exp(-jnp.abs(x)))


def _sigmoid(x):
    return 0.5 * (1.0 + jnp.tanh(0.5 * x))


def _silu(x):
    return x * _sigmoid(x)


def _gelu_tanh(x):
    c = 0.7978845608028654
    return 0.5 * x * (1.0 + jnp.tanh(c * (x + 0.044715 * (x * x * x))))


def _rms(x, g):
    return x * lax.rsqrt(jnp.mean(x * x, axis=-1, keepdims=True) + EPS) * g


def _dot(a, b):
    return jnp.dot(a, b, preferred_element_type=F32)


def _dot_nt(a, b):
    return lax.dot_general(a, b, (((1,), (1,)), ((), ())), preferred_element_type=F32)


def _dot_tn(a, b):
    return lax.dot_general(a, b, (((0,), (0,)), ((), ())), preferred_element_type=F32)


def _mod_kernel(c_ref, w_ref, b_ref, o_ref):
    c = _silu(c_ref[...]).astype(BF16)
    o_ref[...] = _dot(c, w_ref[...].astype(BF16)) + b_ref[...]


def _modulation(cond, w_ada, b_ada):
    n_layers = w_ada.shape[0]
    rows = cond.shape[0]
    tn = D_MODEL
    return pl.pallas_call(
        _mod_kernel,
        grid=(n_layers, 6 * D_MODEL // tn),
        in_specs=[
            pl.BlockSpec((rows, D_MODEL), lambda l, j: (0, 0)),
            pl.BlockSpec((None, D_MODEL, tn), lambda l, j: (l, 0, j)),
            pl.BlockSpec((None, 1, tn), lambda l, j: (l, 0, j)),
        ],
        out_specs=pl.BlockSpec((None, rows, tn), lambda l, j: (l, 0, j)),
        out_shape=jax.ShapeDtypeStruct((n_layers, rows, 6 * D_MODEL), F32),
        compiler_params=_params(("parallel", "parallel")),
        name="modulation",
    )(cond, w_ada, b_ada.reshape(n_layers, 1, 6 * D_MODEL))


def _inproj_kernel(x_ref, xp_ref, xn_ref, mod_ref, g_ref, w1_ref, w2_ref, cw_ref, cb_ref,
                   p1_ref, p2_ref, hext, cbuf, obuf, *, seq_pos):
    tc = T_CHUNK
    sh = mod_ref[:, 0:D_MODEL]
    sc1 = 1.0 + mod_ref[:, D_MODEL:2 * D_MODEL]
    g = g_ref[...]
    h_main = _rms(x_ref[...], g) * sc1 + sh
    hext[SUBLANES:SUBLANES + tc, :] = h_main
    pos, seq_len = seq_pos(pl.program_id(0) * tc)
    head = jnp.where(pos == 0, 0.0, 1.0)
    tail = jnp.where(pos + tc == seq_len, 0.0, 1.0)
    hext[0:SUBLANES, :] = (_rms(xp_ref[...], g) * sc1 + sh) * head
    hext[SUBLANES + tc:, :] = (_rms(xn_ref[...], g) * sc1 + sh) * tail
    he = hext[...].astype(BF16)
    hb = h_main.astype(BF16)

    def conv_columns():
        half = tc // 2
        for c0 in range(0, CONV_COLS, IN_CHUNK):
            res = _dot(he, w1_ref[:, c0:c0 + IN_CHUNK])
            for k in range(IN_CHUNK // LANES):
                cbuf[c0 // LANES + k] = res[:, k * LANES:(k + 1) * LANES]
            yield
            for k in range(IN_CHUNK // LANES):
                slab = c0 // LANES + k
                ls = slice(slab * LANES, (slab + 1) * LANES)
                for parity in range(2):
                    xc = cb_ref[:, ls]
                    for j in range(CONV_W):
                        row0 = SUBLANES - CONV_LEFT + j + parity
                        xc = xc + cw_ref[j:j + 1, ls] * cbuf[slab, pl.ds(row0, half, stride=2), :]
                    obuf[slab, pl.ds(parity, half, stride=2), :] = xc if c0 < D_MODEL else _silu(xc)
                if k % 2 == 1:
                    yield
            for k in range(IN_CHUNK // LANES):
                slab = c0 // LANES + k
                p1_ref[:, slab * LANES:(slab + 1) * LANES] = obuf[slab]
            yield

    def plain_columns():
        for c0 in range(CONV_COLS, P1_COLS, IN_CHUNK):
            cs = slice(c0, min(c0 + IN_CHUNK, P1_COLS))
            p1_ref[:, cs] = _dot(hb, w1_ref[:, cs])
            yield
        for c0 in range(0, P2_COLS, IN_CHUNK):
            cs = slice(c0, c0 + IN_CHUNK)
            p2_ref[:, cs] = _dot(hb, w2_ref[:, cs]).astype(p2_ref.dtype)
            yield

    _round_robin([(conv_columns(), 1), (plain_columns(), 1)])


def _inproj(x, mod, row_of_block, seq_pos, lw, l):
    m = x.shape[0]
    tc = T_CHUNK
    rows8 = tc // SUBLANES
    last8 = m // SUBLANES - 1
    single = dict(pipeline_mode=pl.Buffered(1))
    return pl.pallas_call(
        functools.partial(_inproj_kernel, seq_pos=seq_pos),
        grid=(m // tc,),
        in_specs=[
            pl.BlockSpec((tc, D_MODEL), lambda i: (i, 0)),
            pl.BlockSpec((SUBLANES, D_MODEL), lambda i: (jnp.maximum(i * rows8 - 1, 0), 0)),
            pl.BlockSpec((SUBLANES, D_MODEL), lambda i: (jnp.minimum((i + 1) * rows8, last8), 0)),
            pl.BlockSpec((None, 1, 6 * D_MODEL), lambda i: (row_of_block(i, tc), 0, 0)),
            pl.BlockSpec((1, D_MODEL), lambda i: (0, 0)),
            pl.BlockSpec((None, D_MODEL, P1_COLS), lambda i: (l, 0, 0), **single),
            pl.BlockSpec((None, D_MODEL, P2_COLS), lambda i: (l, 0, 0), **single),
            pl.BlockSpec((CONV_W, CONV_COLS), lambda i: (0, 0)),
            pl.BlockSpec((1, CONV_COLS), lambda i: (0, 0)),
        ],
        out_specs=[pl.BlockSpec((tc, P1_COLS), lambda i: (i, 0)),
                   pl.BlockSpec((tc, P2_COLS), lambda i: (i, 0))],
        out_shape=[jax.ShapeDtypeStruct((m, P1_COLS), F32), jax.ShapeDtypeStruct((m, P2_COLS), BF16)],
        scratch_shapes=[pltpu.VMEM((tc + 2 * SUBLANES, D_MODEL), F32),
                        pltpu.VMEM((CONV_COLS // LANES, tc + 2 * SUBLANES, LANES), F32),
                        pltpu.VMEM((CONV_COLS // LANES, tc, LANES), F32)],
        compiler_params=_params(("parallel",)),
        name="inproj",
    )(x, x, x, mod, lw["norm_mix_g"], lw["w1"], lw["w2"], lw["conv_w"], lw["conv_b"])


def _tile_scan(a, b, rev):
    t, w = a.shape
    a = a.reshape(t // SUBLANES, SUBLANES, w)
    b = b.reshape(t // SUBLANES, SUBLANES, w)
    pos = lax.broadcasted_iota(jnp.int32, (1, SUBLANES, w), 1)
    s = 1
    while s < SUBLANES:
        shift = (SUBLANES - s) if rev else s
        valid = (pos < SUBLANES - s) if rev else (pos >= s)
        a_sh = jnp.where(valid, pltpu.roll(a, shift, 1), 1.0)
        b_sh = jnp.where(valid, pltpu.roll(b, shift, 1), 0.0)
        yield
        b = a * b_sh + b
        a = a * a_sh
        yield
        s *= 2
    return a, b


def _round_robin(tasks):
    tasks = list(tasks)
    while tasks:
        for task in list(tasks):
            gen, stages = task
            try:
                for _ in range(stages):
                    next(gen)
            except StopIteration:
                tasks.remove(task)


def _seg_cumsum(x, seg, rev):
    t, w = x.shape
    ntile = t // SUBLANES
    per_seg = seg // SUBLANES
    x = x.reshape(ntile, SUBLANES, w)
    pos = lax.broadcasted_iota(jnp.int32, (1, SUBLANES, w), 1)
    s = 1
    while s < SUBLANES:
        shift = (SUBLANES - s) if rev else s
        valid = (pos < SUBLANES - s) if rev else (pos >= s)
        x = x + jnp.where(valid, pltpu.roll(x, shift, 1), 0.0)
        s *= 2
    tiles = [None] * ntile
    for s0 in range(0, ntile, per_seg):
        carry = None
        for j in (range(s0 + per_seg - 1, s0 - 1, -1) if rev else range(s0, s0 + per_seg)):
            tiles[j] = x[j] if carry is None else x[j] + carry
            carry = tiles[j][0:1, :] if rev else tiles[j][SUBLANES - 1:SUBLANES, :]
    return jnp.stack(tiles, axis=0).reshape(t, w)


def _ssd_intra(gmat, col, rowv, xhat_bf, tri, rev):
    half = tri.shape[0]
    lo, hi = slice(0, half), slice(half, 2 * half)

    def blk(rs, cs, masked):
        e = col[rs, :] - rowv[:, cs]
        if masked:
            e = jnp.where(tri, e, NEG_BIG)
        return (gmat[rs, cs] * jnp.exp2(e)).astype(BF16)

    if rev:
        out_top = _dot(jnp.concatenate([blk(lo, lo, True), blk(lo, hi, False)], axis=1), xhat_bf)
        yield
        out_bot = _dot(blk(hi, hi, True), xhat_bf[hi, :])
    else:
        out_top = _dot(blk(lo, lo, True), xhat_bf[lo, :])
        yield
        out_bot = _dot(jnp.concatenate([blk(hi, lo, False), blk(hi, hi, True)], axis=1), xhat_bf)
    yield
    return jnp.concatenate([out_top, out_bot], axis=0)


def _ssd_state_slices(h):
    g, hl = divmod(h, SSD_HEADS // SSD_GROUPS)
    return (slice(g * SSD_STATE, (g + 1) * SSD_STATE),
            slice(hl * SSD_HEAD_DIM, (hl + 1) * SSD_HEAD_DIM))


def _scan_kernel(*refs, rev, zero_init, emit_state, alias_out, alias_state, nchunks):
    it = iter(refs)
    p_ref = next(it)
    lruw_ref, lruba_ref, lrubx_ref, lam_ref = next(it), next(it), next(it), next(it)
    dtb_ref, alog_ref, dskip_ref = next(it), next(it), next(it)
    w2_ref, gb_ref = next(it), next(it)
    if not zero_init:
        i_lru, i_ssd, i_gla = next(it), next(it), next(it)
    if alias_out:
        next(it)
    if alias_state:
        next(it), next(it), next(it)
    o_ref = next(it)
    if emit_state:
        s_lru, s_ssd, s_gla = next(it), next(it), next(it)
    h_scr, ssd_scr, gla_scr = next(it), next(it), next(it)
    la_scr, lb_scr, lh_scr = next(it), next(it), next(it)

    tc = T_CHUNK
    c = pl.program_id(1)

    @pl.when(c == 0)
    def _():
        if zero_init:
            h_scr[...] = jnp.zeros_like(h_scr)
            ssd_scr[...] = jnp.zeros_like(ssd_scr)
            gla_scr[...] = jnp.zeros_like(gla_scr)
        else:
            h_scr[...] = jnp.broadcast_to(i_lru[...], h_scr.shape)
            for h in range(SSD_HEADS):
                rs, ls = _ssd_state_slices(h)
                ssd_scr[rs, ls] = i_ssd[h]
            gla_scr[...] = i_gla[...]

    small = p_ref[:, SMALL_OFF:SMALL_OFF + LANES]

    c8h = (-0.5 * LRU_C) * _softplus(-lam_ref[...])
    ntile = tc // SUBLANES

    def lru_group(g):
        sl = slice(g * MXU_DIM, (g + 1) * MXU_DIM)
        xl = p_ref[:, sl]
        pre = _dot(xl.astype(BF16), lruw_ref[g])
        yield
        t_r = jnp.tanh(pre[:, :MXU_DIM] + lruba_ref[:, sl])
        yield
        t_i = jnp.tanh(pre[:, MXU_DIM:] + lrubx_ref[:, sl])
        yield
        log_a = c8h[:, sl] * t_r + c8h[:, sl]
        xlh = 0.5 * xl
        ix = xlh * t_i + xlh
        yield
        a = jnp.exp(log_a)
        th = jnp.tanh(log_a)
        yield
        b = jnp.sqrt(-th * (1.0 + a * a)) * ix
        slabs = range(g * MXU_DIM // LANES, (g + 1) * MXU_DIM // LANES)
        for k, slab in enumerate(slabs):
            la_scr[slab] = a[:, k * LANES:(k + 1) * LANES]
            lb_scr[slab] = b[:, k * LANES:(k + 1) * LANES]
        yield
        nrun = tc // LRU_RUN
        order = range(LRU_RUN - 1, -1, -1) if rev else range(LRU_RUN)
        run_a, run_b = {}, {}
        for k, slab in enumerate(slabs):
            pa = pb = None
            for r in order:
                ar = la_scr[slab, pl.ds(r, nrun, stride=LRU_RUN), :]
                br = lb_scr[slab, pl.ds(r, nrun, stride=LRU_RUN), :]
                if pa is not None:
                    br = ar * pb + br
                    ar = ar * pa
                run_a[k, r], run_b[k, r] = pa, pb = ar, br
        yield
        tot_a = jnp.concatenate([run_a[k, order[-1]] for k in range(len(slabs))], axis=1)
        tot_b = jnp.concatenate([run_b[k, order[-1]] for k in range(len(slabs))], axis=1)
        a_cum, b_cum = yield from _tile_scan(tot_a, tot_b, rev)
        carry_in = h_scr[0:1, sl]
        carry = carry_in
        nt = nrun // SUBLANES
        hs = [None] * nt
        for ti in (range(nt - 1, -1, -1) if rev else range(nt)):
            hs[ti] = a_cum[ti] * carry + b_cum[ti]
            carry = hs[ti][0:1, :] if rev else hs[ti][SUBLANES - 1:SUBLANES, :]
        h_scr[:, sl] = jnp.broadcast_to(carry, (SUBLANES, MXU_DIM))
        yield
        h_out = jnp.stack(hs, axis=0).reshape(nrun, MXU_DIM)
        row = lax.broadcasted_iota(jnp.int32, (nrun, MXU_DIM), 0)
        if rev:
            h_in = jnp.where(row == nrun - 1, carry_in, pltpu.roll(h_out, nrun - 1, 0))
        else:
            h_in = jnp.where(row == 0, carry_in, pltpu.roll(h_out, 1, 0))
        for k, slab in enumerate(slabs):
            hk = h_in[:, k * LANES:(k + 1) * LANES]
            for r in order:
                lh_scr[slab, pl.ds(r, nrun, stride=LRU_RUN), :] = run_a[k, r] * hk + run_b[k, r]
        yield
        for k, slab in enumerate(slabs):
            o_ref[:, g * MXU_DIM + k * LANES:g * MXU_DIM + (k + 1) * LANES] = lh_scr[slab].astype(o_ref.dtype)

    lane = lax.broadcasted_iota(jnp.int32, (1, LANES), 1)
    dt_lo = SMALL_DT + (SSD_HEADS if rev else 0)
    dt_mask = (lane >= dt_lo) & (lane < dt_lo + SSD_HEADS)
    dt = jnp.where(dt_mask, _softplus_log(small + dtb_ref[...]), 0.0)
    la = dt * (-LOG2E * jnp.exp(alog_ref[...]))
    bcum = _seg_cumsum(la, tc, rev)
    bcum_t = bcum.T
    tot = bcum[0:1, :] if rev else bcum[tc - 1:tc, :]

    half = tc // 2
    ri = lax.broadcasted_iota(jnp.int32, (half, half), 0)
    ci = lax.broadcasted_iota(jnp.int32, (half, half), 1)
    tri = (ci >= ri) if rev else (ci <= ri)
    lane_t = lax.broadcasted_iota(jnp.int32, (tc, LANES), 1)
    lo_half = lane_t < SSD_HEAD_DIM
    row_s = lax.broadcasted_iota(jnp.int32, (2 * SSD_STATE, SSD_LANES), 0)
    lane_s = lax.broadcasted_iota(jnp.int32, (1, SSD_LANES), 1) // SSD_HEAD_DIM
    b_off = 2 * D_MODEL
    c_off = b_off + SSD_GROUPS * SSD_STATE
    ssd_upd = {}

    def ssd_group(g):
        pair, gl = divmod(g, 2)
        b_tile = p_ref[:, b_off + pair * LANES:b_off + (pair + 1) * LANES].astype(BF16)
        c_tile = p_ref[:, c_off + pair * LANES:c_off + (pair + 1) * LANES]
        s_pair = ssd_scr[pair * LANES:(pair + 1) * LANES, :]
        gmask = lo_half if gl == 0 else jnp.logical_not(lo_half)
        c_g = jnp.where(gmask, c_tile, 0.0).astype(BF16)
        gmat = _dot_nt(c_g, b_tile)
        inter = _dot(c_g, s_pair.astype(BF16))
        yield
        wx_parts = []
        for hp in range(2):
            h0 = g * 4 + hp * 2
            col0 = bcum[:, dt_lo + h0:dt_lo + h0 + 1]
            col1 = bcum[:, dt_lo + h0 + 1:dt_lo + h0 + 2]
            dtc0 = dt[:, dt_lo + h0:dt_lo + h0 + 1]
            dtc1 = dt[:, dt_lo + h0 + 1:dt_lo + h0 + 2]
            xsl = slice(D_MODEL + h0 * SSD_HEAD_DIM, D_MODEL + (h0 + 2) * SSD_HEAD_DIM)
            xs = p_ref[:, xsl]
            xhat = xs * jnp.where(lo_half, dtc0, dtc1)
            xhat_bf = xhat.astype(BF16)
            yield
            outs = []
            for hh, col in ((0, col0), (1, col1)):
                rowv = bcum_t[dt_lo + h0 + hh:dt_lo + h0 + hh + 1, :]
                outs.append((yield from _ssd_intra(gmat, col, rowv, xhat_bf, tri, rev)))
            o_pair = jnp.where(lo_half, outs[0], outs[1])
            colp = jnp.where(lo_half, col0, col1)
            isl = slice(hp * LANES, (hp + 1) * LANES)
            o_pair = o_pair + jnp.exp2(colp) * inter[:, isl]
            if not rev:
                o_pair = o_pair + dskip_ref[:, h0 * SSD_HEAD_DIM:(h0 + 2) * SSD_HEAD_DIM] * xs
            o_ref[:, xsl] = o_pair.astype(o_ref.dtype)
            yield
            t0 = tot[:, dt_lo + h0:dt_lo + h0 + 1]
            t1 = tot[:, dt_lo + h0 + 1:dt_lo + h0 + 2]
            totp = jnp.where(lane < SSD_HEAD_DIM, t0, t1)
            wx_parts.append((jnp.exp2(totp - colp) * xhat).astype(BF16))
        wx = jnp.concatenate(wx_parts, axis=1)
        ssd_upd[g] = _dot_tn(b_tile, wx)
        yield
        d = jnp.zeros((1, SSD_LANES), F32)
        for hl in range(4):
            th = tot[:, dt_lo + g * 4 + hl:dt_lo + g * 4 + hl + 1]
            d = jnp.where(lane_s == hl, jnp.exp2(th), d)
        ssd_upd[("dec", g)] = d
        if gl == 1:
            first = row_s < SSD_STATE
            new = jnp.where(first, ssd_upd[g - 1], ssd_upd[g])
            dec = jnp.where(first, ssd_upd[("dec", g - 1)], d)
            ssd_scr[pair * LANES:(pair + 1) * LANES, :] = dec * s_pair + new

    small_bf = small.astype(BF16)
    gla_bg = {}

    def gla_log_decay(hd):
        ksl = slice(hd * GLA_HEAD_K, (hd + 1) * GLA_HEAD_K)
        z = _dot(small_bf, w2_ref[:, ksl]) + gb_ref[:, ksl]
        yield
        ld = -_softplus_log(-z) * (LOG2E / GLA_GATE_NORM)
        yield
        gla_bg[hd] = _seg_cumsum(ld, GLA_SUB, rev)
        yield

    nsub = tc // GLA_SUB
    nblk = GLA_SUB // GLA_BLK
    scale = GLA_HEAD_K ** -0.5
    gla_st = [gla_scr[hd * GLA_HEAD_K:(hd + 1) * GLA_HEAD_K, :] for hd in range(GLA_HEADS)]

    def gla_subchunk(sc_i, hd):
        ksl = slice(hd * GLA_HEAD_K, (hd + 1) * GLA_HEAD_K)
        st = gla_st[hd]
        rsl = slice(sc_i * GLA_SUB, (sc_i + 1) * GLA_SUB)
        q = p_ref[rsl, Q_OFF + hd * GLA_HEAD_K:Q_OFF + (hd + 1) * GLA_HEAD_K] * scale
        k = p_ref[rsl, K_OFF + hd * GLA_HEAD_K:K_OFF + (hd + 1) * GLA_HEAD_K]
        v = p_ref[rsl, V_OFF + hd * GLA_HEAD_V:V_OFF + (hd + 1) * GLA_HEAD_V].astype(BF16)
        b = gla_bg[hd][rsl, :]
        inter = _dot((q * jnp.exp2(b)).astype(BF16), st.astype(BF16))
        yield
        o_rows = []
        for blk in range(nblk):
            bsl = slice(blk * GLA_BLK, (blk + 1) * GLA_BLK)
            if rev:
                keys = slice(blk * GLA_BLK, GLA_SUB)
                ref_row = b[(blk + 1) * GLA_BLK - 1:(blk + 1) * GLA_BLK, :]
            else:
                keys = slice(0, (blk + 1) * GLA_BLK)
                ref_row = b[blk * GLA_BLK:blk * GLA_BLK + 1, :]
            nk = keys.stop - keys.start
            qt = (q[bsl, :] * jnp.exp2(b[bsl, :] - ref_row)).astype(BF16)
            kt = (k[keys, :] * jnp.exp2(ref_row - b[keys, :])).astype(BF16)
            a_blk = _dot_nt(qt, kt)
            yield
            ai = lax.broadcasted_iota(jnp.int32, (GLA_BLK, nk), 0)
            aj = lax.broadcasted_iota(jnp.int32, (GLA_BLK, nk), 1)
            vis = (aj >= ai) if rev else (aj <= ai + blk * GLA_BLK)
            a_blk = jnp.where(vis, a_blk, 0.0).astype(BF16)
            o_rows.append(inter[bsl, :] + _dot(a_blk, v[keys, :]))
            yield
        o = jnp.concatenate(o_rows, axis=0)
        o_ref[rsl, 2 * D_MODEL + hd * GLA_HEAD_V:2 * D_MODEL + (hd + 1) * GLA_HEAD_V] = o.astype(o_ref.dtype)
        end = b[0:1, :] if rev else b[GLA_SUB - 1:GLA_SUB, :]
        k_dec = (k * jnp.exp2(end - b)).astype(BF16)
        dec_col = jnp.broadcast_to(jnp.exp2(end), (GLA_HEAD_K, GLA_HEAD_K)).T[:, 0:1]
        gla_st[hd] = dec_col * st + _dot_tn(k_dec, v)
        yield

    def gla_head(hd):
        yield from gla_log_decay(hd)
        for step in range(nsub):
            yield from gla_subchunk((nsub - 1 - step) if rev else step, hd)

    def lru_all():
        for g in range(D_MODEL // MXU_DIM):
            yield from lru_group(g)

    def ssd_all():
        for g in range(SSD_GROUPS):
            yield from ssd_group(g)

    _round_robin([(gla_head(hd), 1) for hd in range(GLA_HEADS)] + [(ssd_all(), 2), (lru_all(), 1)])
    for hd in range(GLA_HEADS):
        gla_scr[hd * GLA_HEAD_K:(hd + 1) * GLA_HEAD_K, :] = gla_st[hd]

    if emit_state:
        @pl.when(c == nchunks - 1)
        def _():
            s_lru[...] = h_scr[0:1, :]
            for h in range(SSD_HEADS):
                rs, ls = _ssd_state_slices(h)
                s_ssd[h] = ssd_scr[rs, ls]
            s_gla[...] = gla_scr[...]


def _scan(p1, tok0, nseq, nchunks, lw, l, d, init, o_prev, st_prev, st_shapes):
    rev = d == 1
    tc = T_CHUNK
    blk0 = tok0 // tc
    m = p1.shape[0]

    def chunk_of(c):
        return (nchunks - 1 - c) if rev else c

    const2 = lambda b, c: (0, 0)
    const3 = lambda b, c: (0, 0, 0)
    in_specs = [
        pl.BlockSpec((tc, P1_COLS), lambda b, c: (blk0 + b * nchunks + chunk_of(c), 0)),
        pl.BlockSpec((D_MODEL // MXU_DIM, MXU_DIM, 2 * MXU_DIM), const3),
        pl.BlockSpec((1, D_MODEL), const2),
        pl.BlockSpec((1, D_MODEL), const2),
        pl.BlockSpec((1, D_MODEL), const2),
        pl.BlockSpec((1, LANES), const2),
        pl.BlockSpec((1, LANES), const2),
        pl.BlockSpec((1, D_MODEL), const2),
        pl.BlockSpec((LANES, GLA_DK), const2),
        pl.BlockSpec((1, GLA_DK), const2),
    ]
    args = [p1, lw["lru_w"][d], lw["lru_ba"][d], lw["lru_bx"][d], lw["lam"][d], lw["dt_bias"][d],
            lw["a_log"][d], lw["d_skip"], lw["gate_w2"][d], lw["gate_b"][d]]
    state_specs = [
        pl.BlockSpec((None, None, None, 1, D_MODEL), lambda b, c: (b, l, d, 0, 0)),
        pl.BlockSpec((None, None, None, SSD_HEADS, SSD_STATE, SSD_HEAD_DIM), lambda b, c: (b, l, d, 0, 0, 0)),
        pl.BlockSpec((None, None, None, GLA_ROWS, GLA_HEAD_V), lambda b, c: (b, l, d, 0, 0)),
    ]
    if init is not None:
        in_specs += state_specs
        args += list(init)
    aliases = {}
    if o_prev is not None:
        aliases[len(args)] = 0
        in_specs.append(pl.BlockSpec(memory_space=pl.ANY))
        args.append(o_prev)
    if st_prev is not None:
        for k, arr in enumerate(st_prev):
            aliases[len(args)] = 1 + k
            in_specs.append(pl.BlockSpec(memory_space=pl.ANY))
            args.append(arr)
    out_specs = [pl.BlockSpec((tc, 3 * D_MODEL), lambda b, c: (blk0 + b * nchunks + chunk_of(c), 0))]
    out_shape = [jax.ShapeDtypeStruct((m, 3 * D_MODEL), BF16)]
    if st_shapes is not None:
        out_specs += state_specs
        out_shape += [jax.ShapeDtypeStruct(s, F32) for s in st_shapes]
    kern = functools.partial(_scan_kernel, rev=rev, zero_init=init is None, emit_state=st_shapes is not None,
                             alias_out=o_prev is not None, alias_state=st_prev is not None, nchunks=nchunks)
    return pl.pallas_call(
        kern,
        grid=(nseq, nchunks),
        in_specs=in_specs,
        out_specs=out_specs,
        out_shape=out_shape,
        input_output_aliases=aliases,
        scratch_shapes=[
            pltpu.VMEM((SUBLANES, D_MODEL), F32),
            pltpu.VMEM((SSD_ROWS, SSD_LANES), F32),
            pltpu.VMEM((GLA_ROWS, GLA_HEAD_V), F32),
            pltpu.VMEM((D_MODEL // LANES, tc, LANES), F32),
            pltpu.VMEM((D_MODEL // LANES, tc, LANES), F32),
            pltpu.VMEM((D_MODEL // LANES, tc, LANES), F32),
        ],
        compiler_params=_params(("parallel", "arbitrary")),
        name="scan_rev" if rev else "scan_fwd",
    )(*args)


def _post_kernel(x_ref, of_ref, ob_ref, p2_ref, mod_ref, ssdg_ref, glag_ref,
                 wl_ref, ws_ref, wg_ref, wo_ref, o_ref):
    dm = D_MODEL

    def both(sl):
        return of_ref[:, sl].astype(F32) + ob_ref[:, sl].astype(F32)

    def p2(k):
        return p2_ref[:, k * dm:(k + 1) * dm].astype(F32)

    y_lru = both(slice(0, dm)) * _gelu_tanh(p2(0))
    y_ssd = _rms(both(slice(dm, 2 * dm)) * _silu(p2(1)), ssdg_ref[...])
    parts = []
    for hd in range(GLA_HEADS):
        sl = slice(2 * dm + hd * GLA_HEAD_V, 2 * dm + (hd + 1) * GLA_HEAD_V)
        parts.append(_rms(both(sl), glag_ref[...]))
    y_gla = jnp.concatenate(parts, axis=1) * _silu(p2(2))
    merged = (_sigmoid(p2(3)) * _dot(y_lru.astype(BF16), wl_ref[...])
              + _sigmoid(p2(4)) * _dot(y_ssd.astype(BF16), ws_ref[...])
              + _sigmoid(p2(5)) * _dot(y_gla.astype(BF16), wg_ref[...]))
    out = _dot(merged.astype(BF16), wo_ref[...])
    o_ref[...] = x_ref[...] + mod_ref[:, 2 * dm:3 * dm] * out


def _post(x, o_f, o_b, p2, mod, row_of_block, lw, l):
    m = x.shape[0]
    tm = TM_POST
    wspec = pl.BlockSpec((None, D_MODEL, D_MODEL), lambda i: (l, 0, 0), pipeline_mode=pl.Buffered(1))
    return pl.pallas_call(
        _post_kernel,
        grid=(m // tm,),
        in_specs=[
            pl.BlockSpec((tm, D_MODEL), lambda i: (i, 0)),
            pl.BlockSpec((tm, 3 * D_MODEL), lambda i: (i, 0)),
            pl.BlockSpec((tm, 3 * D_MODEL), lambda i: (i, 0)),
            pl.BlockSpec((tm, P2_COLS), lambda i: (i, 0)),
            pl.BlockSpec((None, 1, 6 * D_MODEL), lambda i: (row_of_block(i, tm), 0, 0)),
            pl.BlockSpec((1, D_MODEL), lambda i: (0, 0)),
            pl.BlockSpec((1, GLA_HEAD_V), lambda i: (0, 0)),
            wspec, wspec, wspec, wspec,
        ],
        out_specs=pl.BlockSpec((tm, D_MODEL), lambda i: (i, 0)),
        out_shape=jax.ShapeDtypeStruct((m, D_MODEL), F32),
        compiler_params=_params(("parallel",)),
        name="post",
    )(x, o_f, o_b, p2, mod, lw["ssd_norm_g"], lw["gla_norm_g"],
      lw["w_br_lru"], lw["w_br_ssd"], lw["w_br_gla"], lw["w_out"])


def _ffn_kernel(x_ref, mod_ref, g_ref, wi_ref, wo_ref, fg_ref, o_ref, *, final_norm, nsplit):
    dm = D_MODEL
    x = x_ref[...]
    h = (_rms(x, g_ref[...]) * (1.0 + mod_ref[:, 4 * dm:5 * dm]) + mod_ref[:, 3 * dm:4 * dm]).astype(BF16)
    tf = D_FF // nsplit
    acc = jnp.zeros(x.shape, F32)
    for s in range(nsplit):
        fs = slice(s * tf, (s + 1) * tf)
        gate = _dot(h, wi_ref[:, fs])
        up = _dot(h, wi_ref[:, D_FF + s * tf:D_FF + (s + 1) * tf])
        acc = acc + _dot((_silu(gate) * up).astype(BF16), wo_ref[fs, :])
    y = x + mod_ref[:, 5 * dm:6 * dm] * acc
    if final_norm:
        y = _rms(y, fg_ref[...])
    o_ref[...] = y


def _ffn(x, mod, row_of_block, lw, l, final_g, final_norm):
    m = x.shape[0]
    tm = TM_FFN
    single = dict(pipeline_mode=pl.Buffered(1))
    kern = functools.partial(_ffn_kernel, final_norm=final_norm, nsplit=2)
    return pl.pallas_call(
        kern,
        grid=(m // tm,),
        in_specs=[
            pl.BlockSpec((tm, D_MODEL), lambda i: (i, 0)),
            pl.BlockSpec((None, 1, 6 * D_MODEL), lambda i: (row_of_block(i, tm), 0, 0)),
            pl.BlockSpec((1, D_MODEL), lambda i: (0, 0)),
            pl.BlockSpec((None, D_MODEL, 2 * D_FF), lambda i: (l, 0, 0), **single),
            pl.BlockSpec((None, D_FF, D_MODEL), lambda i: (l, 0, 0), **single),
            pl.BlockSpec((1, D_MODEL), lambda i: (0, 0)),
        ],
        out_specs=pl.BlockSpec((tm, D_MODEL), lambda i: (i, 0)),
        out_shape=jax.ShapeDtypeStruct((m, D_MODEL), F32),
        compiler_params=_params(("parallel",)),
        name="ffn",
    )(x, mod, lw["norm_ffn_g"], lw["w_ffn_in"], lw["w_ffn_out"], final_g)


def _snake_rows(src_ref, dst_ref):
    rows = src_ref.shape[0]
    for g0 in range(0, rows, 2 * GRID_W):
        dst_ref[g0:g0 + GRID_W, :] = src_ref[g0:g0 + GRID_W, :]
        for r in range(GRID_W):
            dst_ref[pl.ds(g0 + GRID_W + r, 1), :] = src_ref[pl.ds(g0 + 2 * GRID_W - 1 - r, 1), :]


def _assemble_kernel(xa_ref, xb_ref, o_ref, *, nb_ctx):
    i = pl.program_id(0)

    @pl.when(i < nb_ctx)
    def _():
        o_ref[...] = xa_ref[...]

    @pl.when(i >= nb_ctx)
    def _():
        _snake_rows(xb_ref, o_ref)


def _split_kernel(y_ref, ya_ref, yb_ref, *, nb_ctx):
    i = pl.program_id(0)

    @pl.when(i < nb_ctx)
    def _():
        ya_ref[...] = y_ref[...]

    @pl.when(i >= nb_ctx)
    def _():
        _snake_rows(y_ref, yb_ref)


def _assemble_tokens(x_ctx, x_lat):
    tr = T_CHUNK
    nb_ctx, nb_lat = x_ctx.shape[0] // tr, x_lat.shape[0] // tr
    return pl.pallas_call(
        functools.partial(_assemble_kernel, nb_ctx=nb_ctx),
        grid=(nb_ctx + nb_lat,),
        in_specs=[pl.BlockSpec((tr, D_MODEL), lambda i: (jnp.minimum(i, nb_ctx - 1), 0)),
                  pl.BlockSpec((tr, D_MODEL), lambda i: (jnp.maximum(i - nb_ctx, 0), 0))],
        out_specs=pl.BlockSpec((tr, D_MODEL), lambda i: (i, 0)),
        out_shape=jax.ShapeDtypeStruct((x_ctx.shape[0] + x_lat.shape[0], D_MODEL), x_ctx.dtype),
        compiler_params=_params(("arbitrary",)),
        name="assemble_tokens",
    )(x_ctx, x_lat)


def _split_tokens(y, m_ctx):
    tr = T_CHUNK
    nb_ctx, nb_lat = m_ctx // tr, (y.shape[0] - m_ctx) // tr
    return pl.pallas_call(
        functools.partial(_split_kernel, nb_ctx=nb_ctx),
        grid=(nb_ctx + nb_lat,),
        in_specs=[pl.BlockSpec((tr, D_MODEL), lambda i: (i, 0))],
        out_specs=[pl.BlockSpec((tr, D_MODEL), lambda i: (jnp.minimum(i, nb_ctx - 1), 0)),
                   pl.BlockSpec((tr, D_MODEL), lambda i: (jnp.maximum(i - nb_ctx, 0), 0))],
        out_shape=[jax.ShapeDtypeStruct((m_ctx, D_MODEL), y.dtype),
                   jax.ShapeDtypeStruct((y.shape[0] - m_ctx, D_MODEL), y.dtype)],
        compiler_params=_params(("arbitrary",)),
        name="split_tokens",
    )(y)


_O_GATE, _O_Z, _O_XBC = D_MODEL, 2 * D_MODEL, 3 * D_MODEL
_O_DT = _O_XBC + CONV_COLS - D_MODEL
_O_Q = _O_DT + 2 * SSD_HEADS
_O_G = _O_Q + 2 * GLA_DK + D_MODEL
_O_LR = _O_G + D_MODEL
_O_M = _O_LR + 2 * GLA_GATE_RANK
IN_WIDTH = _O_M + 3 * D_MODEL
W_PREP_ROWS = 128


def _wprep_kernel(w_ref, w1_ref, w2_ref):
    def cols(lo, hi):
        return w_ref[:, lo:hi].astype(BF16)

    w1_ref[:, 0:D_MODEL] = cols(0, D_MODEL)
    w1_ref[:, D_MODEL:CONV_COLS] = cols(_O_XBC, _O_DT)
    w1_ref[:, Q_OFF:SMALL_OFF] = cols(_O_Q, _O_G)
    assert _O_DT % LANES == SMALL_DT and _O_LR % LANES == SMALL_LR
    lane = lax.broadcasted_iota(jnp.int32, (w_ref.shape[0], LANES), 1)
    dt_tile = w_ref[:, _O_DT:_O_DT + LANES]
    lr_tile = w_ref[:, _O_LR - SMALL_LR:_O_LR - SMALL_LR + LANES]
    small = jnp.where(lane < SMALL_LR, dt_tile, jnp.where(lane < 2 * SMALL_LR, lr_tile, 0.0))
    w1_ref[:, SMALL_OFF:] = small.astype(BF16)
    w2_ref[:, 0:2 * D_MODEL] = cols(_O_GATE, _O_XBC)
    w2_ref[:, 2 * D_MODEL:3 * D_MODEL] = cols(_O_G, _O_LR)
    w2_ref[:, 3 * D_MODEL:] = cols(_O_M, IN_WIDTH)


def _prep_w_in(w_in):
    n_layers = w_in.shape[0]
    tr = W_PREP_ROWS
    return pl.pallas_call(
        _wprep_kernel,
        grid=(n_layers, D_MODEL // tr),
        in_specs=[pl.BlockSpec((None, tr, IN_WIDTH), lambda l, r: (l, r, 0))],
        out_specs=[pl.BlockSpec((None, tr, P1_COLS), lambda l, r: (l, r, 0)),
                   pl.BlockSpec((None, tr, P2_COLS), lambda l, r: (l, r, 0))],
        out_shape=[jax.ShapeDtypeStruct((n_layers, D_MODEL, P1_COLS), BF16),
                   jax.ShapeDtypeStruct((n_layers, D_MODEL, P2_COLS), BF16)],
        compiler_params=_params(("parallel", "parallel")),
        name="w_in_relayout",
    )(w_in)


def _layer_weights(l, w):
    dm = D_MODEL

    def block_diag_tiles(wa, wx):
        per = MXU_DIM // LRU_BLOCK_W
        eye = jnp.eye(per, dtype=wa.dtype)
        def tiles(wb):
            wb = wb.reshape(dm // MXU_DIM, per, LRU_BLOCK_W, LRU_BLOCK_W)
            t = jnp.einsum("gpwv,pq->gpwqv", wb, eye)
            return t.reshape(dm // MXU_DIM, MXU_DIM, MXU_DIM)
        return jnp.concatenate([tiles(wa), tiles(wx)], axis=2).astype(BF16)

    def lane_pad(v, lo):
        return jnp.zeros((1, LANES), F32).at[0, lo:lo + v.shape[0]].set(v)

    def gate_w2(d):
        lo = SMALL_LR + d * GLA_GATE_RANK
        return jnp.zeros((LANES, GLA_DK), F32).at[lo:lo + GLA_GATE_RANK].set(w["gla_gate_w2"][l, d]).astype(BF16)

    return dict(
        w1=w["w1"], w2=w["w2"],
        norm_mix_g=w["norm_mix_g"][l][None], norm_ffn_g=w["norm_ffn_g"][l][None],
        conv_w=jnp.concatenate([w["lru_conv_w"][l], w["ssd_conv_w"][l]], axis=1),
        conv_b=jnp.concatenate([w["lru_conv_b"][l], w["ssd_conv_b"][l]])[None],
        lru_w=[block_diag_tiles(0.5 * w["lru_w_a"][l, d], 0.5 * w["lru_w_x"][l, d]) for d in range(2)],
        lru_ba=[0.5 * w["lru_b_a"][l, d][None] for d in range(2)],
        lru_bx=[0.5 * w["lru_b_x"][l, d][None] for d in range(2)],
        lam=[w["lru_lambda"][l, d][None] for d in range(2)],
        dt_bias=[lane_pad(w["ssd_dt_bias"][l, d], SMALL_DT + d * SSD_HEADS) for d in range(2)],
        a_log=[lane_pad(w["ssd_a_log"][l, d], SMALL_DT + d * SSD_HEADS) for d in range(2)],
        d_skip=jnp.repeat(w["ssd_d"][l], SSD_HEAD_DIM)[None],
        gate_w2=[gate_w2(d) for d in range(2)],
        gate_b=[w["gla_gate_b"][l, d][None] for d in range(2)],
        ssd_norm_g=w["ssd_norm_g"][l][None], gla_norm_g=w["gla_norm_g"][l][None],
        w_br_lru=w["w_br_lru_bf"], w_br_ssd=w["w_br_ssd_bf"], w_br_gla=w["w_br_gla_bf"], w_out=w["w_out_bf"],
        w_ffn_in=w["w_ffn_in_bf"], w_ffn_out=w["w_ffn_out_bf"],
    )


def kernel(x_prompt, x_sample, state_lru, state_ssd, state_gla, c, c_ctx, norm_mix_g, norm_ffn_g, w_ada, b_ada, w_in, lru_conv_w, lru_conv_b, lru_w_a, lru_b_a, lru_w_x, lru_b_x, lru_lambda, ssd_conv_w, ssd_conv_b, ssd_dt_bias, ssd_a_log, ssd_d, ssd_norm_g, gla_gate_w2, gla_gate_b, gla_norm_g, w_br_lru, w_br_ssd, w_br_gla, w_out, w_ffn_in, w_ffn_out, final_norm_g):
    w = dict(norm_mix_g=norm_mix_g, norm_ffn_g=norm_ffn_g, w_in=w_in, lru_conv_w=lru_conv_w,
             lru_conv_b=lru_conv_b, lru_w_a=lru_w_a, lru_b_a=lru_b_a, lru_w_x=lru_w_x, lru_b_x=lru_b_x,
             lru_lambda=lru_lambda, ssd_conv_w=ssd_conv_w, ssd_conv_b=ssd_conv_b, ssd_dt_bias=ssd_dt_bias,
             ssd_a_log=ssd_a_log, ssd_d=ssd_d, ssd_norm_g=ssd_norm_g, gla_gate_w2=gla_gate_w2,
             gla_gate_b=gla_gate_b, gla_norm_g=gla_norm_g, w_br_lru=w_br_lru, w_br_ssd=w_br_ssd,
             w_br_gla=w_br_gla, w_out=w_out, w_ffn_in=w_ffn_in, w_ffn_out=w_ffn_out)
    w["w1"], w["w2"] = _prep_w_in(w_in)
    for name in ("w_br_lru", "w_br_ssd", "w_br_gla", "w_out", "w_ffn_in", "w_ffn_out"):
        w[name + "_bf"] = w[name].astype(BF16)
    n_layers = w_in.shape[0]
    b_ctx, t_ctx, dm = x_prompt.shape
    b_lat, t_lat, _ = x_sample.shape
    m_ctx = b_ctx * t_ctx
    m_lat = b_lat * t_lat
    nch_ctx = t_ctx // T_CHUNK
    nch_lat = t_lat // T_CHUNK

    n_rows = -(-(1 + b_lat) // SUBLANES) * SUBLANES
    cond = jnp.zeros((n_rows, dm), F32).at[0].set(c_ctx).at[1:1 + b_lat].set(c)
    mod = _modulation(cond, w_ada, b_ada)

    def row_of_block(i, tm):
        tok = i * tm
        return jnp.where(tok < m_ctx, 0, 1 + (tok - m_ctx) // t_lat)

    def seq_pos(tok):
        in_ctx = tok < m_ctx
        pos = jnp.where(in_ctx, lax.rem(tok, t_ctx), lax.rem(tok - m_ctx, t_lat))
        return pos, jnp.where(in_ctx, t_ctx, t_lat)

    x = _assemble_tokens(x_prompt.reshape(m_ctx, dm), x_sample.reshape(m_lat, dm))

    st_shapes = [(b_ctx, n_layers, 2, 1, dm),
                 (b_ctx, n_layers, 2, SSD_HEADS, SSD_STATE, SSD_HEAD_DIM),
                 (b_ctx, n_layers, 2, GLA_ROWS, GLA_HEAD_V)]
    init = (state_lru.reshape(b_lat, n_layers, 2, 1, dm), state_ssd,
            state_gla.reshape(b_lat, n_layers, 2, GLA_ROWS, GLA_HEAD_V))
    states = None
    for l in range(n_layers):
        lw = _layer_weights(l, w)
        mod_l = mod[l][:, None, :]
        p1, p2 = _inproj(x, mod_l, row_of_block, seq_pos, lw, l)
        outs = []
        for d in range(2):
            o, *states = _scan(p1, 0, b_ctx, nch_ctx, lw, l, d, None, None, states, st_shapes)
            (o,) = _scan(p1, m_ctx, b_lat, nch_lat, lw, l, d, init, o, None, None)
            outs.append(o)
        x = _post(x, outs[0], outs[1], p2, mod_l, row_of_block, lw, l)
        x = _ffn(x, mod_l, row_of_block, lw, l, final_norm_g[None], l == n_layers - 1)

    y_ctx, y_lat = _split_tokens(x, m_ctx)
    y_prompt = y_ctx.reshape(b_ctx, t_ctx, dm)
    y_sample = y_lat.reshape(b_lat, t_lat, dm)
    return (y_prompt, y_sample, states[0].reshape(b_ctx, n_layers, 2, dm), states[1],
            states[2].reshape(b_ctx, n_layers, 2, GLA_HEADS, GLA_HEAD_K, GLA_HEAD_V))
```

```python
import functools

import jax
import jax.numpy as jnp
from jax import lax
from jax.experimental import pallas as pl
from jax.experimental.pallas import tpu as pltpu

F32 = jnp.float32
BF16 = jnp.bfloat16

D_MODEL = 1024
GRID_W = 64
CONV_W = 4
CONV_LEFT = 2
LRU_BLOCK_W = 64
LRU_C = 8.0
SSD_HEAD_DIM = 64
SSD_HEADS = 16
SSD_GROUPS = 4
SSD_STATE = 64
GLA_HEADS = 4
GLA_HEAD_K = 128
GLA_HEAD_V = 256
GLA_DK = GLA_HEADS * GLA_HEAD_K
GLA_GATE_RANK = 16
GLA_GATE_NORM = 16.0
D_FF = 2816
EPS = 1e-6

LANES = 128
SUBLANES = 8
MXU_DIM = 256
VMEM_LIMIT_BYTES = 60 * 1024 * 1024

CONV_COLS = D_MODEL + D_MODEL + 2 * SSD_GROUPS * SSD_STATE
Q_OFF = CONV_COLS
K_OFF = Q_OFF + GLA_DK
V_OFF = K_OFF + GLA_DK
SMALL_OFF = V_OFF + D_MODEL
P1_COLS = SMALL_OFF + LANES
P2_COLS = 6 * D_MODEL
SMALL_DT = 0
SMALL_LR = 2 * SSD_HEADS

T_CHUNK = 256
IN_CHUNK = 512
TM_POST = 512
TM_FFN = 512
LRU_RUN = 4
GLA_SUB = 64
GLA_BLK = 16
NEG_BIG = -1e30
LOG2E = 1.4426950408889634
SSD_ROWS = SSD_GROUPS * SSD_STATE
SSD_LANES = (SSD_HEADS // SSD_GROUPS) * SSD_HEAD_DIM
GLA_ROWS = GLA_HEADS * GLA_HEAD_K


def _params(sem, **kw):
    return pltpu.CompilerParams(dimension_semantics=sem, vmem_limit_bytes=VMEM_LIMIT_BYTES, **kw)


def _softplus(x):
    return jnp.maximum(x, 0.0) + jnp.log1p(jnp.exp(-jnp.abs(x)))


def _softplus_log(x):
    return jnp.maximum(x, 0.0) + jnp.log(1.0 + jnp.exp(-jnp.abs(x)))


def _sigmoid(x):
    return 0.5 * (1.0 + jnp.tanh(0.5 * x))


def _silu(x):
    return x * _sigmoid(x)


def _gelu_tanh(x):
    c = 0.7978845608028654
    return 0.5 * x * (1.0 + jnp.tanh(c * (x + 0.044715 * (x * x * x))))


def _rms(x, g):
    return x * lax.rsqrt(jnp.mean(x * x, axis=-1, keepdims=True) + EPS) * g


def _dot(a, b):
    return jnp.dot(a, b, preferred_element_type=F32)


def _dot_nt(a, b):
    return lax.dot_general(a, b, (((1,), (1,)), ((), ())), preferred_element_type=F32)


def _dot_tn(a, b):
    return lax.dot_general(a, b, (((0,), (0,)), ((), ())), preferred_element_type=F32)


def _mod_kernel(c_ref, w_ref, b_ref, o_ref):
    c = _silu(c_ref[...]).astype(BF16)
    o_ref[...] = _dot(c, w_ref[...].astype(BF16)) + b_ref[...]


def _modulation(cond, w_ada, b_ada):
    n_layers = w_ada.shape[0]
    rows = cond.shape[0]
    tn = D_MODEL
    return pl.pallas_call(
        _mod_kernel,
        grid=(n_layers, 6 * D_MODEL // tn),
        in_specs=[
            pl.BlockSpec((rows, D_MODEL), lambda l, j: (0, 0)),
            pl.BlockSpec((None, D_MODEL, tn), lambda l, j: (l, 0, j)),
            pl.BlockSpec((None, 1, tn), lambda l, j: (l, 0, j)),
        ],
        out_specs=pl.BlockSpec((None, rows, tn), lambda l, j: (l, 0, j)),
        out_shape=jax.ShapeDtypeStruct((n_layers, rows, 6 * D_MODEL), F32),
        compiler_params=_params(("parallel", "parallel")),
        name="modulation",
    )(cond, w_ada, b_ada.reshape(n_layers, 1, 6 * D_MODEL))


def _inproj_kernel(x_ref, xp_ref, xn_ref, mod_ref, g_ref, w1_ref, w2_ref, cw_ref, cb_ref,
                   p1_ref, p2_ref, hext, cbuf, obuf, *, seq_pos):
    tc = T_CHUNK
    sh = mod_ref[:, 0:D_MODEL]
    sc1 = 1.0 + mod_ref[:, D_MODEL:2 * D_MODEL]
    g = g_ref[...]
    h_main = _rms(x_ref[...], g) * sc1 + sh
    hext[SUBLANES:SUBLANES + tc, :] = h_main
    pos, seq_len = seq_pos(pl.program_id(0) * tc)
    head = jnp.where(pos == 0, 0.0, 1.0)
    tail = jnp.where(pos + tc == seq_len, 0.0, 1.0)
    hext[0:SUBLANES, :] = (_rms(xp_ref[...], g) * sc1 + sh) * head
    hext[SUBLANES + tc:, :] = (_rms(xn_ref[...], g) * sc1 + sh) * tail
    he = hext[...].astype(BF16)
    hb = h_main.astype(BF16)

    def conv_columns():
        half = tc // 2
        for c0 in range(0, CONV_COLS, IN_CHUNK):
            res = _dot(he, w1_ref[:, c0:c0 + IN_CHUNK])
            for k in range(IN_CHUNK // LANES):
                cbuf[c0 // LANES + k] = res[:, k * LANES:(k + 1) * LANES]
            yield
            for k in range(IN_CHUNK // LANES):
                slab = c0 // LANES + k
                ls = slice(slab * LANES, (slab + 1) * LANES)
                for parity in range(2):
                    xc = cb_ref[:, ls]
                    for j in range(CONV_W):
                        row0 = SUBLANES - CONV_LEFT + j + parity
                        xc = xc + cw_ref[j:j + 1, ls] * cbuf[slab, pl.ds(row0, half, stride=2), :]
                    obuf[slab, pl.ds(parity, half, stride=2), :] = xc if c0 < D_MODEL else _silu(xc)
                if k % 2 == 1:
                    yield
            for k in range(IN_CHUNK // LANES):
                slab = c0 // LANES + k
                p1_ref[:, slab * LANES:(slab + 1) * LANES] = obuf[slab]
            yield

    def plain_columns():
        for c0 in range(CONV_COLS, P1_COLS, IN_CHUNK):
            cs = slice(c0, min(c0 + IN_CHUNK, P1_COLS))
            p1_ref[:, cs] = _dot(hb, w1_ref[:, cs])
            yield
        for c0 in range(0, P2_COLS, IN_CHUNK):
            cs = slice(c0, c0 + IN_CHUNK)
            p2_ref[:, cs] = _dot(hb, w2_ref[:, cs]).astype(p2_ref.dtype)
            yield

    _round_robin([(conv_columns(), 1), (plain_columns(), 1)])


def _inproj(x, mod, row_of_block, seq_pos, lw, l):
    m = x.shape[0]
    tc = T_CHUNK
    rows8 = tc // SUBLANES
    last8 = m // SUBLANES - 1
    single = dict(pipeline_mode=pl.Buffered(1))
    return pl.pallas_call(
        functools.partial(_inproj_kernel, seq_pos=seq_pos),
        grid=(m // tc,),
        in_specs=[
            pl.BlockSpec((tc, D_MODEL), lambda i: (i, 0)),
            pl.BlockSpec((SUBLANES, D_MODEL), lambda i: (jnp.maximum(i * rows8 - 1, 0), 0)),
            pl.BlockSpec((SUBLANES, D_MODEL), lambda i: (jnp.minimum((i + 1) * rows8, last8), 0)),
            pl.BlockSpec((None, 1, 6 * D_MODEL), lambda i: (row_of_block(i, tc), 0, 0)),
            pl.BlockSpec((1, D_MODEL), lambda i: (0, 0)),
            pl.BlockSpec((None, D_MODEL, P1_COLS), lambda i: (l, 0, 0), **single),
            pl.BlockSpec((None, D_MODEL, P2_COLS), lambda i: (l, 0, 0), **single),
            pl.BlockSpec((CONV_W, CONV_COLS), lambda i: (0, 0)),
            pl.BlockSpec((1, CONV_COLS), lambda i: (0, 0)),
        ],
        out_specs=[pl.BlockSpec((tc, P1_COLS), lambda i: (i, 0)),
                   pl.BlockSpec((tc, P2_COLS), lambda i: (i, 0))],
        out_shape=[jax.ShapeDtypeStruct((m, P1_COLS), F32), jax.ShapeDtypeStruct((m, P2_COLS), BF16)],
        scratch_shapes=[pltpu.VMEM((tc + 2 * SUBLANES, D_MODEL), F32),
                        pltpu.VMEM((CONV_COLS // LANES, tc + 2 * SUBLANES, LANES), F32),
                        pltpu.VMEM((CONV_COLS // LANES, tc, LANES), F32)],
        compiler_params=_params(("parallel",)),
        name="inproj",
    )(x, x, x, mod, lw["norm_mix_g"], lw["w1"], lw["w2"], lw["conv_w"], lw["conv_b"])


def _tile_scan(a, b, rev):
    t, w = a.shape
    a = a.reshape(t // SUBLANES, SUBLANES, w)
    b = b.reshape(t // SUBLANES, SUBLANES, w)
    pos = lax.broadcasted_iota(jnp.int32, (1, SUBLANES, w), 1)
    s = 1
    while s < SUBLANES:
        shift = (SUBLANES - s) if rev else s
        valid = (pos < SUBLANES - s) if rev else (pos >= s)
        a_sh = jnp.where(valid, pltpu.roll(a, shift, 1), 1.0)
        b_sh = jnp.where(valid, pltpu.roll(b, shift, 1), 0.0)
        yield
        b = a * b_sh + b
        a = a * a_sh
        yield
        s *= 2
    return a, b


def _round_robin(tasks):
    tasks = list(tasks)
    while tasks:
        for task in list(tasks):
            gen, stages = task
            try:
                for _ in range(stages):
                    next(gen)
            except StopIteration:
                tasks.remove(task)


def _seg_cumsum(x, seg, rev):
    t, w = x.shape
    ntile = t // SUBLANES
    per_seg = seg // SUBLANES
    x = x.reshape(ntile, SUBLANES, w)
    pos = lax.broadcasted_iota(jnp.int32, (1, SUBLANES, w), 1)
    s = 1
    while s < SUBLANES:
        shift = (SUBLANES - s) if rev else s
        valid = (pos < SUBLANES - s) if rev else (pos >= s)
        x = x + jnp.where(valid, pltpu.roll(x, shift, 1), 0.0)
        s *= 2
    tiles = [None] * ntile
    for s0 in range(0, ntile, per_seg):
        carry = None
        for j in (range(s0 + per_seg - 1, s0 - 1, -1) if rev else range(s0, s0 + per_seg)):
            tiles[j] = x[j] if carry is None else x[j] + carry
            carry = tiles[j][0:1, :] if rev else tiles[j][SUBLANES - 1:SUBLANES, :]
    return jnp.stack(tiles, axis=0).reshape(t, w)


def _ssd_intra(gmat, col, rowv, xhat_bf, tri, rev):
    half = tri.shape[0]
    lo, hi = slice(0, half), slice(half, 2 * half)

    def blk(rs, cs, masked):
        e = col[rs, :] - rowv[:, cs]
        if masked:
            e = jnp.where(tri, e, NEG_BIG)
        return (gmat[rs, cs] * jnp.exp2(e)).astype(BF16)

    if rev:
        out_top = _dot(jnp.concatenate([blk(lo, lo, True), blk(lo, hi, False)], axis=1), xhat_bf)
        yield
        out_bot = _dot(blk(hi, hi, True), xhat_bf[hi, :])
    else:
        out_top = _dot(blk(lo, lo, True), xhat_bf[lo, :])
        yield
        out_bot = _dot(jnp.concatenate([blk(hi, lo, False), blk(hi, hi, True)], axis=1), xhat_bf)
    yield
    return jnp.concatenate([out_top, out_bot], axis=0)


def _ssd_state_slices(h):
    g, hl = divmod(h, SSD_HEADS // SSD_GROUPS)
    return (slice(g * SSD_STATE, (g + 1) * SSD_STATE),
            slice(hl * SSD_HEAD_DIM, (hl + 1) * SSD_HEAD_DIM))


N_DIR_PARAMS = 9
N_DIR_SCRATCH = 6


def _scan_kernel(*refs, zero_init, emit_state, n_alias, nchunks):
    it = iter(refs)
    params = [[next(it) for _ in range(N_DIR_PARAMS)] for _ in range(2)]
    dskip_ref = next(it)
    if not zero_init:
        i_lru, i_ssd, i_gla = next(it), next(it), next(it)
    for _ in range(n_alias):
        next(it)
    o_refs = [next(it), next(it)]
    if emit_state:
        s_lru, s_ssd, s_gla = next(it), next(it), next(it)
    scratch = [[next(it) for _ in range(N_DIR_SCRATCH)] for _ in range(2)]

    c = pl.program_id(1)

    @pl.when(c == 0)
    def _():
        for d in range(2):
            h_scr, ssd_scr, gla_scr = scratch[d][:3]
            if zero_init:
                h_scr[...] = jnp.zeros_like(h_scr)
                ssd_scr[...] = jnp.zeros_like(ssd_scr)
                gla_scr[...] = jnp.zeros_like(gla_scr)
            else:
                h_scr[...] = jnp.broadcast_to(i_lru[d], h_scr.shape)
                for h in range(SSD_HEADS):
                    rs, ls = _ssd_state_slices(h)
                    ssd_scr[rs, ls] = i_ssd[d, h]
                gla_scr[...] = i_gla[d]

    tasks, finals = [], []
    for d in range(2):
        t, f = _direction_tasks(d == 1, *params[d], dskip_ref, o_refs[d], *scratch[d])
        tasks.append(t)
        finals.append(f)
    _round_robin([task for pair in zip(*tasks) for task in pair])
    for f in finals:
        f()

    if emit_state:
        @pl.when(c == nchunks - 1)
        def _():
            for d in range(2):
                h_scr, ssd_scr, gla_scr = scratch[d][:3]
                s_lru[d] = h_scr[0:1, :]
                for h in range(SSD_HEADS):
                    rs, ls = _ssd_state_slices(h)
                    s_ssd[d, h] = ssd_scr[rs, ls]
                s_gla[d] = gla_scr[...]


def _direction_tasks(rev, p_ref, lruw_ref, lruba_ref, lrubx_ref, lam_ref, dtb_ref, alog_ref, w2_ref, gb_ref,
                     dskip_ref, o_ref, h_scr, ssd_scr, gla_scr, la_scr, lb_scr, lh_scr):
    tc = T_CHUNK
    small = p_ref[:, SMALL_OFF:SMALL_OFF + LANES]

    c8h = (-0.5 * LRU_C) * _softplus(-lam_ref[...])
    ntile = tc // SUBLANES

    def lru_group(g):
        sl = slice(g * MXU_DIM, (g + 1) * MXU_DIM)
        xl = p_ref[:, sl]
        pre = _dot(xl.astype(BF16), lruw_ref[g])
        yield
        t_r = jnp.tanh(pre[:, :MXU_DIM] + lruba_ref[:, sl])
        yield
        t_i = jnp.tanh(pre[:, MXU_DIM:] + lrubx_ref[:, sl])
        yield
        log_a = c8h[:, sl] * t_r + c8h[:, sl]
        xlh = 0.5 * xl
        ix = xlh * t_i + xlh
        yield
        a = jnp.exp(log_a)
        th = jnp.tanh(log_a)
        yield
        b = jnp.sqrt(-th * (1.0 + a * a)) * ix
        slabs = range(g * MXU_DIM // LANES, (g + 1) * MXU_DIM // LANES)
        for k, slab in enumerate(slabs):
            la_scr[slab] = a[:, k * LANES:(k + 1) * LANES]
            lb_scr[slab] = b[:, k * LANES:(k + 1) * LANES]
        yield
        nrun = tc // LRU_RUN
        order = range(LRU_RUN - 1, -1, -1) if rev else range(LRU_RUN)
        run_a, run_b = {}, {}
        for k, slab in enumerate(slabs):
            pa = pb = None
            for r in order:
                ar = la_scr[slab, pl.ds(r, nrun, stride=LRU_RUN), :]
                br = lb_scr[slab, pl.ds(r, nrun, stride=LRU_RUN), :]
                if pa is not None:
                    br = ar * pb + br
                    ar = ar * pa
                run_a[k, r], run_b[k, r] = pa, pb = ar, br
        yield
        tot_a = jnp.concatenate([run_a[k, order[-1]] for k in range(len(slabs))], axis=1)
        tot_b = jnp.concatenate([run_b[k, order[-1]] for k in range(len(slabs))], axis=1)
        a_cum, b_cum = yield from _tile_scan(tot_a, tot_b, rev)
        carry_in = h_scr[0:1, sl]
        carry = carry_in
        nt = nrun // SUBLANES
        hs = [None] * nt
        for ti in (range(nt - 1, -1, -1) if rev else range(nt)):
            hs[ti] = a_cum[ti] * carry + b_cum[ti]
            carry = hs[ti][0:1, :] if rev else hs[ti][SUBLANES - 1:SUBLANES, :]
        h_scr[:, sl] = jnp.broadcast_to(carry, (SUBLANES, MXU_DIM))
        yield
        h_out = jnp.stack(hs, axis=0).reshape(nrun, MXU_DIM)
        row = lax.broadcasted_iota(jnp.int32, (nrun, MXU_DIM), 0)
        if rev:
            h_in = jnp.where(row == nrun - 1, carry_in, pltpu.roll(h_out, nrun - 1, 0))
        else:
            h_in = jnp.where(row == 0, carry_in, pltpu.roll(h_out, 1, 0))
        for k, slab in enumerate(slabs):
            hk = h_in[:, k * LANES:(k + 1) * LANES]
            for r in order:
                lh_scr[slab, pl.ds(r, nrun, stride=LRU_RUN), :] = run_a[k, r] * hk + run_b[k, r]
        yield
        for k, slab in enumerate(slabs):
            o_ref[:, g * MXU_DIM + k * LANES:g * MXU_DIM + (k + 1) * LANES] = lh_scr[slab].astype(o_ref.dtype)

    lane = lax.broadcasted_iota(jnp.int32, (1, LANES), 1)
    dt_lo = SMALL_DT + (SSD_HEADS if rev else 0)
    dt_mask = (lane >= dt_lo) & (lane < dt_lo + SSD_HEADS)
    dt = jnp.where(dt_mask, _softplus_log(small + dtb_ref[...]), 0.0)
    la = dt * (-LOG2E * jnp.exp(alog_ref[...]))
    bcum = _seg_cumsum(la, tc, rev)
    bcum_t = bcum.T
    tot = bcum[0:1, :] if rev else bcum[tc - 1:tc, :]

    half = tc // 2
    ri = lax.broadcasted_iota(jnp.int32, (half, half), 0)
    ci = lax.broadcasted_iota(jnp.int32, (half, half), 1)
    tri = (ci >= ri) if rev else (ci <= ri)
    lane_t = lax.broadcasted_iota(jnp.int32, (tc, LANES), 1)
    lo_half = lane_t < SSD_HEAD_DIM
    row_s = lax.broadcasted_iota(jnp.int32, (2 * SSD_STATE, SSD_LANES), 0)
    lane_s = lax.broadcasted_iota(jnp.int32, (1, SSD_LANES), 1) // SSD_HEAD_DIM
    b_off = 2 * D_MODEL
    c_off = b_off + SSD_GROUPS * SSD_STATE
    ssd_upd = {}

    def ssd_group(g):
        pair, gl = divmod(g, 2)
        b_tile = p_ref[:, b_off + pair * LANES:b_off + (pair + 1) * LANES].astype(BF16)
        c_tile = p_ref[:, c_off + pair * LANES:c_off + (pair + 1) * LANES]
        s_pair = ssd_scr[pair * LANES:(pair + 1) * LANES, :]
        gmask = lo_half if gl == 0 else jnp.logical_not(lo_half)
        c_g = jnp.where(gmask, c_tile, 0.0).astype(BF16)
        gmat = _dot_nt(c_g, b_tile)
        inter = _dot(c_g, s_pair.astype(BF16))
        yield
        wx_parts = []
        for hp in range(2):
            h0 = g * 4 + hp * 2
            col0 = bcum[:, dt_lo + h0:dt_lo + h0 + 1]
            col1 = bcum[:, dt_lo + h0 + 1:dt_lo + h0 + 2]
            dtc0 = dt[:, dt_lo + h0:dt_lo + h0 + 1]
            dtc1 = dt[:, dt_lo + h0 + 1:dt_lo + h0 + 2]
            xsl = slice(D_MODEL + h0 * SSD_HEAD_DIM, D_MODEL + (h0 + 2) * SSD_HEAD_DIM)
            xs = p_ref[:, xsl]
            xhat = xs * jnp.where(lo_half, dtc0, dtc1)
            xhat_bf = xhat.astype(BF16)
            yield
            outs = []
            for hh, col in ((0, col0), (1, col1)):
                rowv = bcum_t[dt_lo + h0 + hh:dt_lo + h0 + hh + 1, :]
                outs.append((yield from _ssd_intra(gmat, col, rowv, xhat_bf, tri, rev)))
            o_pair = jnp.where(lo_half, outs[0], outs[1])
            colp = jnp.where(lo_half, col0, col1)
            isl = slice(hp * LANES, (hp + 1) * LANES)
            o_pair = o_pair + jnp.exp2(colp) * inter[:, isl]
            if not rev:
                o_pair = o_pair + dskip_ref[:, h0 * SSD_HEAD_DIM:(h0 + 2) * SSD_HEAD_DIM] * xs
            o_ref[:, xsl] = o_pair.astype(o_ref.dtype)
            yield
            t0 = tot[:, dt_lo + h0:dt_lo + h0 + 1]
            t1 = tot[:, dt_lo + h0 + 1:dt_lo + h0 + 2]
            totp = jnp.where(lane < SSD_HEAD_DIM, t0, t1)
            wx_parts.append((jnp.exp2(totp - colp) * xhat).astype(BF16))
        wx = jnp.concatenate(wx_parts, axis=1)
        ssd_upd[g] = _dot_tn(b_tile, wx)
        yield
        d = jnp.zeros((1, SSD_LANES), F32)
        for hl in range(4):
            th = tot[:, dt_lo + g * 4 + hl:dt_lo + g * 4 + hl + 1]
            d = jnp.where(lane_s == hl, jnp.exp2(th), d)
        ssd_upd[("dec", g)] = d
        if gl == 1:
            first = row_s < SSD_STATE
            new = jnp.where(first, ssd_upd[g - 1], ssd_upd[g])
            dec = jnp.where(first, ssd_upd[("dec", g - 1)], d)
            ssd_scr[pair * LANES:(pair + 1) * LANES, :] = dec * s_pair + new

    small_bf = small.astype(BF16)
    gla_bg = {}

    def gla_log_decay(hd):
        ksl = slice(hd * GLA_HEAD_K, (hd + 1) * GLA_HEAD_K)
        z = _dot(small_bf, w2_ref[:, ksl]) + gb_ref[:, ksl]
        yield
        ld = -_softplus_log(-z) * (LOG2E / GLA_GATE_NORM)
        yield
        gla_bg[hd] = _seg_cumsum(ld, GLA_SUB, rev)
        yield

    nsub = tc // GLA_SUB
    nblk = GLA_SUB // GLA_BLK
    scale = GLA_HEAD_K ** -0.5
    gla_st = [gla_scr[hd * GLA_HEAD_K:(hd + 1) * GLA_HEAD_K, :] for hd in range(GLA_HEADS)]

    def gla_subchunk(sc_i, hd):
        ksl = slice(hd * GLA_HEAD_K, (hd + 1) * GLA_HEAD_K)
        st = gla_st[hd]
        rsl = slice(sc_i * GLA_SUB, (sc_i + 1) * GLA_SUB)
        q = p_ref[rsl, Q_OFF + hd * GLA_HEAD_K:Q_OFF + (hd + 1) * GLA_HEAD_K] * scale
        k = p_ref[rsl, K_OFF + hd * GLA_HEAD_K:K_OFF + (hd + 1) * GLA_HEAD_K]
        v = p_ref[rsl, V_OFF + hd * GLA_HEAD_V:V_OFF + (hd + 1) * GLA_HEAD_V].astype(BF16)
        b = gla_bg[hd][rsl, :]
        inter = _dot((q * jnp.exp2(b)).astype(BF16), st.astype(BF16))
        yield
        o_rows = []
        for blk in range(nblk):
            bsl = slice(blk * GLA_BLK, (blk + 1) * GLA_BLK)
            if rev:
                keys = slice(blk * GLA_BLK, GLA_SUB)
                ref_row = b[(blk + 1) * GLA_BLK - 1:(blk + 1) * GLA_BLK, :]
            else:
                keys = slice(0, (blk + 1) * GLA_BLK)
                ref_row = b[blk * GLA_BLK:blk * GLA_BLK + 1, :]
            nk = keys.stop - keys.start
            qt = (q[bsl, :] * jnp.exp2(b[bsl, :] - ref_row)).astype(BF16)
            kt = (k[keys, :] * jnp.exp2(ref_row - b[keys, :])).astype(BF16)
            a_blk = _dot_nt(qt, kt)
            yield
            ai = lax.broadcasted_iota(jnp.int32, (GLA_BLK, nk), 0)
            aj = lax.broadcasted_iota(jnp.int32, (GLA_BLK, nk), 1)
            vis = (aj >= ai) if rev else (aj <= ai + blk * GLA_BLK)
            a_blk = jnp.where(vis, a_blk, 0.0).astype(BF16)
            o_rows.append(inter[bsl, :] + _dot(a_blk, v[keys, :]))
            yield
        o = jnp.concatenate(o_rows, axis=0)
        o_ref[rsl, 2 * D_MODEL + hd * GLA_HEAD_V:2 * D_MODEL + (hd + 1) * GLA_HEAD_V] = o.astype(o_ref.dtype)
        end = b[0:1, :] if rev else b[GLA_SUB - 1:GLA_SUB, :]
        k_dec = (k * jnp.exp2(end - b)).astype(BF16)
        dec_col = jnp.broadcast_to(jnp.exp2(end), (GLA_HEAD_K, GLA_HEAD_K)).T[:, 0:1]
        gla_st[hd] = dec_col * st + _dot_tn(k_dec, v)
        yield

    def gla_head(hd):
        yield from gla_log_decay(hd)
        for step in range(nsub):
            yield from gla_subchunk((nsub - 1 - step) if rev else step, hd)

    def lru_all():
        for g in range(D_MODEL // MXU_DIM):
            yield from lru_group(g)

    def ssd_all():
        for g in range(SSD_GROUPS):
            yield from ssd_group(g)

    tasks = [(gla_head(hd), 1) for hd in range(GLA_HEADS)] + [(ssd_all(), 2), (lru_all(), 1)]

    def finalize():
        for hd in range(GLA_HEADS):
            gla_scr[hd * GLA_HEAD_K:(hd + 1) * GLA_HEAD_K, :] = gla_st[hd]

    return tasks, finalize


def _scan(p1, tok0, nseq, nchunks, lw, l, init, o_prev, st_prev, st_shapes):
    tc = T_CHUNK
    blk0 = tok0 // tc
    m = p1.shape[0]

    def row_block(d):
        return lambda b, c: (blk0 + b * nchunks + ((nchunks - 1 - c) if d else c), 0)

    const2 = lambda b, c: (0, 0)
    const3 = lambda b, c: (0, 0, 0)
    in_specs, args = [], []
    for d in range(2):
        in_specs += [
            pl.BlockSpec((tc, P1_COLS), row_block(d)),
            pl.BlockSpec((D_MODEL // MXU_DIM, MXU_DIM, 2 * MXU_DIM), const3),
            pl.BlockSpec((1, D_MODEL), const2),
            pl.BlockSpec((1, D_MODEL), const2),
            pl.BlockSpec((1, D_MODEL), const2),
            pl.BlockSpec((1, LANES), const2),
            pl.BlockSpec((1, LANES), const2),
            pl.BlockSpec((LANES, GLA_DK), const2),
            pl.BlockSpec((1, GLA_DK), const2),
        ]
        args += [p1, lw["lru_w"][d], lw["lru_ba"][d], lw["lru_bx"][d], lw["lam"][d], lw["dt_bias"][d],
                 lw["a_log"][d], lw["gate_w2"][d], lw["gate_b"][d]]
        assert len(in_specs) == (d + 1) * N_DIR_PARAMS
    in_specs.append(pl.BlockSpec((1, D_MODEL), const2))
    args.append(lw["d_skip"])
    state_specs = [
        pl.BlockSpec((None, None, 2, 1, D_MODEL), lambda b, c: (b, l, 0, 0, 0)),
        pl.BlockSpec((None, None, 2, SSD_HEADS, SSD_STATE, SSD_HEAD_DIM), lambda b, c: (b, l, 0, 0, 0, 0)),
        pl.BlockSpec((None, None, 2, GLA_ROWS, GLA_HEAD_V), lambda b, c: (b, l, 0, 0, 0)),
    ]
    if init is not None:
        in_specs += state_specs
        args += list(init)
    aliases = {}
    prev = list(o_prev or []) + list(st_prev or [])
    first_out = 0 if o_prev is not None else 2
    for k, arr in enumerate(prev):
        aliases[len(args)] = first_out + k
        in_specs.append(pl.BlockSpec(memory_space=pl.ANY))
        args.append(arr)
    out_specs = [pl.BlockSpec((tc, 3 * D_MODEL), row_block(d)) for d in range(2)]
    out_shape = [jax.ShapeDtypeStruct((m, 3 * D_MODEL), BF16)] * 2
    if st_shapes is not None:
        out_specs += state_specs
        out_shape += [jax.ShapeDtypeStruct(s, F32) for s in st_shapes]
    kern = functools.partial(_scan_kernel, zero_init=init is None, emit_state=st_shapes is not None,
                             n_alias=len(prev), nchunks=nchunks)
    dir_scratch = [
        pltpu.VMEM((SUBLANES, D_MODEL), F32),
        pltpu.VMEM((SSD_ROWS, SSD_LANES), F32),
        pltpu.VMEM((GLA_ROWS, GLA_HEAD_V), F32),
        pltpu.VMEM((D_MODEL // LANES, tc, LANES), F32),
        pltpu.VMEM((D_MODEL // LANES, tc, LANES), F32),
        pltpu.VMEM((D_MODEL // LANES, tc, LANES), F32),
    ]
    assert len(dir_scratch) == N_DIR_SCRATCH
    return pl.pallas_call(
        kern,
        grid=(nseq, nchunks),
        in_specs=in_specs,
        out_specs=out_specs,
        out_shape=out_shape,
        input_output_aliases=aliases,
        scratch_shapes=dir_scratch * 2,
        compiler_params=_params(("parallel", "arbitrary")),
        name="scan",
    )(*args)


def _post_kernel(x_ref, of_ref, ob_ref, p2_ref, mod_ref, ssdg_ref, glag_ref,
                 wl_ref, ws_ref, wg_ref, wo_ref, o_ref):
    dm = D_MODEL

    def both(sl):
        return of_ref[:, sl].astype(F32) + ob_ref[:, sl].astype(F32)

    def p2(k):
        return p2_ref[:, k * dm:(k + 1) * dm].astype(F32)

    y_lru = both(slice(0, dm)) * _gelu_tanh(p2(0))
    y_ssd = _rms(both(slice(dm, 2 * dm)) * _silu(p2(1)), ssdg_ref[...])
    parts = []
    for hd in range(GLA_HEADS):
        sl = slice(2 * dm + hd * GLA_HEAD_V, 2 * dm + (hd + 1) * GLA_HEAD_V)
        parts.append(_rms(both(sl), glag_ref[...]))
    y_gla = jnp.concatenate(parts, axis=1) * _silu(p2(2))
    merged = (_sigmoid(p2(3)) * _dot(y_lru.astype(BF16), wl_ref[...])
              + _sigmoid(p2(4)) * _dot(y_ssd.astype(BF16), ws_ref[...])
              + _sigmoid(p2(5)) * _dot(y_gla.astype(BF16), wg_ref[...]))
    out = _dot(merged.astype(BF16), wo_ref[...])
    o_ref[...] = x_ref[...] + mod_ref[:, 2 * dm:3 * dm] * out


def _post(x, o_f, o_b, p2, mod, row_of_block, lw, l):
    m = x.shape[0]
    tm = TM_POST
    wspec = pl.BlockSpec((None, D_MODEL, D_MODEL), lambda i: (l, 0, 0), pipeline_mode=pl.Buffered(1))
    return pl.pallas_call(
        _post_kernel,
        grid=(m // tm,),
        in_specs=[
            pl.BlockSpec((tm, D_MODEL), lambda i: (i, 0)),
            pl.BlockSpec((tm, 3 * D_MODEL), lambda i: (i, 0)),
            pl.BlockSpec((tm, 3 * D_MODEL), lambda i: (i, 0)),
            pl.BlockSpec((tm, P2_COLS), lambda i: (i, 0)),
            pl.BlockSpec((None, 1, 6 * D_MODEL), lambda i: (row_of_block(i, tm), 0, 0)),
            pl.BlockSpec((1, D_MODEL), lambda i: (0, 0)),
            pl.BlockSpec((1, GLA_HEAD_V), lambda i: (0, 0)),
            wspec, wspec, wspec, wspec,
        ],
        out_specs=pl.BlockSpec((tm, D_MODEL), lambda i: (i, 0)),
        out_shape=jax.ShapeDtypeStruct((m, D_MODEL), F32),
        compiler_params=_params(("parallel",)),
        name="post",
    )(x, o_f, o_b, p2, mod, lw["ssd_norm_g"], lw["gla_norm_g"],
      lw["w_br_lru"], lw["w_br_ssd"], lw["w_br_gla"], lw["w_out"])


def _ffn_kernel(x_ref, mod_ref, g_ref, wi_ref, wo_ref, fg_ref, o_ref, *, final_norm, nsplit):
    dm = D_MODEL
    x = x_ref[...]
    h = (_rms(x, g_ref[...]) * (1.0 + mod_ref[:, 4 * dm:5 * dm]) + mod_ref[:, 3 * dm:4 * dm]).astype(BF16)
    tf = D_FF // nsplit
    acc = jnp.zeros(x.shape, F32)
    for s in range(nsplit):
        fs = slice(s * tf, (s + 1) * tf)
        gate = _dot(h, wi_ref[:, fs])
        up = _dot(h, wi_ref[:, D_FF + s * tf:D_FF + (s + 1) * tf])
        acc = acc + _dot((_silu(gate) * up).astype(BF16), wo_ref[fs, :])
    y = x + mod_ref[:, 5 * dm:6 * dm] * acc
    if final_norm:
        y = _rms(y, fg_ref[...])
    o_ref[...] = y


def _ffn(x, mod, row_of_block, lw, l, final_g, final_norm):
    m = x.shape[0]
    tm = TM_FFN
    single = dict(pipeline_mode=pl.Buffered(1))
    kern = functools.partial(_ffn_kernel, final_norm=final_norm, nsplit=2)
    return pl.pallas_call(
        kern,
        grid=(m // tm,),
        in_specs=[
            pl.BlockSpec((tm, D_MODEL), lambda i: (i, 0)),
            pl.BlockSpec((None, 1, 6 * D_MODEL), lambda i: (row_of_block(i, tm), 0, 0)),
            pl.BlockSpec((1, D_MODEL), lambda i: (0, 0)),
            pl.BlockSpec((None, D_MODEL, 2 * D_FF), lambda i: (l, 0, 0), **single),
            pl.BlockSpec((None, D_FF, D_MODEL), lambda i: (l, 0, 0), **single),
            pl.BlockSpec((1, D_MODEL), lambda i: (0, 0)),
        ],
        out_specs=pl.BlockSpec((tm, D_MODEL), lambda i: (i, 0)),
        out_shape=jax.ShapeDtypeStruct((m, D_MODEL), F32),
        compiler_params=_params(("parallel",)),
        name="ffn",
    )(x, mod, lw["norm_ffn_g"], lw["w_ffn_in"], lw["w_ffn_out"], final_g)


def _snake_rows(src_ref, dst_ref):
    rows = src_ref.shape[0]
    for g0 in range(0, rows, 2 * GRID_W):
        dst_ref[g0:g0 + GRID_W, :] = src_ref[g0:g0 + GRID_W, :]
        for r in range(GRID_W):
            dst_ref[pl.ds(g0 + GRID_W + r, 1), :] = src_ref[pl.ds(g0 + 2 * GRID_W - 1 - r, 1), :]


def _assemble_kernel(xa_ref, xb_ref, o_ref, *, nb_ctx):
    i = pl.program_id(0)

    @pl.when(i < nb_ctx)
    def _():
        o_ref[...] = xa_ref[...]

    @pl.when(i >= nb_ctx)
    def _():
        _snake_rows(xb_ref, o_ref)


def _split_kernel(y_ref, ya_ref, yb_ref, *, nb_ctx):
    i = pl.program_id(0)

    @pl.when(i < nb_ctx)
    def _():
        ya_ref[...] = y_ref[...]

    @pl.when(i >= nb_ctx)
    def _():
        _snake_rows(y_ref, yb_ref)


def _assemble_tokens(x_ctx, x_lat):
    tr = T_CHUNK
    nb_ctx, nb_lat = x_ctx.shape[0] // tr, x_lat.shape[0] // tr
    return pl.pallas_call(
        functools.partial(_assemble_kernel, nb_ctx=nb_ctx),
        grid=(nb_ctx + nb_lat,),
        in_specs=[pl.BlockSpec((tr, D_MODEL), lambda i: (jnp.minimum(i, nb_ctx - 1), 0)),
                  pl.BlockSpec((tr, D_MODEL), lambda i: (jnp.maximum(i - nb_ctx, 0), 0))],
        out_specs=pl.BlockSpec((tr, D_MODEL), lambda i: (i, 0)),
        out_shape=jax.ShapeDtypeStruct((x_ctx.shape[0] + x_lat.shape[0], D_MODEL), x_ctx.dtype),
        compiler_params=_params(("arbitrary",)),
        name="assemble_tokens",
    )(x_ctx, x_lat)


def _split_tokens(y, m_ctx):
    tr = T_CHUNK
    nb_ctx, nb_lat = m_ctx // tr, (y.shape[0] - m_ctx) // tr
    return pl.pallas_call(
        functools.partial(_split_kernel, nb_ctx=nb_ctx),
        grid=(nb_ctx + nb_lat,),
        in_specs=[pl.BlockSpec((tr, D_MODEL), lambda i: (i, 0))],
        out_specs=[pl.BlockSpec((tr, D_MODEL), lambda i: (jnp.minimum(i, nb_ctx - 1), 0)),
                   pl.BlockSpec((tr, D_MODEL), lambda i: (jnp.maximum(i - nb_ctx, 0), 0))],
        out_shape=[jax.ShapeDtypeStruct((m_ctx, D_MODEL), y.dtype),
                   jax.ShapeDtypeStruct((y.shape[0] - m_ctx, D_MODEL), y.dtype)],
        compiler_params=_params(("arbitrary",)),
        name="split_tokens",
    )(y)


_O_GATE, _O_Z, _O_XBC = D_MODEL, 2 * D_MODEL, 3 * D_MODEL
_O_DT = _O_XBC + CONV_COLS - D_MODEL
_O_Q = _O_DT + 2 * SSD_HEADS
_O_G = _O_Q + 2 * GLA_DK + D_MODEL
_O_LR = _O_G + D_MODEL
_O_M = _O_LR + 2 * GLA_GATE_RANK
IN_WIDTH = _O_M + 3 * D_MODEL
W_PREP_ROWS = 128


def _wprep_kernel(w_ref, w1_ref, w2_ref):
    def cols(lo, hi):
        return w_ref[:, lo:hi].astype(BF16)

    w1_ref[:, 0:D_MODEL] = cols(0, D_MODEL)
    w1_ref[:, D_MODEL:CONV_COLS] = cols(_O_XBC, _O_DT)
    w1_ref[:, Q_OFF:SMALL_OFF] = cols(_O_Q, _O_G)
    assert _O_DT % LANES == SMALL_DT and _O_LR % LANES == SMALL_LR
    lane = lax.broadcasted_iota(jnp.int32, (w_ref.shape[0], LANES), 1)
    dt_tile = w_ref[:, _O_DT:_O_DT + LANES]
    lr_tile = w_ref[:, _O_LR - SMALL_LR:_O_LR - SMALL_LR + LANES]
    small = jnp.where(lane < SMALL_LR, dt_tile, jnp.where(lane < 2 * SMALL_LR, lr_tile, 0.0))
    w1_ref[:, SMALL_OFF:] = small.astype(BF16)
    w2_ref[:, 0:2 * D_MODEL] = cols(_O_GATE, _O_XBC)
    w2_ref[:, 2 * D_MODEL:3 * D_MODEL] = cols(_O_G, _O_LR)
    w2_ref[:, 3 * D_MODEL:] = cols(_O_M, IN_WIDTH)


def _prep_w_in(w_in):
    n_layers = w_in.shape[0]
    tr = W_PREP_ROWS
    return pl.pallas_call(
        _wprep_kernel,
        grid=(n_layers, D_MODEL // tr),
        in_specs=[pl.BlockSpec((None, tr, IN_WIDTH), lambda l, r: (l, r, 0))],
        out_specs=[pl.BlockSpec((None, tr, P1_COLS), lambda l, r: (l, r, 0)),
                   pl.BlockSpec((None, tr, P2_COLS), lambda l, r: (l, r, 0))],
        out_shape=[jax.ShapeDtypeStruct((n_layers, D_MODEL, P1_COLS), BF16),
                   jax.ShapeDtypeStruct((n_layers, D_MODEL, P2_COLS), BF16)],
        compiler_params=_params(("parallel", "parallel")),
        name="w_in_relayout",
    )(w_in)


def _layer_weights(l, w):
    dm = D_MODEL

    def block_diag_tiles(wa, wx):
        per = MXU_DIM // LRU_BLOCK_W
        eye = jnp.eye(per, dtype=wa.dtype)
        def tiles(wb):
            wb = wb.reshape(dm // MXU_DIM, per, LRU_BLOCK_W, LRU_BLOCK_W)
            t = jnp.einsum("gpwv,pq->gpwqv", wb, eye)
            return t.reshape(dm // MXU_DIM, MXU_DIM, MXU_DIM)
        return jnp.concatenate([tiles(wa), tiles(wx)], axis=2).astype(BF16)

    def lane_pad(v, lo):
        return jnp.zeros((1, LANES), F32).at[0, lo:lo + v.shape[0]].set(v)

    def gate_w2(d):
        lo = SMALL_LR + d * GLA_GATE_RANK
        return jnp.zeros((LANES, GLA_DK), F32).at[lo:lo + GLA_GATE_RANK].set(w["gla_gate_w2"][l, d]).astype(BF16)

    return dict(
        w1=w["w1"], w2=w["w2"],
        norm_mix_g=w["norm_mix_g"][l][None], norm_ffn_g=w["norm_ffn_g"][l][None],
        conv_w=jnp.concatenate([w["lru_conv_w"][l], w["ssd_conv_w"][l]], axis=1),
        conv_b=jnp.concatenate([w["lru_conv_b"][l], w["ssd_conv_b"][l]])[None],
        lru_w=[block_diag_tiles(0.5 * w["lru_w_a"][l, d], 0.5 * w["lru_w_x"][l, d]) for d in range(2)],
        lru_ba=[0.5 * w["lru_b_a"][l, d][None] for d in range(2)],
        lru_bx=[0.5 * w["lru_b_x"][l, d][None] for d in range(2)],
        lam=[w["lru_lambda"][l, d][None] for d in range(2)],
        dt_bias=[lane_pad(w["ssd_dt_bias"][l, d], SMALL_DT + d * SSD_HEADS) for d in range(2)],
        a_log=[lane_pad(w["ssd_a_log"][l, d], SMALL_DT + d * SSD_HEADS) for d in range(2)],
        d_skip=jnp.repeat(w["ssd_d"][l], SSD_HEAD_DIM)[None],
        gate_w2=[gate_w2(d) for d in range(2)],
        gate_b=[w["gla_gate_b"][l, d][None] for d in range(2)],
        ssd_norm_g=w["ssd_norm_g"][l][None], gla_norm_g=w["gla_norm_g"][l][None],
        w_br_lru=w["w_br_lru_bf"], w_br_ssd=w["w_br_ssd_bf"], w_br_gla=w["w_br_gla_bf"], w_out=w["w_out_bf"],
        w_ffn_in=w["w_ffn_in_bf"], w_ffn_out=w["w_ffn_out_bf"],
    )


def kernel(x_prompt, x_sample, state_lru, state_ssd, state_gla, c, c_ctx, norm_mix_g, norm_ffn_g, w_ada, b_ada, w_in, lru_conv_w, lru_conv_b, lru_w_a, lru_b_a, lru_w_x, lru_b_x, lru_lambda, ssd_conv_w, ssd_conv_b, ssd_dt_bias, ssd_a_log, ssd_d, ssd_norm_g, gla_gate_w2, gla_gate_b, gla_norm_g, w_br_lru, w_br_ssd, w_br_gla, w_out, w_ffn_in, w_ffn_out, final_norm_g):
    w = dict(norm_mix_g=norm_mix_g, norm_ffn_g=norm_ffn_g, w_in=w_in, lru_conv_w=lru_conv_w,
             lru_conv_b=lru_conv_b, lru_w_a=lru_w_a, lru_b_a=lru_b_a, lru_w_x=lru_w_x, lru_b_x=lru_b_x,
             lru_lambda=lru_lambda, ssd_conv_w=ssd_conv_w, ssd_conv_b=ssd_conv_b, ssd_dt_bias=ssd_dt_bias,
             ssd_a_log=ssd_a_log, ssd_d=ssd_d, ssd_norm_g=ssd_norm_g, gla_gate_w2=gla_gate_w2,
             gla_gate_b=gla_gate_b, gla_norm_g=gla_norm_g, w_br_lru=w_br_lru, w_br_ssd=w_br_ssd,
             w_br_gla=w_br_gla, w_out=w_out, w_ffn_in=w_ffn_in, w_ffn_out=w_ffn_out)
    w["w1"], w["w2"] = _prep_w_in(w_in)
    for name in ("w_br_lru", "w_br_ssd", "w_br_gla", "w_out", "w_ffn_in", "w_ffn_out"):
        w[name + "_bf"] = w[name].astype(BF16)
    n_layers = w_in.shape[0]
    b_ctx, t_ctx, dm = x_prompt.shape
    b_lat, t_lat, _ = x_sample.shape
    m_ctx = b_ctx * t_ctx
    m_lat = b_lat * t_lat
    nch_ctx = t_ctx // T_CHUNK
    nch_lat = t_lat // T_CHUNK

    n_rows = -(-(1 + b_lat) // SUBLANES) * SUBLANES
    cond = jnp.zeros((n_rows, dm), F32).at[0].set(c_ctx).at[1:1 + b_lat].set(c)
    mod = _modulation(cond, w_ada, b_ada)

    def row_of_block(i, tm):
        tok = i * tm
        return jnp.where(tok < m_ctx, 0, 1 + (tok - m_ctx) // t_lat)

    def seq_pos(tok):
        in_ctx = tok < m_ctx
        pos = jnp.where(in_ctx, lax.rem(tok, t_ctx), lax.rem(tok - m_ctx, t_lat))
        return pos, jnp.where(in_ctx, t_ctx, t_lat)

    x = _assemble_tokens(x_prompt.reshape(m_ctx, dm), x_sample.reshape(m_lat, dm))

    st_shapes = [(b_ctx, n_layers, 2, 1, dm),
                 (b_ctx, n_layers, 2, SSD_HEADS, SSD_STATE, SSD_HEAD_DIM),
                 (b_ctx, n_layers, 2, GLA_ROWS, GLA_HEAD_V)]
    init = (state_lru.reshape(b_lat, n_layers, 2, 1, dm), state_ssd,
            state_gla.reshape(b_lat, n_layers, 2, GLA_ROWS, GLA_HEAD_V))
    states = None
    for l in range(n_layers):
        lw = _layer_weights(l, w)
        mod_l = mod[l][:, None, :]
        p1, p2 = _inproj(x, mod_l, row_of_block, seq_pos, lw, l)
        o_f, o_b, *states = _scan(p1, 0, b_ctx, nch_ctx, lw, l, None, None, states, st_shapes)
        outs = _scan(p1, m_ctx, b_lat, nch_lat, lw, l, init, [o_f, o_b], None, None)
        x = _post(x, outs[0], outs[1], p2, mod_l, row_of_block, lw, l)
        x = _ffn(x, mod_l, row_of_block, lw, l, final_norm_g[None], l == n_layers - 1)

    y_ctx, y_lat = _split_tokens(x, m_ctx)
    y_prompt = y_ctx.reshape(b_ctx, t_ctx, dm)
    y_sample = y_lat.reshape(b_lat, t_lat, dm)
    return (y_prompt, y_sample, states[0].reshape(b_ctx, n_layers, 2, dm), states[1],
            states[2].reshape(b_ctx, n_layers, 2, GLA_HEADS, GLA_HEAD_K, GLA_HEAD_V))
```

```python
import functools

import jax
import jax.numpy as jnp
from jax import lax
from jax.experimental import pallas as pl
from jax.experimental.pallas import tpu as pltpu

F32 = jnp.float32
BF16 = jnp.bfloat16

D_MODEL = 1024
GRID_W = 64
CONV_W = 4
CONV_LEFT = 2
LRU_BLOCK_W = 64
LRU_C = 8.0
SSD_HEAD_DIM = 64
SSD_HEADS = 16
SSD_GROUPS = 4
SSD_STATE = 64
GLA_HEADS = 4
GLA_HEAD_K = 128
GLA_HEAD_V = 256
GLA_DK = GLA_HEADS * GLA_HEAD_K
GLA_GATE_RANK = 16
GLA_GATE_NORM = 16.0
D_FF = 2816
EPS = 1e-6

LANES = 128
SUBLANES = 8
MXU_DIM = 256
VMEM_LIMIT_BYTES = 60 * 1024 * 1024

CONV_COLS = D_MODEL + D_MODEL + 2 * SSD_GROUPS * SSD_STATE
Q_OFF = CONV_COLS
K_OFF = Q_OFF + GLA_DK
V_OFF = K_OFF + GLA_DK
SMALL_OFF = V_OFF + D_MODEL
P1_COLS = SMALL_OFF + LANES
P2_COLS = 6 * D_MODEL
SMALL_DT = 0
SMALL_LR = 2 * SSD_HEADS

T_CHUNK = 256
IN_CHUNK = 512
TM_POST = 512
TM_FFN = 512
TOKEN_ROWS = 1024
LRU_RUN = 4
GLA_SUB = 64
GLA_BLK = 16
NEG_BIG = -1e30
LOG2E = 1.4426950408889634
SSD_ROWS = SSD_GROUPS * SSD_STATE
SSD_LANES = (SSD_HEADS // SSD_GROUPS) * SSD_HEAD_DIM
GLA_ROWS = GLA_HEADS * GLA_HEAD_K


def _params(sem, **kw):
    return pltpu.CompilerParams(dimension_semantics=sem, vmem_limit_bytes=VMEM_LIMIT_BYTES, **kw)


def _softplus(x):
    return jnp.maximum(x, 0.0) + jnp.log1p(jnp.exp(-jnp.abs(x)))


def _softplus_log(x):
    return jnp.maximum(x, 0.0) + jnp.log(1.0 + jnp.exp(-jnp.abs(x)))


def _sigmoid(x):
    return 0.5 * (1.0 + jnp.tanh(0.5 * x))


def _silu(x):
    return x * _sigmoid(x)


def _gelu_tanh(x):
    c = 0.7978845608028654
    return 0.5 * x * (1.0 + jnp.tanh(c * (x + 0.044715 * (x * x * x))))


def _rms(x, g):
    return x * lax.rsqrt(jnp.mean(x * x, axis=-1, keepdims=True) + EPS) * g


def _dot(a, b):
    return jnp.dot(a, b, preferred_element_type=F32)


def _dot_nt(a, b):
    return lax.dot_general(a, b, (((1,), (1,)), ((), ())), preferred_element_type=F32)


def _dot_tn(a, b):
    return lax.dot_general(a, b, (((0,), (0,)), ((), ())), preferred_element_type=F32)


def _mod_kernel(c_ref, w_ref, b_ref, o_ref):
    c = _silu(c_ref[...]).astype(BF16)
    o_ref[...] = _dot(c, w_ref[...].astype(BF16)) + b_ref[...]


def _modulation(cond, w_ada, b_ada):
    n_layers = w_ada.shape[0]
    rows = cond.shape[0]
    tn = D_MODEL
    return pl.pallas_call(
        _mod_kernel,
        grid=(n_layers, 6 * D_MODEL // tn),
        in_specs=[
            pl.BlockSpec((rows, D_MODEL), lambda l, j: (0, 0)),
            pl.BlockSpec((None, D_MODEL, tn), lambda l, j: (l, 0, j)),
            pl.BlockSpec((None, 1, tn), lambda l, j: (l, 0, j)),
        ],
        out_specs=pl.BlockSpec((None, rows, tn), lambda l, j: (l, 0, j)),
        out_shape=jax.ShapeDtypeStruct((n_layers, rows, 6 * D_MODEL), F32),
        compiler_params=_params(("parallel", "parallel")),
        name="modulation",
    )(cond, w_ada, b_ada.reshape(n_layers, 1, 6 * D_MODEL))


def _inproj_kernel(x_ref, xp_ref, xn_ref, mod_ref, g_ref, w1_ref, w2_ref, cw_ref, cb_ref,
                   p1_ref, p2_ref, hext, cbuf, obuf, *, seq_pos):
    tc = T_CHUNK
    sh = mod_ref[:, 0:D_MODEL]
    sc1 = 1.0 + mod_ref[:, D_MODEL:2 * D_MODEL]
    g = g_ref[...]
    h_main = _rms(x_ref[...], g) * sc1 + sh
    hext[SUBLANES:SUBLANES + tc, :] = h_main
    pos, seq_len = seq_pos(pl.program_id(0) * tc)
    head = jnp.where(pos == 0, 0.0, 1.0)
    tail = jnp.where(pos + tc == seq_len, 0.0, 1.0)
    hext[0:SUBLANES, :] = (_rms(xp_ref[...], g) * sc1 + sh) * head
    hext[SUBLANES + tc:, :] = (_rms(xn_ref[...], g) * sc1 + sh) * tail
    he = hext[...].astype(BF16)
    hb = h_main.astype(BF16)

    def conv_columns():
        half = tc // 2
        for c0 in range(0, CONV_COLS, IN_CHUNK):
            res = _dot(he, w1_ref[:, c0:c0 + IN_CHUNK])
            for k in range(IN_CHUNK // LANES):
                cbuf[c0 // LANES + k] = res[:, k * LANES:(k + 1) * LANES]
            yield
            for k in range(IN_CHUNK // LANES):
                slab = c0 // LANES + k
                ls = slice(slab * LANES, (slab + 1) * LANES)
                for parity in range(2):
                    xc = cb_ref[:, ls]
                    for j in range(CONV_W):
                        row0 = SUBLANES - CONV_LEFT + j + parity
                        xc = xc + cw_ref[j:j + 1, ls] * cbuf[slab, pl.ds(row0, half, stride=2), :]
                    obuf[slab, pl.ds(parity, half, stride=2), :] = xc if c0 < D_MODEL else _silu(xc)
                if k % 2 == 1:
                    yield
            for k in range(IN_CHUNK // LANES):
                slab = c0 // LANES + k
                p1_ref[:, slab * LANES:(slab + 1) * LANES] = obuf[slab]
            yield

    def plain_columns():
        for c0 in range(CONV_COLS, P1_COLS, IN_CHUNK):
            cs = slice(c0, min(c0 + IN_CHUNK, P1_COLS))
            p1_ref[:, cs] = _dot(hb, w1_ref[:, cs])
            yield
        for c0 in range(0, P2_COLS, IN_CHUNK):
            cs = slice(c0, c0 + IN_CHUNK)
            p2_ref[:, cs] = _dot(hb, w2_ref[:, cs]).astype(p2_ref.dtype)
            yield

    _round_robin([(conv_columns(), 1), (plain_columns(), 1)])


def _inproj(x, mod, row_of_block, seq_pos, lw, l):
    m = x.shape[0]
    tc = T_CHUNK
    rows8 = tc // SUBLANES
    last8 = m // SUBLANES - 1
    single = dict(pipeline_mode=pl.Buffered(1))
    return pl.pallas_call(
        functools.partial(_inproj_kernel, seq_pos=seq_pos),
        grid=(m // tc,),
        in_specs=[
            pl.BlockSpec((tc, D_MODEL), lambda i: (i, 0)),
            pl.BlockSpec((SUBLANES, D_MODEL), lambda i: (jnp.maximum(i * rows8 - 1, 0), 0)),
            pl.BlockSpec((SUBLANES, D_MODEL), lambda i: (jnp.minimum((i + 1) * rows8, last8), 0)),
            pl.BlockSpec((None, 1, 6 * D_MODEL), lambda i: (row_of_block(i, tc), 0, 0)),
            pl.BlockSpec((1, D_MODEL), lambda i: (0, 0)),
            pl.BlockSpec((None, D_MODEL, P1_COLS), lambda i: (l, 0, 0), **single),
            pl.BlockSpec((None, D_MODEL, P2_COLS), lambda i: (l, 0, 0), **single),
            pl.BlockSpec((CONV_W, CONV_COLS), lambda i: (0, 0)),
            pl.BlockSpec((1, CONV_COLS), lambda i: (0, 0)),
        ],
        out_specs=[pl.BlockSpec((tc, P1_COLS), lambda i: (i, 0)),
                   pl.BlockSpec((tc, P2_COLS), lambda i: (i, 0))],
        out_shape=[jax.ShapeDtypeStruct((m, P1_COLS), F32), jax.ShapeDtypeStruct((m, P2_COLS), BF16)],
        scratch_shapes=[pltpu.VMEM((tc + 2 * SUBLANES, D_MODEL), F32),
                        pltpu.VMEM((CONV_COLS // LANES, tc + 2 * SUBLANES, LANES), F32),
                        pltpu.VMEM((CONV_COLS // LANES, tc, LANES), F32)],
        compiler_params=_params(("parallel",)),
        name="inproj",
    )(x, x, x, mod, lw["norm_mix_g"], lw["w1"], lw["w2"], lw["conv_w"], lw["conv_b"])


def _tile_scan(a, b, rev):
    t, w = a.shape
    a = a.reshape(t // SUBLANES, SUBLANES, w)
    b = b.reshape(t // SUBLANES, SUBLANES, w)
    pos = lax.broadcasted_iota(jnp.int32, (1, SUBLANES, w), 1)
    s = 1
    while s < SUBLANES:
        shift = (SUBLANES - s) if rev else s
        valid = (pos < SUBLANES - s) if rev else (pos >= s)
        a_sh = jnp.where(valid, pltpu.roll(a, shift, 1), 1.0)
        b_sh = jnp.where(valid, pltpu.roll(b, shift, 1), 0.0)
        yield
        b = a * b_sh + b
        a = a * a_sh
        yield
        s *= 2
    return a, b


def _delayed(gen, turns):
    for _ in range(turns):
        yield
    yield from gen


def _round_robin(tasks):
    tasks = list(tasks)
    while tasks:
        for task in list(tasks):
            gen, stages = task
            try:
                for _ in range(stages):
                    next(gen)
            except StopIteration:
                tasks.remove(task)


def _seg_cumsum(x, seg, rev):
    t, w = x.shape
    ntile = t // SUBLANES
    per_seg = seg // SUBLANES
    x = x.reshape(ntile, SUBLANES, w)
    pos = lax.broadcasted_iota(jnp.int32, (1, SUBLANES, w), 1)
    s = 1
    while s < SUBLANES:
        shift = (SUBLANES - s) if rev else s
        valid = (pos < SUBLANES - s) if rev else (pos >= s)
        x = x + jnp.where(valid, pltpu.roll(x, shift, 1), 0.0)
        s *= 2
    tiles = [None] * ntile
    for s0 in range(0, ntile, per_seg):
        carry = None
        for j in (range(s0 + per_seg - 1, s0 - 1, -1) if rev else range(s0, s0 + per_seg)):
            tiles[j] = x[j] if carry is None else x[j] + carry
            carry = tiles[j][0:1, :] if rev else tiles[j][SUBLANES - 1:SUBLANES, :]
    return jnp.stack(tiles, axis=0).reshape(t, w)


def _ssd_intra(gmat, col, rowv, xhat_bf, tri, rev):
    half = tri.shape[0]
    lo, hi = slice(0, half), slice(half, 2 * half)

    def blk(rs, cs, masked):
        e = col[rs, :] - rowv[:, cs]
        if masked:
            e = jnp.where(tri, e, NEG_BIG)
        return (gmat[rs, cs] * jnp.exp2(e)).astype(BF16)

    if rev:
        out_top = _dot(jnp.concatenate([blk(lo, lo, True), blk(lo, hi, False)], axis=1), xhat_bf)
        yield
        out_bot = _dot(blk(hi, hi, True), xhat_bf[hi, :])
    else:
        out_top = _dot(blk(lo, lo, True), xhat_bf[lo, :])
        yield
        out_bot = _dot(jnp.concatenate([blk(hi, lo, False), blk(hi, hi, True)], axis=1), xhat_bf)
    yield
    return jnp.concatenate([out_top, out_bot], axis=0)


def _ssd_state_slices(h):
    g, hl = divmod(h, SSD_HEADS // SSD_GROUPS)
    return (slice(g * SSD_STATE, (g + 1) * SSD_STATE),
            slice(hl * SSD_HEAD_DIM, (hl + 1) * SSD_HEAD_DIM))


N_DIR_PARAMS = 9
N_DIR_SCRATCH = 6
SCAN_STAGGER = 6


def _scan_kernel(*refs, zero_init, emit_state, n_alias, nchunks):
    it = iter(refs)
    params = [[next(it) for _ in range(N_DIR_PARAMS)] for _ in range(2)]
    dskip_ref = next(it)
    if not zero_init:
        i_lru, i_ssd, i_gla = next(it), next(it), next(it)
    for _ in range(n_alias):
        next(it)
    o_refs = [next(it), next(it)]
    if emit_state:
        s_lru, s_ssd, s_gla = next(it), next(it), next(it)
    scratch = [[next(it) for _ in range(N_DIR_SCRATCH)] for _ in range(2)]

    c = pl.program_id(1)

    @pl.when(c == 0)
    def _():
        for d in range(2):
            h_scr, ssd_scr, gla_scr = scratch[d][:3]
            if zero_init:
                h_scr[...] = jnp.zeros_like(h_scr)
                ssd_scr[...] = jnp.zeros_like(ssd_scr)
                gla_scr[...] = jnp.zeros_like(gla_scr)
            else:
                h_scr[...] = jnp.broadcast_to(i_lru[d], h_scr.shape)
                for h in range(SSD_HEADS):
                    rs, ls = _ssd_state_slices(h)
                    ssd_scr[rs, ls] = i_ssd[d, h]
                gla_scr[...] = i_gla[d]

    tasks, finals = [], []
    for d in range(2):
        t, f = _direction_tasks(d == 1, *params[d], dskip_ref, o_refs[d], *scratch[d])
        tasks.append(t)
        finals.append(f)
    tasks[1] = [(_delayed(gen, SCAN_STAGGER), n) for gen, n in tasks[1]]
    _round_robin([task for pair in zip(*tasks) for task in pair])
    for f in finals:
        f()

    if emit_state:
        @pl.when(c == nchunks - 1)
        def _():
            for d in range(2):
                h_scr, ssd_scr, gla_scr = scratch[d][:3]
                s_lru[d] = h_scr[0:1, :]
                for h in range(SSD_HEADS):
                    rs, ls = _ssd_state_slices(h)
                    s_ssd[d, h] = ssd_scr[rs, ls]
                s_gla[d] = gla_scr[...]


def _direction_tasks(rev, p_ref, lruw_ref, lruba_ref, lrubx_ref, lam_ref, dtb_ref, alog_ref, w2_ref, gb_ref,
                     dskip_ref, o_ref, h_scr, ssd_scr, gla_scr, la_scr, lb_scr, lh_scr):
    tc = T_CHUNK
    small = p_ref[:, SMALL_OFF:SMALL_OFF + LANES]

    c8h = (-0.5 * LRU_C) * _softplus(-lam_ref[...])
    ntile = tc // SUBLANES

    def lru_group(g):
        sl = slice(g * MXU_DIM, (g + 1) * MXU_DIM)
        xl = p_ref[:, sl]
        pre = _dot(xl.astype(BF16), lruw_ref[g])
        yield
        t_r = jnp.tanh(pre[:, :MXU_DIM] + lruba_ref[:, sl])
        yield
        t_i = jnp.tanh(pre[:, MXU_DIM:] + lrubx_ref[:, sl])
        yield
        log_a = c8h[:, sl] * t_r + c8h[:, sl]
        xlh = 0.5 * xl
        ix = xlh * t_i + xlh
        yield
        a = jnp.exp(log_a)
        th = jnp.tanh(log_a)
        yield
        b = jnp.sqrt(-th * (1.0 + a * a)) * ix
        slabs = range(g * MXU_DIM // LANES, (g + 1) * MXU_DIM // LANES)
        for k, slab in enumerate(slabs):
            la_scr[slab] = a[:, k * LANES:(k + 1) * LANES]
            lb_scr[slab] = b[:, k * LANES:(k + 1) * LANES]
        yield
        nrun = tc // LRU_RUN
        order = range(LRU_RUN - 1, -1, -1) if rev else range(LRU_RUN)
        run_a, run_b = {}, {}
        for k, slab in enumerate(slabs):
            pa = pb = None
            for r in order:
                ar = la_scr[slab, pl.ds(r, nrun, stride=LRU_RUN), :]
                br = lb_scr[slab, pl.ds(r, nrun, stride=LRU_RUN), :]
                if pa is not None:
                    br = ar * pb + br
                    ar = ar * pa
                run_a[k, r], run_b[k, r] = pa, pb = ar, br
        yield
        tot_a = jnp.concatenate([run_a[k, order[-1]] for k in range(len(slabs))], axis=1)
        tot_b = jnp.concatenate([run_b[k, order[-1]] for k in range(len(slabs))], axis=1)
        a_cum, b_cum = yield from _tile_scan(tot_a, tot_b, rev)
        carry_in = h_scr[0:1, sl]
        carry = carry_in
        nt = nrun // SUBLANES
        hs = [None] * nt
        for ti in (range(nt - 1, -1, -1) if rev else range(nt)):
            hs[ti] = a_cum[ti] * carry + b_cum[ti]
            carry = hs[ti][0:1, :] if rev else hs[ti][SUBLANES - 1:SUBLANES, :]
        h_scr[:, sl] = jnp.broadcast_to(carry, (SUBLANES, MXU_DIM))
        yield
        h_out = jnp.stack(hs, axis=0).reshape(nrun, MXU_DIM)
        row = lax.broadcasted_iota(jnp.int32, (nrun, MXU_DIM), 0)
        if rev:
            h_in = jnp.where(row == nrun - 1, carry_in, pltpu.roll(h_out, nrun - 1, 0))
        else:
            h_in = jnp.where(row == 0, carry_in, pltpu.roll(h_out, 1, 0))
        for k, slab in enumerate(slabs):
            hk = h_in[:, k * LANES:(k + 1) * LANES]
            for r in order:
                lh_scr[slab, pl.ds(r, nrun, stride=LRU_RUN), :] = run_a[k, r] * hk + run_b[k, r]
        yield
        for k, slab in enumerate(slabs):
            o_ref[:, g * MXU_DIM + k * LANES:g * MXU_DIM + (k + 1) * LANES] = lh_scr[slab].astype(o_ref.dtype)

    lane = lax.broadcasted_iota(jnp.int32, (1, LANES), 1)
    dt_lo = SMALL_DT + (SSD_HEADS if rev else 0)
    dt_mask = (lane >= dt_lo) & (lane < dt_lo + SSD_HEADS)
    dt = jnp.where(dt_mask, _softplus_log(small + dtb_ref[...]), 0.0)
    la = dt * (-LOG2E * jnp.exp(alog_ref[...]))
    bcum = _seg_cumsum(la, tc, rev)
    bcum_t = bcum.T
    tot = bcum[0:1, :] if rev else bcum[tc - 1:tc, :]

    half = tc // 2
    ri = lax.broadcasted_iota(jnp.int32, (half, half), 0)
    ci = lax.broadcasted_iota(jnp.int32, (half, half), 1)
    tri = (ci >= ri) if rev else (ci <= ri)
    lane_t = lax.broadcasted_iota(jnp.int32, (tc, LANES), 1)
    lo_half = lane_t < SSD_HEAD_DIM
    row_s = lax.broadcasted_iota(jnp.int32, (2 * SSD_STATE, SSD_LANES), 0)
    lane_s = lax.broadcasted_iota(jnp.int32, (1, SSD_LANES), 1) // SSD_HEAD_DIM
    b_off = 2 * D_MODEL
    c_off = b_off + SSD_GROUPS * SSD_STATE
    ssd_upd = {}

    def ssd_group(g):
        pair, gl = divmod(g, 2)
        b_tile = p_ref[:, b_off + pair * LANES:b_off + (pair + 1) * LANES].astype(BF16)
        c_tile = p_ref[:, c_off + pair * LANES:c_off + (pair + 1) * LANES]
        s_pair = ssd_scr[pair * LANES:(pair + 1) * LANES, :]
        gmask = lo_half if gl == 0 else jnp.logical_not(lo_half)
        c_g = jnp.where(gmask, c_tile, 0.0).astype(BF16)
        gmat = _dot_nt(c_g, b_tile)
        inter = _dot(c_g, s_pair.astype(BF16))
        yield
        wx_parts = []
        for hp in range(2):
            h0 = g * 4 + hp * 2
            col0 = bcum[:, dt_lo + h0:dt_lo + h0 + 1]
            col1 = bcum[:, dt_lo + h0 + 1:dt_lo + h0 + 2]
            dtc0 = dt[:, dt_lo + h0:dt_lo + h0 + 1]
            dtc1 = dt[:, dt_lo + h0 + 1:dt_lo + h0 + 2]
            xsl = slice(D_MODEL + h0 * SSD_HEAD_DIM, D_MODEL + (h0 + 2) * SSD_HEAD_DIM)
            xs = p_ref[:, xsl]
            xhat = xs * jnp.where(lo_half, dtc0, dtc1)
            xhat_bf = xhat.astype(BF16)
            yield
            outs = []
            for hh, col in ((0, col0), (1, col1)):
                rowv = bcum_t[dt_lo + h0 + hh:dt_lo + h0 + hh + 1, :]
                outs.append((yield from _ssd_intra(gmat, col, rowv, xhat_bf, tri, rev)))
            o_pair = jnp.where(lo_half, outs[0], outs[1])
            colp = jnp.where(lo_half, col0, col1)
            isl = slice(hp * LANES, (hp + 1) * LANES)
            o_pair = o_pair + jnp.exp2(colp) * inter[:, isl]
            if not rev:
                o_pair = o_pair + dskip_ref[:, h0 * SSD_HEAD_DIM:(h0 + 2) * SSD_HEAD_DIM] * xs
            o_ref[:, xsl] = o_pair.astype(o_ref.dtype)
            yield
            t0 = tot[:, dt_lo + h0:dt_lo + h0 + 1]
            t1 = tot[:, dt_lo + h0 + 1:dt_lo + h0 + 2]
            totp = jnp.where(lane < SSD_HEAD_DIM, t0, t1)
            wx_parts.append((jnp.exp2(totp - colp) * xhat).astype(BF16))
        wx = jnp.concatenate(wx_parts, axis=1)
        ssd_upd[g] = _dot_tn(b_tile, wx)
        yield
        d = jnp.zeros((1, SSD_LANES), F32)
        for hl in range(4):
            th = tot[:, dt_lo + g * 4 + hl:dt_lo + g * 4 + hl + 1]
            d = jnp.where(lane_s == hl, jnp.exp2(th), d)
        ssd_upd[("dec", g)] = d
        if gl == 1:
            first = row_s < SSD_STATE
            new = jnp.where(first, ssd_upd[g - 1], ssd_upd[g])
            dec = jnp.where(first, ssd_upd[("dec", g - 1)], d)
            ssd_scr[pair * LANES:(pair + 1) * LANES, :] = dec * s_pair + new

    small_bf = small.astype(BF16)
    gla_bg = {}

    def gla_log_decay(hd):
        ksl = slice(hd * GLA_HEAD_K, (hd + 1) * GLA_HEAD_K)
        z = _dot(small_bf, w2_ref[:, ksl]) + gb_ref[:, ksl]
        yield
        ld = -_softplus_log(-z) * (LOG2E / GLA_GATE_NORM)
        yield
        gla_bg[hd] = _seg_cumsum(ld, GLA_SUB, rev)
        yield

    nsub = tc // GLA_SUB
    nblk = GLA_SUB // GLA_BLK
    scale = GLA_HEAD_K ** -0.5
    gla_st = [gla_scr[hd * GLA_HEAD_K:(hd + 1) * GLA_HEAD_K, :] for hd in range(GLA_HEADS)]

    def gla_subchunk(sc_i, hd):
        ksl = slice(hd * GLA_HEAD_K, (hd + 1) * GLA_HEAD_K)
        st = gla_st[hd]
        rsl = slice(sc_i * GLA_SUB, (sc_i + 1) * GLA_SUB)
        q = p_ref[rsl, Q_OFF + hd * GLA_HEAD_K:Q_OFF + (hd + 1) * GLA_HEAD_K] * scale
        k = p_ref[rsl, K_OFF + hd * GLA_HEAD_K:K_OFF + (hd + 1) * GLA_HEAD_K]
        v = p_ref[rsl, V_OFF + hd * GLA_HEAD_V:V_OFF + (hd + 1) * GLA_HEAD_V].astype(BF16)
        b = gla_bg[hd][rsl, :]
        inter = _dot((q * jnp.exp2(b)).astype(BF16), st.astype(BF16))
        yield
        o_rows = []
        for blk in range(nblk):
            bsl = slice(blk * GLA_BLK, (blk + 1) * GLA_BLK)
            if rev:
                keys = slice(blk * GLA_BLK, GLA_SUB)
                ref_row = b[(blk + 1) * GLA_BLK - 1:(blk + 1) * GLA_BLK, :]
            else:
                keys = slice(0, (blk + 1) * GLA_BLK)
                ref_row = b[blk * GLA_BLK:blk * GLA_BLK + 1, :]
            nk = keys.stop - keys.start
            qt = (q[bsl, :] * jnp.exp2(b[bsl, :] - ref_row)).astype(BF16)
            kt = (k[keys, :] * jnp.exp2(ref_row - b[keys, :])).astype(BF16)
            a_blk = _dot_nt(qt, kt)
            yield
            ai = lax.broadcasted_iota(jnp.int32, (GLA_BLK, nk), 0)
            aj = lax.broadcasted_iota(jnp.int32, (GLA_BLK, nk), 1)
            vis = (aj >= ai) if rev else (aj <= ai + blk * GLA_BLK)
            a_blk = jnp.where(vis, a_blk, 0.0).astype(BF16)
            o_rows.append(inter[bsl, :] + _dot(a_blk, v[keys, :]))
            yield
        o = jnp.concatenate(o_rows, axis=0)
        o_ref[rsl, 2 * D_MODEL + hd * GLA_HEAD_V:2 * D_MODEL + (hd + 1) * GLA_HEAD_V] = o.astype(o_ref.dtype)
        end = b[0:1, :] if rev else b[GLA_SUB - 1:GLA_SUB, :]
        k_dec = (k * jnp.exp2(end - b)).astype(BF16)
        dec_col = jnp.broadcast_to(jnp.exp2(end), (GLA_HEAD_K, GLA_HEAD_K)).T[:, 0:1]
        gla_st[hd] = dec_col * st + _dot_tn(k_dec, v)
        yield

    def gla_head(hd):
        yield from gla_log_decay(hd)
        for step in range(nsub):
            yield from gla_subchunk((nsub - 1 - step) if rev else step, hd)

    def lru_all():
        for g in range(D_MODEL // MXU_DIM):
            yield from lru_group(g)

    def ssd_all():
        for g in range(SSD_GROUPS):
            yield from ssd_group(g)

    tasks = [(gla_head(hd), 1) for hd in range(GLA_HEADS)] + [(ssd_all(), 2), (lru_all(), 1)]

    def finalize():
        for hd in range(GLA_HEADS):
            gla_scr[hd * GLA_HEAD_K:(hd + 1) * GLA_HEAD_K, :] = gla_st[hd]

    return tasks, finalize


def _scan(p1, tok0, nseq, nchunks, lw, l, init, o_prev, st_prev, st_shapes):
    tc = T_CHUNK
    blk0 = tok0 // tc
    m = p1.shape[0]

    def row_block(d):
        return lambda b, c: (blk0 + b * nchunks + ((nchunks - 1 - c) if d else c), 0)

    const2 = lambda b, c: (0, 0)
    const3 = lambda b, c: (0, 0, 0)
    in_specs, args = [], []
    for d in range(2):
        in_specs += [
            pl.BlockSpec((tc, P1_COLS), row_block(d)),
            pl.BlockSpec((D_MODEL // MXU_DIM, MXU_DIM, 2 * MXU_DIM), const3),
            pl.BlockSpec((1, D_MODEL), const2),
            pl.BlockSpec((1, D_MODEL), const2),
            pl.BlockSpec((1, D_MODEL), const2),
            pl.BlockSpec((1, LANES), const2),
            pl.BlockSpec((1, LANES), const2),
            pl.BlockSpec((LANES, GLA_DK), const2),
            pl.BlockSpec((1, GLA_DK), const2),
        ]
        args += [p1, lw["lru_w"][d], lw["lru_ba"][d], lw["lru_bx"][d], lw["lam"][d], lw["dt_bias"][d],
                 lw["a_log"][d], lw["gate_w2"][d], lw["gate_b"][d]]
        assert len(in_specs) == (d + 1) * N_DIR_PARAMS
    in_specs.append(pl.BlockSpec((1, D_MODEL), const2))
    args.append(lw["d_skip"])
    state_specs = [
        pl.BlockSpec((None, None, 2, 1, D_MODEL), lambda b, c: (b, l, 0, 0, 0)),
        pl.BlockSpec((None, None, 2, SSD_HEADS, SSD_STATE, SSD_HEAD_DIM), lambda b, c: (b, l, 0, 0, 0, 0)),
        pl.BlockSpec((None, None, 2, GLA_ROWS, GLA_HEAD_V), lambda b, c: (b, l, 0, 0, 0)),
    ]
    if init is not None:
        in_specs += state_specs
        args += list(init)
    aliases = {}
    prev = list(o_prev or []) + list(st_prev or [])
    first_out = 0 if o_prev is not None else 2
    for k, arr in enumerate(prev):
        aliases[len(args)] = first_out + k
        in_specs.append(pl.BlockSpec(memory_space=pl.ANY))
        args.append(arr)
    out_specs = [pl.BlockSpec((tc, 3 * D_MODEL), row_block(d)) for d in range(2)]
    out_shape = [jax.ShapeDtypeStruct((m, 3 * D_MODEL), BF16)] * 2
    if st_shapes is not None:
        out_specs += state_specs
        out_shape += [jax.ShapeDtypeStruct(s, F32) for s in st_shapes]
    kern = functools.partial(_scan_kernel, zero_init=init is None, emit_state=st_shapes is not None,
                             n_alias=len(prev), nchunks=nchunks)
    dir_scratch = [
        pltpu.VMEM((SUBLANES, D_MODEL), F32),
        pltpu.VMEM((SSD_ROWS, SSD_LANES), F32),
        pltpu.VMEM((GLA_ROWS, GLA_HEAD_V), F32),
        pltpu.VMEM((D_MODEL // LANES, tc, LANES), F32),
        pltpu.VMEM((D_MODEL // LANES, tc, LANES), F32),
        pltpu.VMEM((D_MODEL // LANES, tc, LANES), F32),
    ]
    assert len(dir_scratch) == N_DIR_SCRATCH
    return pl.pallas_call(
        kern,
        grid=(nseq, nchunks),
        in_specs=in_specs,
        out_specs=out_specs,
        out_shape=out_shape,
        input_output_aliases=aliases,
        scratch_shapes=dir_scratch * 2,
        compiler_params=_params(("parallel", "arbitrary")),
        name="scan",
    )(*args)


def _post_kernel(x_ref, of_ref, ob_ref, p2_ref, mod_ref, ssdg_ref, glag_ref,
                 wl_ref, ws_ref, wg_ref, wo_ref, o_ref):
    dm = D_MODEL

    def both(sl):
        return of_ref[:, sl].astype(F32) + ob_ref[:, sl].astype(F32)

    def p2(k):
        return p2_ref[:, k * dm:(k + 1) * dm].astype(F32)

    y_lru = both(slice(0, dm)) * _gelu_tanh(p2(0))
    y_ssd = _rms(both(slice(dm, 2 * dm)) * _silu(p2(1)), ssdg_ref[...])
    parts = []
    for hd in range(GLA_HEADS):
        sl = slice(2 * dm + hd * GLA_HEAD_V, 2 * dm + (hd + 1) * GLA_HEAD_V)
        parts.append(_rms(both(sl), glag_ref[...]))
    y_gla = jnp.concatenate(parts, axis=1) * _silu(p2(2))
    merged = (_sigmoid(p2(3)) * _dot(y_lru.astype(BF16), wl_ref[...])
              + _sigmoid(p2(4)) * _dot(y_ssd.astype(BF16), ws_ref[...])
              + _sigmoid(p2(5)) * _dot(y_gla.astype(BF16), wg_ref[...]))
    out = _dot(merged.astype(BF16), wo_ref[...])
    o_ref[...] = x_ref[...] + mod_ref[:, 2 * dm:3 * dm] * out


def _post(x, o_f, o_b, p2, mod, row_of_block, lw, l):
    m = x.shape[0]
    tm = TM_POST
    wspec = pl.BlockSpec((None, D_MODEL, D_MODEL), lambda i: (l, 0, 0), pipeline_mode=pl.Buffered(1))
    return pl.pallas_call(
        _post_kernel,
        grid=(m // tm,),
        in_specs=[
            pl.BlockSpec((tm, D_MODEL), lambda i: (i, 0)),
            pl.BlockSpec((tm, 3 * D_MODEL), lambda i: (i, 0)),
            pl.BlockSpec((tm, 3 * D_MODEL), lambda i: (i, 0)),
            pl.BlockSpec((tm, P2_COLS), lambda i: (i, 0)),
            pl.BlockSpec((None, 1, 6 * D_MODEL), lambda i: (row_of_block(i, tm), 0, 0)),
            pl.BlockSpec((1, D_MODEL), lambda i: (0, 0)),
            pl.BlockSpec((1, GLA_HEAD_V), lambda i: (0, 0)),
            wspec, wspec, wspec, wspec,
        ],
        out_specs=pl.BlockSpec((tm, D_MODEL), lambda i: (i, 0)),
        out_shape=jax.ShapeDtypeStruct((m, D_MODEL), F32),
        compiler_params=_params(("parallel",)),
        name="post",
    )(x, o_f, o_b, p2, mod, lw["ssd_norm_g"], lw["gla_norm_g"],
      lw["w_br_lru"], lw["w_br_ssd"], lw["w_br_gla"], lw["w_out"])


def _ffn_kernel(x_ref, mod_ref, g_ref, wi_ref, wo_ref, fg_ref, o_ref, *, final_norm, nsplit):
    dm = D_MODEL
    x = x_ref[...]
    h = (_rms(x, g_ref[...]) * (1.0 + mod_ref[:, 4 * dm:5 * dm]) + mod_ref[:, 3 * dm:4 * dm]).astype(BF16)
    tf = D_FF // nsplit
    acc = jnp.zeros(x.shape, F32)
    for s in range(nsplit):
        fs = slice(s * tf, (s + 1) * tf)
        gate = _dot(h, wi_ref[:, fs])
        up = _dot(h, wi_ref[:, D_FF + s * tf:D_FF + (s + 1) * tf])
        acc = acc + _dot((_silu(gate) * up).astype(BF16), wo_ref[fs, :])
    y = x + mod_ref[:, 5 * dm:6 * dm] * acc
    if final_norm:
        y = _rms(y, fg_ref[...])
    o_ref[...] = y


def _ffn(x, mod, row_of_block, lw, l, final_g, final_norm):
    m = x.shape[0]
    tm = TM_FFN
    single = dict(pipeline_mode=pl.Buffered(1))
    kern = functools.partial(_ffn_kernel, final_norm=final_norm, nsplit=2)
    return pl.pallas_call(
        kern,
        grid=(m // tm,),
        in_specs=[
            pl.BlockSpec((tm, D_MODEL), lambda i: (i, 0)),
            pl.BlockSpec((None, 1, 6 * D_MODEL), lambda i: (row_of_block(i, tm), 0, 0)),
            pl.BlockSpec((1, D_MODEL), lambda i: (0, 0)),
            pl.BlockSpec((None, D_MODEL, 2 * D_FF), lambda i: (l, 0, 0), **single),
            pl.BlockSpec((None, D_FF, D_MODEL), lambda i: (l, 0, 0), **single),
            pl.BlockSpec((1, D_MODEL), lambda i: (0, 0)),
        ],
        out_specs=pl.BlockSpec((tm, D_MODEL), lambda i: (i, 0)),
        out_shape=jax.ShapeDtypeStruct((m, D_MODEL), F32),
        compiler_params=_params(("parallel",)),
        name="ffn",
    )(x, mod, lw["norm_ffn_g"], lw["w_ffn_in"], lw["w_ffn_out"], final_g)


def _snake_rows(src_ref, dst_ref):
    rows = src_ref.shape[0]
    for g0 in range(0, rows, 2 * GRID_W):
        dst_ref[g0:g0 + GRID_W, :] = src_ref[g0:g0 + GRID_W, :]
        for r in range(GRID_W):
            dst_ref[pl.ds(g0 + GRID_W + r, 1), :] = src_ref[pl.ds(g0 + 2 * GRID_W - 1 - r, 1), :]


def _assemble_kernel(xa_ref, xb_ref, o_ref, *, nb_ctx):
    i = pl.program_id(0)

    @pl.when(i < nb_ctx)
    def _():
        o_ref[...] = xa_ref[...]

    @pl.when(i >= nb_ctx)
    def _():
        _snake_rows(xb_ref, o_ref)


def _split_kernel(y_ref, ya_ref, yb_ref, *, nb_ctx):
    i = pl.program_id(0)

    @pl.when(i < nb_ctx)
    def _():
        ya_ref[...] = y_ref[...]

    @pl.when(i >= nb_ctx)
    def _():
        _snake_rows(y_ref, yb_ref)


def _assemble_tokens(x_ctx, x_lat):
    tr = TOKEN_ROWS
    nb_ctx, nb_lat = x_ctx.shape[0] // tr, x_lat.shape[0] // tr
    return pl.pallas_call(
        functools.partial(_assemble_kernel, nb_ctx=nb_ctx),
        grid=(nb_ctx + nb_lat,),
        in_specs=[pl.BlockSpec((tr, D_MODEL), lambda i: (jnp.minimum(i, nb_ctx - 1), 0)),
                  pl.BlockSpec((tr, D_MODEL), lambda i: (jnp.maximum(i - nb_ctx, 0), 0))],
        out_specs=pl.BlockSpec((tr, D_MODEL), lambda i: (i, 0)),
        out_shape=jax.ShapeDtypeStruct((x_ctx.shape[0] + x_lat.shape[0], D_MODEL), x_ctx.dtype),
        compiler_params=_params(("arbitrary",)),
        name="assemble_tokens",
    )(x_ctx, x_lat)


def _split_tokens(y, m_ctx):
    tr = TOKEN_ROWS
    nb_ctx, nb_lat = m_ctx // tr, (y.shape[0] - m_ctx) // tr
    return pl.pallas_call(
        functools.partial(_split_kernel, nb_ctx=nb_ctx),
        grid=(nb_ctx + nb_lat,),
        in_specs=[pl.BlockSpec((tr, D_MODEL), lambda i: (i, 0))],
        out_specs=[pl.BlockSpec((tr, D_MODEL), lambda i: (jnp.minimum(i, nb_ctx - 1), 0)),
                   pl.BlockSpec((tr, D_MODEL), lambda i: (jnp.maximum(i - nb_ctx, 0), 0))],
        out_shape=[jax.ShapeDtypeStruct((m_ctx, D_MODEL), y.dtype),
                   jax.ShapeDtypeStruct((y.shape[0] - m_ctx, D_MODEL), y.dtype)],
        compiler_params=_params(("arbitrary",)),
        name="split_tokens",
    )(y)


_O_GATE, _O_Z, _O_XBC = D_MODEL, 2 * D_MODEL, 3 * D_MODEL
_O_DT = _O_XBC + CONV_COLS - D_MODEL
_O_Q = _O_DT + 2 * SSD_HEADS
_O_G = _O_Q + 2 * GLA_DK + D_MODEL
_O_LR = _O_G + D_MODEL
_O_M = _O_LR + 2 * GLA_GATE_RANK
IN_WIDTH = _O_M + 3 * D_MODEL
W_PREP_ROWS = 128


def _wprep_kernel(w_ref, w1_ref, w2_ref):
    def cols(lo, hi):
        return w_ref[:, lo:hi].astype(BF16)

    w1_ref[:, 0:D_MODEL] = cols(0, D_MODEL)
    w1_ref[:, D_MODEL:CONV_COLS] = cols(_O_XBC, _O_DT)
    w1_ref[:, Q_OFF:SMALL_OFF] = cols(_O_Q, _O_G)
    assert _O_DT % LANES == SMALL_DT and _O_LR % LANES == SMALL_LR
    lane = lax.broadcasted_iota(jnp.int32, (w_ref.shape[0], LANES), 1)
    dt_tile = w_ref[:, _O_DT:_O_DT + LANES]
    lr_tile = w_ref[:, _O_LR - SMALL_LR:_O_LR - SMALL_LR + LANES]
    small = jnp.where(lane < SMALL_LR, dt_tile, jnp.where(lane < 2 * SMALL_LR, lr_tile, 0.0))
    w1_ref[:, SMALL_OFF:] = small.astype(BF16)
    w2_ref[:, 0:2 * D_MODEL] = cols(_O_GATE, _O_XBC)
    w2_ref[:, 2 * D_MODEL:3 * D_MODEL] = cols(_O_G, _O_LR)
    w2_ref[:, 3 * D_MODEL:] = cols(_O_M, IN_WIDTH)


def _prep_w_in(w_in):
    n_layers = w_in.shape[0]
    tr = W_PREP_ROWS
    return pl.pallas_call(
        _wprep_kernel,
        grid=(n_layers, D_MODEL // tr),
        in_specs=[pl.BlockSpec((None, tr, IN_WIDTH), lambda l, r: (l, r, 0))],
        out_specs=[pl.BlockSpec((None, tr, P1_COLS), lambda l, r: (l, r, 0)),
                   pl.BlockSpec((None, tr, P2_COLS), lambda l, r: (l, r, 0))],
        out_shape=[jax.ShapeDtypeStruct((n_layers, D_MODEL, P1_COLS), BF16),
                   jax.ShapeDtypeStruct((n_layers, D_MODEL, P2_COLS), BF16)],
        compiler_params=_params(("parallel", "parallel")),
        name="w_in_relayout",
    )(w_in)


def _layer_weights(l, w):
    dm = D_MODEL

    def block_diag_tiles(wa, wx):
        per = MXU_DIM // LRU_BLOCK_W
        eye = jnp.eye(per, dtype=wa.dtype)
        def tiles(wb):
            wb = wb.reshape(dm // MXU_DIM, per, LRU_BLOCK_W, LRU_BLOCK_W)
            t = jnp.einsum("gpwv,pq->gpwqv", wb, eye)
            return t.reshape(dm // MXU_DIM, MXU_DIM, MXU_DIM)
        return jnp.concatenate([tiles(wa), tiles(wx)], axis=2).astype(BF16)

    def lane_pad(v, lo):
        return jnp.zeros((1, LANES), F32).at[0, lo:lo + v.shape[0]].set(v)

    def gate_w2(d):
        lo = SMALL_LR + d * GLA_GATE_RANK
        return jnp.zeros((LANES, GLA_DK), F32).at[lo:lo + GLA_GATE_RANK].set(w["gla_gate_w2"][l, d]).astype(BF16)

    return dict(
        w1=w["w1"], w2=w["w2"],
        norm_mix_g=w["norm_mix_g"][l][None], norm_ffn_g=w["norm_ffn_g"][l][None],
        conv_w=jnp.concatenate([w["lru_conv_w"][l], w["ssd_conv_w"][l]], axis=1),
        conv_b=jnp.concatenate([w["lru_conv_b"][l], w["ssd_conv_b"][l]])[None],
        lru_w=[block_diag_tiles(0.5 * w["lru_w_a"][l, d], 0.5 * w["lru_w_x"][l, d]) for d in range(2)],
        lru_ba=[0.5 * w["lru_b_a"][l, d][None] for d in range(2)],
        lru_bx=[0.5 * w["lru_b_x"][l, d][None] for d in range(2)],
        lam=[w["lru_lambda"][l, d][None] for d in range(2)],
        dt_bias=[lane_pad(w["ssd_dt_bias"][l, d], SMALL_DT + d * SSD_HEADS) for d in range(2)],
        a_log=[lane_pad(w["ssd_a_log"][l, d], SMALL_DT + d * SSD_HEADS) for d in range(2)],
        d_skip=jnp.repeat(w["ssd_d"][l], SSD_HEAD_DIM)[None],
        gate_w2=[gate_w2(d) for d in range(2)],
        gate_b=[w["gla_gate_b"][l, d][None] for d in range(2)],
        ssd_norm_g=w["ssd_norm_g"][l][None], gla_norm_g=w["gla_norm_g"][l][None],
        w_br_lru=w["w_br_lru_bf"], w_br_ssd=w["w_br_ssd_bf"], w_br_gla=w["w_br_gla_bf"], w_out=w["w_out_bf"],
        w_ffn_in=w["w_ffn_in_bf"], w_ffn_out=w["w_ffn_out_bf"],
    )


def kernel(x_prompt, x_sample, state_lru, state_ssd, state_gla, c, c_ctx, norm_mix_g, norm_ffn_g, w_ada, b_ada, w_in, lru_conv_w, lru_conv_b, lru_w_a, lru_b_a, lru_w_x, lru_b_x, lru_lambda, ssd_conv_w, ssd_conv_b, ssd_dt_bias, ssd_a_log, ssd_d, ssd_norm_g, gla_gate_w2, gla_gate_b, gla_norm_g, w_br_lru, w_br_ssd, w_br_gla, w_out, w_ffn_in, w_ffn_out, final_norm_g):
    w = dict(norm_mix_g=norm_mix_g, norm_ffn_g=norm_ffn_g, w_in=w_in, lru_conv_w=lru_conv_w,
             lru_conv_b=lru_conv_b, lru_w_a=lru_w_a, lru_b_a=lru_b_a, lru_w_x=lru_w_x, lru_b_x=lru_b_x,
             lru_lambda=lru_lambda, ssd_conv_w=ssd_conv_w, ssd_conv_b=ssd_conv_b, ssd_dt_bias=ssd_dt_bias,
             ssd_a_log=ssd_a_log, ssd_d=ssd_d, ssd_norm_g=ssd_norm_g, gla_gate_w2=gla_gate_w2,
             gla_gate_b=gla_gate_b, gla_norm_g=gla_norm_g, w_br_lru=w_br_lru, w_br_ssd=w_br_ssd,
             w_br_gla=w_br_gla, w_out=w_out, w_ffn_in=w_ffn_in, w_ffn_out=w_ffn_out)
    w["w1"], w["w2"] = _prep_w_in(w_in)
    for name in ("w_br_lru", "w_br_ssd", "w_br_gla", "w_out", "w_ffn_in", "w_ffn_out"):
        w[name + "_bf"] = w[name].astype(BF16)
    n_layers = w_in.shape[0]
    b_ctx, t_ctx, dm = x_prompt.shape
    b_lat, t_lat, _ = x_sample.shape
    m_ctx = b_ctx * t_ctx
    m_lat = b_lat * t_lat
    nch_ctx = t_ctx // T_CHUNK
    nch_lat = t_lat // T_CHUNK

    n_rows = -(-(1 + b_lat) // SUBLANES) * SUBLANES
    cond = jnp.zeros((n_rows, dm), F32).at[0].set(c_ctx).at[1:1 + b_lat].set(c)
    mod = _modulation(cond, w_ada, b_ada)

    def row_of_block(i, tm):
        tok = i * tm
        return jnp.where(tok < m_ctx, 0, 1 + (tok - m_ctx) // t_lat)

    def seq_pos(tok):
        in_ctx = tok < m_ctx
        pos = jnp.where(in_ctx, lax.rem(tok, t_ctx), lax.rem(tok - m_ctx, t_lat))
        return pos, jnp.where(in_ctx, t_ctx, t_lat)

    x = _assemble_tokens(x_prompt.reshape(m_ctx, dm), x_sample.reshape(m_lat, dm))

    st_shapes = [(b_ctx, n_layers, 2, 1, dm),
                 (b_ctx, n_layers, 2, SSD_HEADS, SSD_STATE, SSD_HEAD_DIM),
                 (b_ctx, n_layers, 2, GLA_ROWS, GLA_HEAD_V)]
    init = (state_lru.reshape(b_lat, n_layers, 2, 1, dm), state_ssd,
            state_gla.reshape(b_lat, n_layers, 2, GLA_ROWS, GLA_HEAD_V))
    states = None
    for l in range(n_layers):
        lw = _layer_weights(l, w)
        mod_l = mod[l][:, None, :]
        p1, p2 = _inproj(x, mod_l, row_of_block, seq_pos, lw, l)
        o_f, o_b, *states = _scan(p1, 0, b_ctx, nch_ctx, lw, l, None, None, states, st_shapes)
        outs = _scan(p1, m_ctx, b_lat, nch_lat, lw, l, init, [o_f, o_b], None, None)
        x = _post(x, outs[0], outs[1], p2, mod_l, row_of_block, lw, l)
        x = _ffn(x, mod_l, row_of_block, lw, l, final_norm_g[None], l == n_layers - 1)

    y_ctx, y_lat = _split_tokens(x, m_ctx)
    y_prompt = y_ctx.reshape(b_ctx, t_ctx, dm)
    y_sample = y_lat.reshape(b_lat, t_lat, dm)
    return (y_prompt, y_sample, states[0].reshape(b_ctx, n_layers, 2, dm), states[1],
            states[2].reshape(b_ctx, n_layers, 2, GLA_HEADS, GLA_HEAD_K, GLA_HEAD_V))
```

```python
import functools

import jax
import jax.numpy as jnp
from jax import lax
from jax.experimental import pallas as pl
from jax.experimental.pallas import tpu as pltpu

F32 = jnp.float32
BF16 = jnp.bfloat16

D_MODEL = 1024
GRID_W = 64
CONV_W = 4
CONV_LEFT = 2
LRU_BLOCK_W = 64
LRU_C = 8.0
SSD_HEAD_DIM = 64
SSD_HEADS = 16
SSD_GROUPS = 4
SSD_STATE = 64
GLA_HEADS = 4
GLA_HEAD_K = 128
GLA_HEAD_V = 256
GLA_DK = GLA_HEADS * GLA_HEAD_K
GLA_GATE_RANK = 16
GLA_GATE_NORM = 16.0
D_FF = 2816
EPS = 1e-6

LANES = 128
SUBLANES = 8
MXU_DIM = 256
VMEM_LIMIT_BYTES = 60 * 1024 * 1024

CONV_COLS = D_MODEL + D_MODEL + 2 * SSD_GROUPS * SSD_STATE
Q_OFF = CONV_COLS
K_OFF = Q_OFF + GLA_DK
V_OFF = K_OFF + GLA_DK
SMALL_OFF = V_OFF + D_MODEL
P1_COLS = SMALL_OFF + LANES
P2_COLS = 6 * D_MODEL
SMALL_DT = 0
SMALL_LR = 2 * SSD_HEADS

T_CHUNK = 256
IN_CHUNK = 512
TM_POST = 512
TM_FFN = 512
TOKEN_ROWS = 1024
LRU_RUN = 4
GLA_SUB = 64
GLA_BLK = 16
NEG_BIG = -1e30
LOG2E = 1.4426950408889634
SSD_ROWS = SSD_GROUPS * SSD_STATE
SSD_LANES = (SSD_HEADS // SSD_GROUPS) * SSD_HEAD_DIM
GLA_ROWS = GLA_HEADS * GLA_HEAD_K


def _params(sem, **kw):
    return pltpu.CompilerParams(dimension_semantics=sem, vmem_limit_bytes=VMEM_LIMIT_BYTES, **kw)


def _softplus(x):
    return jnp.maximum(x, 0.0) + jnp.log1p(jnp.exp(-jnp.abs(x)))


def _softplus_log(x):
    return jnp.maximum(x, 0.0) + jnp.log(1.0 + jnp.exp(-jnp.abs(x)))


def _sigmoid(x):
    return 0.5 * (1.0 + jnp.tanh(0.5 * x))


def _silu(x):
    return x * _sigmoid(x)


def _gelu_tanh(x):
    c = 0.7978845608028654
    return 0.5 * x * (1.0 + jnp.tanh(c * (x + 0.044715 * (x * x * x))))


def _rms(x, g):
    return x * lax.rsqrt(jnp.mean(x * x, axis=-1, keepdims=True) + EPS) * g


def _dot(a, b):
    return jnp.dot(a, b, preferred_element_type=F32)


def _dot_nt(a, b):
    return lax.dot_general(a, b, (((1,), (1,)), ((), ())), preferred_element_type=F32)


def _dot_tn(a, b):
    return lax.dot_general(a, b, (((0,), (0,)), ((), ())), preferred_element_type=F32)


def _mod_kernel(c_ref, w_ref, b_ref, o_ref):
    c = _silu(c_ref[...]).astype(BF16)
    o_ref[...] = _dot(c, w_ref[...].astype(BF16)) + b_ref[...]


def _modulation(cond, w_ada, b_ada):
    n_layers = w_ada.shape[0]
    rows = cond.shape[0]
    tn = D_MODEL
    return pl.pallas_call(
        _mod_kernel,
        grid=(n_layers, 6 * D_MODEL // tn),
        in_specs=[
            pl.BlockSpec((rows, D_MODEL), lambda l, j: (0, 0)),
            pl.BlockSpec((None, D_MODEL, tn), lambda l, j: (l, 0, j)),
            pl.BlockSpec((None, 1, tn), lambda l, j: (l, 0, j)),
        ],
        out_specs=pl.BlockSpec((None, rows, tn), lambda l, j: (l, 0, j)),
        out_shape=jax.ShapeDtypeStruct((n_layers, rows, 6 * D_MODEL), F32),
        compiler_params=_params(("parallel", "parallel")),
        name="modulation",
    )(cond, w_ada, b_ada.reshape(n_layers, 1, 6 * D_MODEL))


def _inproj_kernel(x_ref, xp_ref, xn_ref, mod_ref, g_ref, w1_ref, w2_ref, cw_ref, cb_ref,
                   p1_ref, p2_ref, hext, cbuf, obuf, *, seq_pos):
    tc = T_CHUNK
    sh = mod_ref[:, 0:D_MODEL]
    sc1 = 1.0 + mod_ref[:, D_MODEL:2 * D_MODEL]
    g = g_ref[...]
    h_main = _rms(x_ref[...], g) * sc1 + sh
    hext[SUBLANES:SUBLANES + tc, :] = h_main
    pos, seq_len = seq_pos(pl.program_id(0) * tc)
    head = jnp.where(pos == 0, 0.0, 1.0)
    tail = jnp.where(pos + tc == seq_len, 0.0, 1.0)
    hext[0:SUBLANES, :] = (_rms(xp_ref[...], g) * sc1 + sh) * head
    hext[SUBLANES + tc:, :] = (_rms(xn_ref[...], g) * sc1 + sh) * tail
    he = hext[...].astype(BF16)
    hb = h_main.astype(BF16)

    def conv_columns():
        half = tc // 2
        for c0 in range(0, CONV_COLS, IN_CHUNK):
            res = _dot(he, w1_ref[:, c0:c0 + IN_CHUNK])
            for k in range(IN_CHUNK // LANES):
                cbuf[c0 // LANES + k] = res[:, k * LANES:(k + 1) * LANES]
            yield
            for k in range(IN_CHUNK // LANES):
                slab = c0 // LANES + k
                ls = slice(slab * LANES, (slab + 1) * LANES)
                for parity in range(2):
                    xc = cb_ref[:, ls]
                    for j in range(CONV_W):
                        row0 = SUBLANES - CONV_LEFT + j + parity
                        xc = xc + cw_ref[j:j + 1, ls] * cbuf[slab, pl.ds(row0, half, stride=2), :]
                    obuf[slab, pl.ds(parity, half, stride=2), :] = xc if c0 < D_MODEL else _silu(xc)
                if k % 2 == 1:
                    yield
            for k in range(IN_CHUNK // LANES):
                slab = c0 // LANES + k
                p1_ref[:, slab * LANES:(slab + 1) * LANES] = obuf[slab]
            yield

    def plain_columns():
        for c0 in range(CONV_COLS, P1_COLS, IN_CHUNK):
            cs = slice(c0, min(c0 + IN_CHUNK, P1_COLS))
            p1_ref[:, cs] = _dot(hb, w1_ref[:, cs])
            yield
        for c0 in range(0, P2_COLS, IN_CHUNK):
            cs = slice(c0, c0 + IN_CHUNK)
            p2_ref[:, cs] = _dot(hb, w2_ref[:, cs]).astype(p2_ref.dtype)
            yield

    _round_robin([(conv_columns(), 1), (plain_columns(), 1)])


def _inproj(x, mod, row_of_block, seq_pos, lw, l):
    m = x.shape[0]
    tc = T_CHUNK
    rows8 = tc // SUBLANES
    last8 = m // SUBLANES - 1
    single = dict(pipeline_mode=pl.Buffered(1))
    return pl.pallas_call(
        functools.partial(_inproj_kernel, seq_pos=seq_pos),
        grid=(m // tc,),
        in_specs=[
            pl.BlockSpec((tc, D_MODEL), lambda i: (i, 0)),
            pl.BlockSpec((SUBLANES, D_MODEL), lambda i: (jnp.maximum(i * rows8 - 1, 0), 0)),
            pl.BlockSpec((SUBLANES, D_MODEL), lambda i: (jnp.minimum((i + 1) * rows8, last8), 0)),
            pl.BlockSpec((None, 1, 6 * D_MODEL), lambda i: (row_of_block(i, tc), 0, 0)),
            pl.BlockSpec((1, D_MODEL), lambda i: (0, 0)),
            pl.BlockSpec((None, D_MODEL, P1_COLS), lambda i: (l, 0, 0), **single),
            pl.BlockSpec((None, D_MODEL, P2_COLS), lambda i: (l, 0, 0), **single),
            pl.BlockSpec((CONV_W, CONV_COLS), lambda i: (0, 0)),
            pl.BlockSpec((1, CONV_COLS), lambda i: (0, 0)),
        ],
        out_specs=[pl.BlockSpec((tc, P1_COLS), lambda i: (i, 0)),
                   pl.BlockSpec((tc, P2_COLS), lambda i: (i, 0))],
        out_shape=[jax.ShapeDtypeStruct((m, P1_COLS), F32), jax.ShapeDtypeStruct((m, P2_COLS), BF16)],
        scratch_shapes=[pltpu.VMEM((tc + 2 * SUBLANES, D_MODEL), F32),
                        pltpu.VMEM((CONV_COLS // LANES, tc + 2 * SUBLANES, LANES), F32),
                        pltpu.VMEM((CONV_COLS // LANES, tc, LANES), F32)],
        compiler_params=_params(("parallel",)),
        name="inproj",
    )(x, x, x, mod, lw["norm_mix_g"], lw["w1"], lw["w2"], lw["conv_w"], lw["conv_b"])


def _tile_scan(a, b, rev):
    t, w = a.shape
    a = a.reshape(t // SUBLANES, SUBLANES, w)
    b = b.reshape(t // SUBLANES, SUBLANES, w)
    pos = lax.broadcasted_iota(jnp.int32, (1, SUBLANES, w), 1)
    s = 1
    while s < SUBLANES:
        shift = (SUBLANES - s) if rev else s
        valid = (pos < SUBLANES - s) if rev else (pos >= s)
        a_sh = jnp.where(valid, pltpu.roll(a, shift, 1), 1.0)
        b_sh = jnp.where(valid, pltpu.roll(b, shift, 1), 0.0)
        yield
        b = a * b_sh + b
        a = a * a_sh
        yield
        s *= 2
    return a, b


def _delayed(gen, turns):
    for _ in range(turns):
        yield
    yield from gen


def _round_robin(tasks):
    tasks = list(tasks)
    while tasks:
        for task in list(tasks):
            gen, stages = task
            try:
                for _ in range(stages):
                    next(gen)
            except StopIteration:
                tasks.remove(task)


def _seg_cumsum(x, seg, rev):
    t, w = x.shape
    ntile = t // SUBLANES
    per_seg = seg // SUBLANES
    x = x.reshape(ntile, SUBLANES, w)
    pos = lax.broadcasted_iota(jnp.int32, (1, SUBLANES, w), 1)
    s = 1
    while s < SUBLANES:
        shift = (SUBLANES - s) if rev else s
        valid = (pos < SUBLANES - s) if rev else (pos >= s)
        x = x + jnp.where(valid, pltpu.roll(x, shift, 1), 0.0)
        s *= 2
    tiles = [None] * ntile
    for s0 in range(0, ntile, per_seg):
        carry = None
        for j in (range(s0 + per_seg - 1, s0 - 1, -1) if rev else range(s0, s0 + per_seg)):
            tiles[j] = x[j] if carry is None else x[j] + carry
            carry = tiles[j][0:1, :] if rev else tiles[j][SUBLANES - 1:SUBLANES, :]
    return jnp.stack(tiles, axis=0).reshape(t, w)


def _ssd_intra(gmat, col, rowv, xhat_bf, tri, rev):
    half = tri.shape[0]
    lo, hi = slice(0, half), slice(half, 2 * half)

    def blk(rs, cs, masked):
        e = col[rs, :] - rowv[:, cs]
        if masked:
            e = jnp.where(tri, e, NEG_BIG)
        return (gmat[rs, cs] * jnp.exp2(e)).astype(BF16)

    if rev:
        out_top = _dot(jnp.concatenate([blk(lo, lo, True), blk(lo, hi, False)], axis=1), xhat_bf)
        yield
        out_bot = _dot(blk(hi, hi, True), xhat_bf[hi, :])
    else:
        out_top = _dot(blk(lo, lo, True), xhat_bf[lo, :])
        yield
        out_bot = _dot(jnp.concatenate([blk(hi, lo, False), blk(hi, hi, True)], axis=1), xhat_bf)
    yield
    return jnp.concatenate([out_top, out_bot], axis=0)


def _ssd_state_slices(h):
    g, hl = divmod(h, SSD_HEADS // SSD_GROUPS)
    return (slice(g * SSD_STATE, (g + 1) * SSD_STATE),
            slice(hl * SSD_HEAD_DIM, (hl + 1) * SSD_HEAD_DIM))


N_DIR_PARAMS = 9
N_DIR_SCRATCH = 6
SCAN_STAGGER = 6


def _scan_kernel(*refs, zero_init, emit_state, n_alias, nchunks):
    it = iter(refs)
    params = [[next(it) for _ in range(N_DIR_PARAMS)] for _ in range(2)]
    dskip_ref = next(it)
    if not zero_init:
        i_lru, i_ssd, i_gla = next(it), next(it), next(it)
    for _ in range(n_alias):
        next(it)
    o_refs = [next(it), next(it)]
    if emit_state:
        s_lru, s_ssd, s_gla = next(it), next(it), next(it)
    scratch = [[next(it) for _ in range(N_DIR_SCRATCH)] for _ in range(2)]

    c = pl.program_id(1)

    @pl.when(c == 0)
    def _():
        for d in range(2):
            h_scr, ssd_scr, gla_scr = scratch[d][:3]
            if zero_init:
                h_scr[...] = jnp.zeros_like(h_scr)
                ssd_scr[...] = jnp.zeros_like(ssd_scr)
                gla_scr[...] = jnp.zeros_like(gla_scr)
            else:
                h_scr[...] = jnp.broadcast_to(i_lru[d], h_scr.shape)
                for h in range(SSD_HEADS):
                    rs, ls = _ssd_state_slices(h)
                    ssd_scr[rs, ls] = i_ssd[d, h]
                gla_scr[...] = i_gla[d]

    tasks, finals = [], []
    for d in range(2):
        t, f = _direction_tasks(d == 1, *params[d], dskip_ref, o_refs[d], *scratch[d])
        tasks.append(t)
        finals.append(f)
    tasks[1] = [(_delayed(gen, SCAN_STAGGER), n) for gen, n in tasks[1]]
    _round_robin([task for pair in zip(*tasks) for task in pair])
    for f in finals:
        f()

    if emit_state:
        @pl.when(c == nchunks - 1)
        def _():
            for d in range(2):
                h_scr, ssd_scr, gla_scr = scratch[d][:3]
                s_lru[d] = h_scr[0:1, :]
                for h in range(SSD_HEADS):
                    rs, ls = _ssd_state_slices(h)
                    s_ssd[d, h] = ssd_scr[rs, ls]
                s_gla[d] = gla_scr[...]


def _direction_tasks(rev, p_ref, lruw_ref, lruba_ref, lrubx_ref, lam_ref, dtb_ref, alog_ref, w2_ref, gb_ref,
                     dskip_ref, o_ref, h_scr, ssd_scr, gla_scr, la_scr, lb_scr, lh_scr):
    tc = T_CHUNK
    small = p_ref[:, SMALL_OFF:SMALL_OFF + LANES]

    c8h = (-0.5 * LRU_C) * _softplus(-lam_ref[...])
    ntile = tc // SUBLANES

    def lru_group(g):
        sl = slice(g * MXU_DIM, (g + 1) * MXU_DIM)
        xl = p_ref[:, sl]
        pre = _dot(xl.astype(BF16), lruw_ref[g])
        yield
        t_r = jnp.tanh(pre[:, :MXU_DIM] + lruba_ref[:, sl])
        yield
        t_i = jnp.tanh(pre[:, MXU_DIM:] + lrubx_ref[:, sl])
        yield
        log_a = c8h[:, sl] * t_r + c8h[:, sl]
        xlh = 0.5 * xl
        ix = xlh * t_i + xlh
        yield
        a = jnp.exp(log_a)
        th = jnp.tanh(log_a)
        yield
        b = jnp.sqrt(-th * (1.0 + a * a)) * ix
        slabs = range(g * MXU_DIM // LANES, (g + 1) * MXU_DIM // LANES)
        for k, slab in enumerate(slabs):
            la_scr[slab] = a[:, k * LANES:(k + 1) * LANES]
            lb_scr[slab] = b[:, k * LANES:(k + 1) * LANES]
        yield
        nrun = tc // LRU_RUN
        order = range(LRU_RUN - 1, -1, -1) if rev else range(LRU_RUN)
        run_a, run_b = {}, {}
        for k, slab in enumerate(slabs):
            pa = pb = None
            for r in order:
                ar = la_scr[slab, pl.ds(r, nrun, stride=LRU_RUN), :]
                br = lb_scr[slab, pl.ds(r, nrun, stride=LRU_RUN), :]
                if pa is not None:
                    br = ar * pb + br
                    ar = ar * pa
                run_a[k, r], run_b[k, r] = pa, pb = ar, br
        yield
        tot_a = jnp.concatenate([run_a[k, order[-1]] for k in range(len(slabs))], axis=1)
        tot_b = jnp.concatenate([run_b[k, order[-1]] for k in range(len(slabs))], axis=1)
        a_cum, b_cum = yield from _tile_scan(tot_a, tot_b, rev)
        carry_in = h_scr[0:1, sl]
        carry = carry_in
        nt = nrun // SUBLANES
        hs = [None] * nt
        for ti in (range(nt - 1, -1, -1) if rev else range(nt)):
            hs[ti] = a_cum[ti] * carry + b_cum[ti]
            carry = hs[ti][0:1, :] if rev else hs[ti][SUBLANES - 1:SUBLANES, :]
        h_scr[:, sl] = jnp.broadcast_to(carry, (SUBLANES, MXU_DIM))
        yield
        h_out = jnp.stack(hs, axis=0).reshape(nrun, MXU_DIM)
        row = lax.broadcasted_iota(jnp.int32, (nrun, MXU_DIM), 0)
        if rev:
            h_in = jnp.where(row == nrun - 1, carry_in, pltpu.roll(h_out, nrun - 1, 0))
        else:
            h_in = jnp.where(row == 0, carry_in, pltpu.roll(h_out, 1, 0))
        for k, slab in enumerate(slabs):
            hk = h_in[:, k * LANES:(k + 1) * LANES]
            for r in order:
                lh_scr[slab, pl.ds(r, nrun, stride=LRU_RUN), :] = run_a[k, r] * hk + run_b[k, r]
        yield
        for k, slab in enumerate(slabs):
            o_ref[:, g * MXU_DIM + k * LANES:g * MXU_DIM + (k + 1) * LANES] = lh_scr[slab].astype(o_ref.dtype)

    lane = lax.broadcasted_iota(jnp.int32, (1, LANES), 1)
    dt_lo = SMALL_DT + (SSD_HEADS if rev else 0)
    dt_mask = (lane >= dt_lo) & (lane < dt_lo + SSD_HEADS)
    dt = jnp.where(dt_mask, _softplus_log(small + dtb_ref[...]), 0.0)
    la = dt * (-LOG2E * jnp.exp(alog_ref[...]))
    bcum = _seg_cumsum(la, tc, rev)
    bcum_t = bcum.T
    tot = bcum[0:1, :] if rev else bcum[tc - 1:tc, :]

    half = tc // 2
    ri = lax.broadcasted_iota(jnp.int32, (half, half), 0)
    ci = lax.broadcasted_iota(jnp.int32, (half, half), 1)
    tri = (ci >= ri) if rev else (ci <= ri)
    lane_t = lax.broadcasted_iota(jnp.int32, (tc, LANES), 1)
    lo_half = lane_t < SSD_HEAD_DIM
    row_s = lax.broadcasted_iota(jnp.int32, (2 * SSD_STATE, SSD_LANES), 0)
    lane_s = lax.broadcasted_iota(jnp.int32, (1, SSD_LANES), 1) // SSD_HEAD_DIM
    b_off = 2 * D_MODEL
    c_off = b_off + SSD_GROUPS * SSD_STATE
    ssd_upd = {}

    def ssd_group(g):
        pair, gl = divmod(g, 2)
        b_tile = p_ref[:, b_off + pair * LANES:b_off + (pair + 1) * LANES].astype(BF16)
        c_tile = p_ref[:, c_off + pair * LANES:c_off + (pair + 1) * LANES]
        s_pair = ssd_scr[pair * LANES:(pair + 1) * LANES, :]
        gmask = lo_half if gl == 0 else jnp.logical_not(lo_half)
        c_g = jnp.where(gmask, c_tile, 0.0).astype(BF16)
        gmat = _dot_nt(c_g, b_tile)
        inter = _dot(c_g, s_pair.astype(BF16))
        yield
        wx_parts = []
        for hp in range(2):
            h0 = g * 4 + hp * 2
            col0 = bcum[:, dt_lo + h0:dt_lo + h0 + 1]
            col1 = bcum[:, dt_lo + h0 + 1:dt_lo + h0 + 2]
            dtc0 = dt[:, dt_lo + h0:dt_lo + h0 + 1]
            dtc1 = dt[:, dt_lo + h0 + 1:dt_lo + h0 + 2]
            xsl = slice(D_MODEL + h0 * SSD_HEAD_DIM, D_MODEL + (h0 + 2) * SSD_HEAD_DIM)
            xs = p_ref[:, xsl]
            xhat = xs * jnp.where(lo_half, dtc0, dtc1)
            xhat_bf = xhat.astype(BF16)
            yield
            outs = []
            for hh, col in ((0, col0), (1, col1)):
                rowv = bcum_t[dt_lo + h0 + hh:dt_lo + h0 + hh + 1, :]
                outs.append((yield from _ssd_intra(gmat, col, rowv, xhat_bf, tri, rev)))
            o_pair = jnp.where(lo_half, outs[0], outs[1])
            colp = jnp.where(lo_half, col0, col1)
            isl = slice(hp * LANES, (hp + 1) * LANES)
            o_pair = o_pair + jnp.exp2(colp) * inter[:, isl]
            if not rev:
                o_pair = o_pair + dskip_ref[:, h0 * SSD_HEAD_DIM:(h0 + 2) * SSD_HEAD_DIM] * xs
            o_ref[:, xsl] = o_pair.astype(o_ref.dtype)
            yield
            t0 = tot[:, dt_lo + h0:dt_lo + h0 + 1]
            t1 = tot[:, dt_lo + h0 + 1:dt_lo + h0 + 2]
            totp = jnp.where(lane < SSD_HEAD_DIM, t0, t1)
            wx_parts.append((jnp.exp2(totp - colp) * xhat).astype(BF16))
        wx = jnp.concatenate(wx_parts, axis=1)
        ssd_upd[g] = _dot_tn(b_tile, wx)
        yield
        d = jnp.zeros((1, SSD_LANES), F32)
        for hl in range(4):
            th = tot[:, dt_lo + g * 4 + hl:dt_lo + g * 4 + hl + 1]
            d = jnp.where(lane_s == hl, jnp.exp2(th), d)
        ssd_upd[("dec", g)] = d
        if gl == 1:
            first = row_s < SSD_STATE
            new = jnp.where(first, ssd_upd[g - 1], ssd_upd[g])
            dec = jnp.where(first, ssd_upd[("dec", g - 1)], d)
            ssd_scr[pair * LANES:(pair + 1) * LANES, :] = dec * s_pair + new

    small_bf = small.astype(BF16)
    gla_bg = {}

    def gla_log_decay(hd):
        ksl = slice(hd * GLA_HEAD_K, (hd + 1) * GLA_HEAD_K)
        z = _dot(small_bf, w2_ref[:, ksl]) + gb_ref[:, ksl]
        yield
        ld = -_softplus_log(-z) * (LOG2E / GLA_GATE_NORM)
        yield
        gla_bg[hd] = _seg_cumsum(ld, GLA_SUB, rev)
        yield

    nsub = tc // GLA_SUB
    nblk = GLA_SUB // GLA_BLK
    scale = GLA_HEAD_K ** -0.5
    gla_st = [gla_scr[hd * GLA_HEAD_K:(hd + 1) * GLA_HEAD_K, :] for hd in range(GLA_HEADS)]

    def gla_subchunk(sc_i, hd):
        ksl = slice(hd * GLA_HEAD_K, (hd + 1) * GLA_HEAD_K)
        st = gla_st[hd]
        rsl = slice(sc_i * GLA_SUB, (sc_i + 1) * GLA_SUB)
        q = p_ref[rsl, Q_OFF + hd * GLA_HEAD_K:Q_OFF + (hd + 1) * GLA_HEAD_K] * scale
        k = p_ref[rsl, K_OFF + hd * GLA_HEAD_K:K_OFF + (hd + 1) * GLA_HEAD_K]
        v = p_ref[rsl, V_OFF + hd * GLA_HEAD_V:V_OFF + (hd + 1) * GLA_HEAD_V].astype(BF16)
        b = gla_bg[hd][rsl, :]
        inter = _dot((q * jnp.exp2(b)).astype(BF16), st.astype(BF16))
        yield
        o_rows = []
        for blk in range(nblk):
            bsl = slice(blk * GLA_BLK, (blk + 1) * GLA_BLK)
            if rev:
                keys = slice(blk * GLA_BLK, GLA_SUB)
                ref_row = b[(blk + 1) * GLA_BLK - 1:(blk + 1) * GLA_BLK, :]
            else:
                keys = slice(0, (blk + 1) * GLA_BLK)
                ref_row = b[blk * GLA_BLK:blk * GLA_BLK + 1, :]
            nk = keys.stop - keys.start
            qt = (q[bsl, :] * jnp.exp2(b[bsl, :] - ref_row)).astype(BF16)
            kt = (k[keys, :] * jnp.exp2(ref_row - b[keys, :])).astype(BF16)
            a_blk = _dot_nt(qt, kt)
            yield
            ai = lax.broadcasted_iota(jnp.int32, (GLA_BLK, nk), 0)
            aj = lax.broadcasted_iota(jnp.int32, (GLA_BLK, nk), 1)
            vis = (aj >= ai) if rev else (aj <= ai + blk * GLA_BLK)
            a_blk = jnp.where(vis, a_blk, 0.0).astype(BF16)
            o_rows.append(inter[bsl, :] + _dot(a_blk, v[keys, :]))
            yield
        o = jnp.concatenate(o_rows, axis=0)
        o_ref[rsl, 2 * D_MODEL + hd * GLA_HEAD_V:2 * D_MODEL + (hd + 1) * GLA_HEAD_V] = o.astype(o_ref.dtype)
        end = b[0:1, :] if rev else b[GLA_SUB - 1:GLA_SUB, :]
        k_dec = (k * jnp.exp2(end - b)).astype(BF16)
        dec_col = jnp.broadcast_to(jnp.exp2(end), (GLA_HEAD_K, GLA_HEAD_K)).T[:, 0:1]
        gla_st[hd] = dec_col * st + _dot_tn(k_dec, v)
        yield

    def gla_head(hd):
        yield from gla_log_decay(hd)
        for step in range(nsub):
            yield from gla_subchunk((nsub - 1 - step) if rev else step, hd)

    def lru_all():
        for g in range(D_MODEL // MXU_DIM):
            yield from lru_group(g)

    def ssd_all():
        for g in range(SSD_GROUPS):
            yield from ssd_group(g)

    tasks = [(gla_head(hd), 1) for hd in range(GLA_HEADS)] + [(ssd_all(), 2), (lru_all(), 1)]

    def finalize():
        for hd in range(GLA_HEADS):
            gla_scr[hd * GLA_HEAD_K:(hd + 1) * GLA_HEAD_K, :] = gla_st[hd]

    return tasks, finalize


def _scan(p1, tok0, nseq, nchunks, lw, l, init, o_prev, st_prev, st_shapes):
    tc = T_CHUNK
    blk0 = tok0 // tc
    m = p1.shape[0]

    def row_block(d):
        return lambda b, c: (blk0 + b * nchunks + ((nchunks - 1 - c) if d else c), 0)

    const2 = lambda b, c: (0, 0)
    const3 = lambda b, c: (0, 0, 0)
    in_specs, args = [], []
    for d in range(2):
        in_specs += [
            pl.BlockSpec((tc, P1_COLS), row_block(d)),
            pl.BlockSpec((D_MODEL // MXU_DIM, MXU_DIM, 2 * MXU_DIM), const3),
            pl.BlockSpec((1, D_MODEL), const2),
            pl.BlockSpec((1, D_MODEL), const2),
            pl.BlockSpec((1, D_MODEL), const2),
            pl.BlockSpec((1, LANES), const2),
            pl.BlockSpec((1, LANES), const2),
            pl.BlockSpec((LANES, GLA_DK), const2),
            pl.BlockSpec((1, GLA_DK), const2),
        ]
        args += [p1, lw["lru_w"][d], lw["lru_ba"][d], lw["lru_bx"][d], lw["lam"][d], lw["dt_bias"][d],
                 lw["a_log"][d], lw["gate_w2"][d], lw["gate_b"][d]]
        assert len(in_specs) == (d + 1) * N_DIR_PARAMS
    in_specs.append(pl.BlockSpec((1, D_MODEL), const2))
    args.append(lw["d_skip"])
    state_specs = [
        pl.BlockSpec((None, None, 2, 1, D_MODEL), lambda b, c: (b, l, 0, 0, 0)),
        pl.BlockSpec((None, None, 2, SSD_HEADS, SSD_STATE, SSD_HEAD_DIM), lambda b, c: (b, l, 0, 0, 0, 0)),
        pl.BlockSpec((None, None, 2, GLA_ROWS, GLA_HEAD_V), lambda b, c: (b, l, 0, 0, 0)),
    ]
    if init is not None:
        in_specs += state_specs
        args += list(init)
    aliases = {}
    prev = list(o_prev or []) + list(st_prev or [])
    first_out = 0 if o_prev is not None else 2
    for k, arr in enumerate(prev):
        aliases[len(args)] = first_out + k
        in_specs.append(pl.BlockSpec(memory_space=pl.ANY))
        args.append(arr)
    out_specs = [pl.BlockSpec((tc, 3 * D_MODEL), row_block(d)) for d in range(2)]
    out_shape = [jax.ShapeDtypeStruct((m, 3 * D_MODEL), BF16)] * 2
    if st_shapes is not None:
        out_specs += state_specs
        out_shape += [jax.ShapeDtypeStruct(s, F32) for s in st_shapes]
    kern = functools.partial(_scan_kernel, zero_init=init is None, emit_state=st_shapes is not None,
                             n_alias=len(prev), nchunks=nchunks)
    dir_scratch = [
        pltpu.VMEM((SUBLANES, D_MODEL), F32),
        pltpu.VMEM((SSD_ROWS, SSD_LANES), F32),
        pltpu.VMEM((GLA_ROWS, GLA_HEAD_V), F32),
        pltpu.VMEM((D_MODEL // LANES, tc, LANES), F32),
        pltpu.VMEM((D_MODEL // LANES, tc, LANES), F32),
        pltpu.VMEM((D_MODEL // LANES, tc, LANES), F32),
    ]
    assert len(dir_scratch) == N_DIR_SCRATCH
    return pl.pallas_call(
        kern,
        grid=(nseq, nchunks),
        in_specs=in_specs,
        out_specs=out_specs,
        out_shape=out_shape,
        input_output_aliases=aliases,
        scratch_shapes=dir_scratch * 2,
        compiler_params=_params(("parallel", "arbitrary")),
        name="scan",
    )(*args)


def _post_kernel(x_ref, of_ref, ob_ref, p2_ref, mod_ref, ssdg_ref, glag_ref,
                 wl_ref, ws_ref, wg_ref, wo_ref, o_ref):
    dm = D_MODEL

    def both(sl):
        return of_ref[:, sl].astype(F32) + ob_ref[:, sl].astype(F32)

    def p2(k):
        return p2_ref[:, k * dm:(k + 1) * dm].astype(F32)

    y_lru = both(slice(0, dm)) * _gelu_tanh(p2(0))
    y_ssd = _rms(both(slice(dm, 2 * dm)) * _silu(p2(1)), ssdg_ref[...])
    parts = []
    for hd in range(GLA_HEADS):
        sl = slice(2 * dm + hd * GLA_HEAD_V, 2 * dm + (hd + 1) * GLA_HEAD_V)
        parts.append(_rms(both(sl), glag_ref[...]))
    y_gla = jnp.concatenate(parts, axis=1) * _silu(p2(2))
    merged = (_sigmoid(p2(3)) * _dot(y_lru.astype(BF16), wl_ref[...])
              + _sigmoid(p2(4)) * _dot(y_ssd.astype(BF16), ws_ref[...])
              + _sigmoid(p2(5)) * _dot(y_gla.astype(BF16), wg_ref[...]))
    out = _dot(merged.astype(BF16), wo_ref[...])
    o_ref[...] = x_ref[...] + mod_ref[:, 2 * dm:3 * dm] * out


def _post(x, o_f, o_b, p2, mod, row_of_block, lw, l):
    m = x.shape[0]
    tm = TM_POST
    wspec = pl.BlockSpec((None, D_MODEL, D_MODEL), lambda i: (l, 0, 0), pipeline_mode=pl.Buffered(1))
    return pl.pallas_call(
        _post_kernel,
        grid=(m // tm,),
        in_specs=[
            pl.BlockSpec((tm, D_MODEL), lambda i: (i, 0)),
            pl.BlockSpec((tm, 3 * D_MODEL), lambda i: (i, 0)),
            pl.BlockSpec((tm, 3 * D_MODEL), lambda i: (i, 0)),
            pl.BlockSpec((tm, P2_COLS), lambda i: (i, 0)),
            pl.BlockSpec((None, 1, 6 * D_MODEL), lambda i: (row_of_block(i, tm), 0, 0)),
            pl.BlockSpec((1, D_MODEL), lambda i: (0, 0)),
            pl.BlockSpec((1, GLA_HEAD_V), lambda i: (0, 0)),
            wspec, wspec, wspec, wspec,
        ],
        out_specs=pl.BlockSpec((tm, D_MODEL), lambda i: (i, 0)),
        out_shape=jax.ShapeDtypeStruct((m, D_MODEL), F32),
        compiler_params=_params(("parallel",)),
        name="post",
    )(x, o_f, o_b, p2, mod, lw["ssd_norm_g"], lw["gla_norm_g"],
      lw["w_br_lru"], lw["w_br_ssd"], lw["w_br_gla"], lw["w_out"])


def _ffn_kernel(x_ref, mod_ref, g_ref, wi_ref, wo_ref, fg_ref, o_ref, *, final_norm, nsplit):
    dm = D_MODEL
    x = x_ref[...]
    h = (_rms(x, g_ref[...]) * (1.0 + mod_ref[:, 4 * dm:5 * dm]) + mod_ref[:, 3 * dm:4 * dm]).astype(BF16)
    tf = D_FF // nsplit
    acc = jnp.zeros(x.shape, F32)
    for s in range(nsplit):
        fs = slice(s * tf, (s + 1) * tf)
        gate = _dot(h, wi_ref[:, fs])
        up = _dot(h, wi_ref[:, D_FF + s * tf:D_FF + (s + 1) * tf])
        acc = acc + _dot((_silu(gate) * up).astype(BF16), wo_ref[fs, :])
    y = x + mod_ref[:, 5 * dm:6 * dm] * acc
    if final_norm:
        y = _rms(y, fg_ref[...])
    o_ref[...] = y


def _ffn(x, mod, row_of_block, lw, l, final_g, final_norm):
    m = x.shape[0]
    tm = TM_FFN
    single = dict(pipeline_mode=pl.Buffered(1))
    kern = functools.partial(_ffn_kernel, final_norm=final_norm, nsplit=2)
    return pl.pallas_call(
        kern,
        grid=(m // tm,),
        in_specs=[
            pl.BlockSpec((tm, D_MODEL), lambda i: (i, 0)),
            pl.BlockSpec((None, 1, 6 * D_MODEL), lambda i: (row_of_block(i, tm), 0, 0)),
            pl.BlockSpec((1, D_MODEL), lambda i: (0, 0)),
            pl.BlockSpec((None, D_MODEL, 2 * D_FF), lambda i: (l, 0, 0), **single),
            pl.BlockSpec((None, D_FF, D_MODEL), lambda i: (l, 0, 0), **single),
            pl.BlockSpec((1, D_MODEL), lambda i: (0, 0)),
        ],
        out_specs=pl.BlockSpec((tm, D_MODEL), lambda i: (i, 0)),
        out_shape=jax.ShapeDtypeStruct((m, D_MODEL), F32),
        compiler_params=_params(("parallel",)),
        name="ffn",
    )(x, mod, lw["norm_ffn_g"], lw["w_ffn_in"], lw["w_ffn_out"], final_g)


def _snake_rows(src_ref, dst_ref):
    rows = src_ref.shape[0]
    for g0 in range(0, rows, 2 * GRID_W):
        dst_ref[g0:g0 + GRID_W, :] = src_ref[g0:g0 + GRID_W, :]
        for r in range(GRID_W):
            dst_ref[pl.ds(g0 + GRID_W + r, 1), :] = src_ref[pl.ds(g0 + 2 * GRID_W - 1 - r, 1), :]


def _assemble_kernel(xa_ref, xb_ref, o_ref, *, nb_ctx):
    i = pl.program_id(0)

    @pl.when(i < nb_ctx)
    def _():
        o_ref[...] = xa_ref[...]

    @pl.when(i >= nb_ctx)
    def _():
        _snake_rows(xb_ref, o_ref)


def _split_kernel(y_ref, ya_ref, yb_ref, *, nb_ctx):
    i = pl.program_id(0)

    @pl.when(i < nb_ctx)
    def _():
        ya_ref[...] = y_ref[...]

    @pl.when(i >= nb_ctx)
    def _():
        _snake_rows(y_ref, yb_ref)


def _assemble_tokens(x_ctx, x_lat):
    tr = TOKEN_ROWS
    nb_ctx, nb_lat = x_ctx.shape[0] // tr, x_lat.shape[0] // tr
    return pl.pallas_call(
        functools.partial(_assemble_kernel, nb_ctx=nb_ctx),
        grid=(nb_ctx + nb_lat,),
        in_specs=[pl.BlockSpec((tr, D_MODEL), lambda i: (jnp.minimum(i, nb_ctx - 1), 0)),
                  pl.BlockSpec((tr, D_MODEL), lambda i: (jnp.maximum(i - nb_ctx, 0), 0))],
        out_specs=pl.BlockSpec((tr, D_MODEL), lambda i: (i, 0)),
        out_shape=jax.ShapeDtypeStruct((x_ctx.shape[0] + x_lat.shape[0], D_MODEL), x_ctx.dtype),
        compiler_params=_params(("arbitrary",)),
        name="assemble_tokens",
    )(x_ctx, x_lat)


def _split_tokens(y, m_ctx):
    tr = TOKEN_ROWS
    nb_ctx, nb_lat = m_ctx // tr, (y.shape[0] - m_ctx) // tr
    return pl.pallas_call(
        functools.partial(_split_kernel, nb_ctx=nb_ctx),
        grid=(nb_ctx + nb_lat,),
        in_specs=[pl.BlockSpec((tr, D_MODEL), lambda i: (i, 0))],
        out_specs=[pl.BlockSpec((tr, D_MODEL), lambda i: (jnp.minimum(i, nb_ctx - 1), 0)),
                   pl.BlockSpec((tr, D_MODEL), lambda i: (jnp.maximum(i - nb_ctx, 0), 0))],
        out_shape=[jax.ShapeDtypeStruct((m_ctx, D_MODEL), y.dtype),
                   jax.ShapeDtypeStruct((y.shape[0] - m_ctx, D_MODEL), y.dtype)],
        compiler_params=_params(("arbitrary",)),
        name="split_tokens",
    )(y)


_O_GATE, _O_Z, _O_XBC = D_MODEL, 2 * D_MODEL, 3 * D_MODEL
_O_DT = _O_XBC + CONV_COLS - D_MODEL
_O_Q = _O_DT + 2 * SSD_HEADS
_O_G = _O_Q + 2 * GLA_DK + D_MODEL
_O_LR = _O_G + D_MODEL
_O_M = _O_LR + 2 * GLA_GATE_RANK
IN_WIDTH = _O_M + 3 * D_MODEL
W_PREP_ROWS = 128


def _wprep_kernel(w_ref, w1_ref, w2_ref):
    def cols(lo, hi):
        return w_ref[:, lo:hi]

    w1_ref[:, 0:D_MODEL] = cols(0, D_MODEL)
    w1_ref[:, D_MODEL:CONV_COLS] = cols(_O_XBC, _O_DT)
    w1_ref[:, Q_OFF:SMALL_OFF] = cols(_O_Q, _O_G)
    assert _O_DT % LANES == SMALL_DT and _O_LR % LANES == SMALL_LR
    lane = lax.broadcasted_iota(jnp.int32, (w_ref.shape[0], LANES), 1)
    dt_tile = w_ref[:, _O_DT:_O_DT + LANES]
    lr_tile = w_ref[:, _O_LR - SMALL_LR:_O_LR - SMALL_LR + LANES]
    w1_ref[:, SMALL_OFF:] = jnp.where(lane < SMALL_LR, dt_tile,
                                      jnp.where(lane < 2 * SMALL_LR, lr_tile, jnp.zeros_like(lr_tile)))
    w2_ref[:, 0:2 * D_MODEL] = cols(_O_GATE, _O_XBC)
    w2_ref[:, 2 * D_MODEL:3 * D_MODEL] = cols(_O_G, _O_LR)
    w2_ref[:, 3 * D_MODEL:] = cols(_O_M, IN_WIDTH)


def _prep_w_in(w_in):
    n_layers = w_in.shape[0]
    tr = W_PREP_ROWS
    lane_pad = -IN_WIDTH % LANES
    w_bf = jnp.pad(w_in.astype(BF16), ((0, 0), (0, 0), (0, lane_pad)))
    return pl.pallas_call(
        _wprep_kernel,
        grid=(n_layers, D_MODEL // tr),
        in_specs=[pl.BlockSpec((None, tr, IN_WIDTH + lane_pad), lambda l, r: (l, r, 0))],
        out_specs=[pl.BlockSpec((None, tr, P1_COLS), lambda l, r: (l, r, 0)),
                   pl.BlockSpec((None, tr, P2_COLS), lambda l, r: (l, r, 0))],
        out_shape=[jax.ShapeDtypeStruct((n_layers, D_MODEL, P1_COLS), BF16),
                   jax.ShapeDtypeStruct((n_layers, D_MODEL, P2_COLS), BF16)],
        compiler_params=_params(("parallel", "parallel")),
        name="w_in_relayout",
    )(w_bf)


def _layer_weights(l, w):
    dm = D_MODEL

    def block_diag_tiles(wa, wx):
        per = MXU_DIM // LRU_BLOCK_W
        eye = jnp.eye(per, dtype=wa.dtype)
        def tiles(wb):
            wb = wb.reshape(dm // MXU_DIM, per, LRU_BLOCK_W, LRU_BLOCK_W)
            t = jnp.einsum("gpwv,pq->gpwqv", wb, eye)
            return t.reshape(dm // MXU_DIM, MXU_DIM, MXU_DIM)
        return jnp.concatenate([tiles(wa), tiles(wx)], axis=2).astype(BF16)

    def lane_pad(v, lo):
        return jnp.zeros((1, LANES), F32).at[0, lo:lo + v.shape[0]].set(v)

    def gate_w2(d):
        lo = SMALL_LR + d * GLA_GATE_RANK
        return jnp.zeros((LANES, GLA_DK), F32).at[lo:lo + GLA_GATE_RANK].set(w["gla_gate_w2"][l, d]).astype(BF16)

    return dict(
        w1=w["w1"], w2=w["w2"],
        norm_mix_g=w["norm_mix_g"][l][None], norm_ffn_g=w["norm_ffn_g"][l][None],
        conv_w=jnp.concatenate([w["lru_conv_w"][l], w["ssd_conv_w"][l]], axis=1),
        conv_b=jnp.concatenate([w["lru_conv_b"][l], w["ssd_conv_b"][l]])[None],
        lru_w=[block_diag_tiles(0.5 * w["lru_w_a"][l, d], 0.5 * w["lru_w_x"][l, d]) for d in range(2)],
        lru_ba=[0.5 * w["lru_b_a"][l, d][None] for d in range(2)],
        lru_bx=[0.5 * w["lru_b_x"][l, d][None] for d in range(2)],
        lam=[w["lru_lambda"][l, d][None] for d in range(2)],
        dt_bias=[lane_pad(w["ssd_dt_bias"][l, d], SMALL_DT + d * SSD_HEADS) for d in range(2)],
        a_log=[lane_pad(w["ssd_a_log"][l, d], SMALL_DT + d * SSD_HEADS) for d in range(2)],
        d_skip=jnp.repeat(w["ssd_d"][l], SSD_HEAD_DIM)[None],
        gate_w2=[gate_w2(d) for d in range(2)],
        gate_b=[w["gla_gate_b"][l, d][None] for d in range(2)],
        ssd_norm_g=w["ssd_norm_g"][l][None], gla_norm_g=w["gla_norm_g"][l][None],
        w_br_lru=w["w_br_lru_bf"], w_br_ssd=w["w_br_ssd_bf"], w_br_gla=w["w_br_gla_bf"], w_out=w["w_out_bf"],
        w_ffn_in=w["w_ffn_in_bf"], w_ffn_out=w["w_ffn_out_bf"],
    )


def kernel(x_prompt, x_sample, state_lru, state_ssd, state_gla, c, c_ctx, norm_mix_g, norm_ffn_g, w_ada, b_ada, w_in, lru_conv_w, lru_conv_b, lru_w_a, lru_b_a, lru_w_x, lru_b_x, lru_lambda, ssd_conv_w, ssd_conv_b, ssd_dt_bias, ssd_a_log, ssd_d, ssd_norm_g, gla_gate_w2, gla_gate_b, gla_norm_g, w_br_lru, w_br_ssd, w_br_gla, w_out, w_ffn_in, w_ffn_out, final_norm_g):
    w = dict(norm_mix_g=norm_mix_g, norm_ffn_g=norm_ffn_g, w_in=w_in, lru_conv_w=lru_conv_w,
             lru_conv_b=lru_conv_b, lru_w_a=lru_w_a, lru_b_a=lru_b_a, lru_w_x=lru_w_x, lru_b_x=lru_b_x,
             lru_lambda=lru_lambda, ssd_conv_w=ssd_conv_w, ssd_conv_b=ssd_conv_b, ssd_dt_bias=ssd_dt_bias,
             ssd_a_log=ssd_a_log, ssd_d=ssd_d, ssd_norm_g=ssd_norm_g, gla_gate_w2=gla_gate_w2,
             gla_gate_b=gla_gate_b, gla_norm_g=gla_norm_g, w_br_lru=w_br_lru, w_br_ssd=w_br_ssd,
             w_br_gla=w_br_gla, w_out=w_out, w_ffn_in=w_ffn_in, w_ffn_out=w_ffn_out)
    w["w1"], w["w2"] = _prep_w_in(w_in)
    for name in ("w_br_lru", "w_br_ssd", "w_br_gla", "w_out", "w_ffn_in", "w_ffn_out"):
        w[name + "_bf"] = w[name].astype(BF16)
    n_layers = w_in.shape[0]
    b_ctx, t_ctx, dm = x_prompt.shape
    b_lat, t_lat, _ = x_sample.shape
    m_ctx = b_ctx * t_ctx
    m_lat = b_lat * t_lat
    nch_ctx = t_ctx // T_CHUNK
    nch_lat = t_lat // T_CHUNK

    n_rows = -(-(1 + b_lat) // SUBLANES) * SUBLANES
    cond = jnp.zeros((n_rows, dm), F32).at[0].set(c_ctx).at[1:1 + b_lat].set(c)
    mod = _modulation(cond, w_ada, b_ada)

    def row_of_block(i, tm):
        tok = i * tm
        return jnp.where(tok < m_ctx, 0, 1 + (tok - m_ctx) // t_lat)

    def seq_pos(tok):
        in_ctx = tok < m_ctx
        pos = jnp.where(in_ctx, lax.rem(tok, t_ctx), lax.rem(tok - m_ctx, t_lat))
        return pos, jnp.where(in_ctx, t_ctx, t_lat)

    x = _assemble_tokens(x_prompt.reshape(m_ctx, dm), x_sample.reshape(m_lat, dm))

    st_shapes = [(b_ctx, n_layers, 2, 1, dm),
                 (b_ctx, n_layers, 2, SSD_HEADS, SSD_STATE, SSD_HEAD_DIM),
                 (b_ctx, n_layers, 2, GLA_ROWS, GLA_HEAD_V)]
    init = (state_lru.reshape(b_lat, n_layers, 2, 1, dm), state_ssd,
            state_gla.reshape(b_lat, n_layers, 2, GLA_ROWS, GLA_HEAD_V))
    states = None
    for l in range(n_layers):
        lw = _layer_weights(l, w)
        mod_l = mod[l][:, None, :]
        p1, p2 = _inproj(x, mod_l, row_of_block, seq_pos, lw, l)
        o_f, o_b, *states = _scan(p1, 0, b_ctx, nch_ctx, lw, l, None, None, states, st_shapes)
        outs = _scan(p1, m_ctx, b_lat, nch_lat, lw, l, init, [o_f, o_b], None, None)
        x = _post(x, outs[0], outs[1], p2, mod_l, row_of_block, lw, l)
        x = _ffn(x, mod_l, row_of_block, lw, l, final_norm_g[None], l == n_layers - 1)

    y_ctx, y_lat = _split_tokens(x, m_ctx)
    y_prompt = y_ctx.reshape(b_ctx, t_ctx, dm)
    y_sample = y_lat.reshape(b_lat, t_lat, dm)
    return (y_prompt, y_sample, states[0].reshape(b_ctx, n_layers, 2, dm), states[1],
            states[2].reshape(b_ctx, n_layers, 2, GLA_HEADS, GLA_HEAD_K, GLA_HEAD_V))
```

```python
import functools

import jax
import jax.numpy as jnp
from jax import lax
from jax.experimental import pallas as pl
from jax.experimental.pallas import tpu as pltpu

F32 = jnp.float32
BF16 = jnp.bfloat16

D_MODEL = 1024
GRID_W = 64
CONV_W = 4
CONV_LEFT = 2
LRU_BLOCK_W = 64
LRU_C = 8.0
SSD_HEAD_DIM = 64
SSD_HEADS = 16
SSD_GROUPS = 4
SSD_STATE = 64
GLA_HEADS = 4
GLA_HEAD_K = 128
GLA_HEAD_V = 256
GLA_DK = GLA_HEADS * GLA_HEAD_K
GLA_GATE_RANK = 16
GLA_GATE_NORM = 16.0
D_FF = 2816
EPS = 1e-6

LANES = 128
SUBLANES = 8
MXU_DIM = 256
VMEM_LIMIT_BYTES = 60 * 1024 * 1024

CONV_COLS = D_MODEL + D_MODEL + 2 * SSD_GROUPS * SSD_STATE
Q_OFF = CONV_COLS
K_OFF = Q_OFF + GLA_DK
V_OFF = K_OFF + GLA_DK
SMALL_OFF = V_OFF + D_MODEL
P1_COLS = SMALL_OFF + LANES
P2_COLS = 6 * D_MODEL
SMALL_DT = 0
SMALL_LR = 2 * SSD_HEADS

T_CHUNK = 256
IN_CHUNK = 256
TM_POST = 512
TM_FFN = 512
TOKEN_ROWS = 1024
LRU_RUN = 4
GLA_SUB = 64
GLA_BLK = 16
NEG_BIG = -1e30
LOG2E = 1.4426950408889634
SSD_ROWS = SSD_GROUPS * SSD_STATE
SSD_LANES = (SSD_HEADS // SSD_GROUPS) * SSD_HEAD_DIM
GLA_ROWS = GLA_HEADS * GLA_HEAD_K


def _params(sem, **kw):
    return pltpu.CompilerParams(dimension_semantics=sem, vmem_limit_bytes=VMEM_LIMIT_BYTES, **kw)


def _softplus(x):
    return jnp.maximum(x, 0.0) + jnp.log1p(jnp.exp(-jnp.abs(x)))


def _softplus_log(x):
    return jnp.maximum(x, 0.0) + jnp.log(1.0 + jnp.exp(-jnp.abs(x)))


def _sigmoid(x):
    return 0.5 * (1.0 + jnp.tanh(0.5 * x))


def _silu(x):
    return x * _sigmoid(x)


def _gelu_tanh(x):
    c = 0.7978845608028654
    return 0.5 * x * (1.0 + jnp.tanh(c * (x + 0.044715 * (x * x * x))))


def _rms(x, g):
    return x * lax.rsqrt(jnp.mean(x * x, axis=-1, keepdims=True) + EPS) * g


def _dot(a, b):
    return jnp.dot(a, b, preferred_element_type=F32)


def _dot_nt(a, b):
    return lax.dot_general(a, b, (((1,), (1,)), ((), ())), preferred_element_type=F32)


def _dot_tn(a, b):
    return lax.dot_general(a, b, (((0,), (0,)), ((), ())), preferred_element_type=F32)


def _mod_kernel(c_ref, w_ref, b_ref, o_ref):
    c = _silu(c_ref[...]).astype(BF16)
    o_ref[...] = _dot(c, w_ref[...].astype(BF16)) + b_ref[...]


def _modulation(cond, w_ada, b_ada):
    n_layers = w_ada.shape[0]
    rows = cond.shape[0]
    tn = D_MODEL
    return pl.pallas_call(
        _mod_kernel,
        grid=(n_layers, 6 * D_MODEL // tn),
        in_specs=[
            pl.BlockSpec((rows, D_MODEL), lambda l, j: (0, 0)),
            pl.BlockSpec((None, D_MODEL, tn), lambda l, j: (l, 0, j)),
            pl.BlockSpec((None, 1, tn), lambda l, j: (l, 0, j)),
        ],
        out_specs=pl.BlockSpec((None, rows, tn), lambda l, j: (l, 0, j)),
        out_shape=jax.ShapeDtypeStruct((n_layers, rows, 6 * D_MODEL), F32),
        compiler_params=_params(("parallel", "parallel")),
        name="modulation",
    )(cond, w_ada, b_ada.reshape(n_layers, 1, 6 * D_MODEL))


def _inproj_kernel(x_ref, xp_ref, xn_ref, mod_ref, g_ref, w1_ref, w2_ref, cw_ref, cb_ref,
                   p1_ref, p2_ref, hext, cbuf, obuf, *, seq_pos):
    tc = T_CHUNK
    sh = mod_ref[:, 0:D_MODEL]
    sc1 = 1.0 + mod_ref[:, D_MODEL:2 * D_MODEL]
    g = g_ref[...]
    h_main = _rms(x_ref[...], g) * sc1 + sh
    hext[SUBLANES:SUBLANES + tc, :] = h_main
    pos, seq_len = seq_pos(pl.program_id(0) * tc)
    head = jnp.where(pos == 0, 0.0, 1.0)
    tail = jnp.where(pos + tc == seq_len, 0.0, 1.0)
    hext[0:SUBLANES, :] = (_rms(xp_ref[...], g) * sc1 + sh) * head
    hext[SUBLANES + tc:, :] = (_rms(xn_ref[...], g) * sc1 + sh) * tail
    he = hext[...].astype(BF16)
    hb = h_main.astype(BF16)

    def conv_columns():
        half = tc // 2
        for c0 in range(0, CONV_COLS, IN_CHUNK):
            res = _dot(he, w1_ref[:, c0:c0 + IN_CHUNK])
            for k in range(IN_CHUNK // LANES):
                cbuf[c0 // LANES + k] = res[:, k * LANES:(k + 1) * LANES]
            yield
            for k in range(IN_CHUNK // LANES):
                slab = c0 // LANES + k
                ls = slice(slab * LANES, (slab + 1) * LANES)
                for parity in range(2):
                    xc = cb_ref[:, ls]
                    for j in range(CONV_W):
                        row0 = SUBLANES - CONV_LEFT + j + parity
                        xc = xc + cw_ref[j:j + 1, ls] * cbuf[slab, pl.ds(row0, half, stride=2), :]
                    obuf[slab, pl.ds(parity, half, stride=2), :] = xc if c0 < D_MODEL else _silu(xc)
                if k % 2 == 1:
                    yield
            for k in range(IN_CHUNK // LANES):
                slab = c0 // LANES + k
                p1_ref[:, slab * LANES:(slab + 1) * LANES] = obuf[slab]
            yield

    def plain_columns():
        for c0 in range(CONV_COLS, P1_COLS, IN_CHUNK):
            cs = slice(c0, min(c0 + IN_CHUNK, P1_COLS))
            p1_ref[:, cs] = _dot(hb, w1_ref[:, cs])
            yield
        for c0 in range(0, P2_COLS, IN_CHUNK):
            cs = slice(c0, c0 + IN_CHUNK)
            p2_ref[:, cs] = _dot(hb, w2_ref[:, cs]).astype(p2_ref.dtype)
            yield

    _round_robin([(conv_columns(), 1), (plain_columns(), 1)])


def _inproj(x, mod, row_of_block, seq_pos, lw, l):
    m = x.shape[0]
    tc = T_CHUNK
    rows8 = tc // SUBLANES
    last8 = m // SUBLANES - 1
    single = dict(pipeline_mode=pl.Buffered(1))
    return pl.pallas_call(
        functools.partial(_inproj_kernel, seq_pos=seq_pos),
        grid=(m // tc,),
        in_specs=[
            pl.BlockSpec((tc, D_MODEL), lambda i: (i, 0)),
            pl.BlockSpec((SUBLANES, D_MODEL), lambda i: (jnp.maximum(i * rows8 - 1, 0), 0)),
            pl.BlockSpec((SUBLANES, D_MODEL), lambda i: (jnp.minimum((i + 1) * rows8, last8), 0)),
            pl.BlockSpec((None, 1, 6 * D_MODEL), lambda i: (row_of_block(i, tc), 0, 0)),
            pl.BlockSpec((1, D_MODEL), lambda i: (0, 0)),
            pl.BlockSpec((None, D_MODEL, P1_COLS), lambda i: (l, 0, 0), **single),
            pl.BlockSpec((None, D_MODEL, P2_COLS), lambda i: (l, 0, 0), **single),
            pl.BlockSpec((CONV_W, CONV_COLS), lambda i: (0, 0)),
            pl.BlockSpec((1, CONV_COLS), lambda i: (0, 0)),
        ],
        out_specs=[pl.BlockSpec((tc, P1_COLS), lambda i: (i, 0)),
                   pl.BlockSpec((tc, P2_COLS), lambda i: (i, 0))],
        out_shape=[jax.ShapeDtypeStruct((m, P1_COLS), F32), jax.ShapeDtypeStruct((m, P2_COLS), BF16)],
        scratch_shapes=[pltpu.VMEM((tc + 2 * SUBLANES, D_MODEL), F32),
                        pltpu.VMEM((CONV_COLS // LANES, tc + 2 * SUBLANES, LANES), F32),
                        pltpu.VMEM((CONV_COLS // LANES, tc, LANES), F32)],
        compiler_params=_params(("parallel",)),
        name="inproj",
    )(x, x, x, mod, lw["norm_mix_g"], lw["w1"], lw["w2"], lw["conv_w"], lw["conv_b"])


def _tile_scan(a, b, rev):
    t, w = a.shape
    a = a.reshape(t // SUBLANES, SUBLANES, w)
    b = b.reshape(t // SUBLANES, SUBLANES, w)
    pos = lax.broadcasted_iota(jnp.int32, (1, SUBLANES, w), 1)
    s = 1
    while s < SUBLANES:
        shift = (SUBLANES - s) if rev else s
        valid = (pos < SUBLANES - s) if rev else (pos >= s)
        a_sh = jnp.where(valid, pltpu.roll(a, shift, 1), 1.0)
        b_sh = jnp.where(valid, pltpu.roll(b, shift, 1), 0.0)
        yield
        b = a * b_sh + b
        a = a * a_sh
        yield
        s *= 2
    return a, b


def _delayed(gen, turns):
    for _ in range(turns):
        yield
    yield from gen


def _round_robin(tasks):
    tasks = list(tasks)
    while tasks:
        for task in list(tasks):
            gen, stages = task
            try:
                for _ in range(stages):
                    next(gen)
            except StopIteration:
                tasks.remove(task)


def _seg_cumsum(x, seg, rev):
    t, w = x.shape
    ntile = t // SUBLANES
    per_seg = seg // SUBLANES
    x = x.reshape(ntile, SUBLANES, w)
    pos = lax.broadcasted_iota(jnp.int32, (1, SUBLANES, w), 1)
    s = 1
    while s < SUBLANES:
        shift = (SUBLANES - s) if rev else s
        valid = (pos < SUBLANES - s) if rev else (pos >= s)
        x = x + jnp.where(valid, pltpu.roll(x, shift, 1), 0.0)
        s *= 2
    tiles = [None] * ntile
    for s0 in range(0, ntile, per_seg):
        carry = None
        for j in (range(s0 + per_seg - 1, s0 - 1, -1) if rev else range(s0, s0 + per_seg)):
            tiles[j] = x[j] if carry is None else x[j] + carry
            carry = tiles[j][0:1, :] if rev else tiles[j][SUBLANES - 1:SUBLANES, :]
    return jnp.stack(tiles, axis=0).reshape(t, w)


def _ssd_intra(gmat, col, rowv, xhat_bf, tri, rev):
    half = tri.shape[0]
    lo, hi = slice(0, half), slice(half, 2 * half)

    def blk(rs, cs, masked):
        e = col[rs, :] - rowv[:, cs]
        if masked:
            e = jnp.where(tri, e, NEG_BIG)
        return (gmat[rs, cs] * jnp.exp2(e)).astype(BF16)

    if rev:
        out_top = _dot(jnp.concatenate([blk(lo, lo, True), blk(lo, hi, False)], axis=1), xhat_bf)
        yield
        out_bot = _dot(blk(hi, hi, True), xhat_bf[hi, :])
    else:
        out_top = _dot(blk(lo, lo, True), xhat_bf[lo, :])
        yield
        out_bot = _dot(jnp.concatenate([blk(hi, lo, False), blk(hi, hi, True)], axis=1), xhat_bf)
    yield
    return jnp.concatenate([out_top, out_bot], axis=0)


def _ssd_state_slices(h):
    g, hl = divmod(h, SSD_HEADS // SSD_GROUPS)
    return (slice(g * SSD_STATE, (g + 1) * SSD_STATE),
            slice(hl * SSD_HEAD_DIM, (hl + 1) * SSD_HEAD_DIM))


N_DIR_PARAMS = 9
N_DIR_SCRATCH = 6
SCAN_STAGGER = 6


def _scan_kernel(*refs, zero_init, emit_state, n_alias, nchunks):
    it = iter(refs)
    params = [[next(it) for _ in range(N_DIR_PARAMS)] for _ in range(2)]
    dskip_ref = next(it)
    if not zero_init:
        i_lru, i_ssd, i_gla = next(it), next(it), next(it)
    for _ in range(n_alias):
        next(it)
    o_refs = [next(it), next(it)]
    if emit_state:
        s_lru, s_ssd, s_gla = next(it), next(it), next(it)
    scratch = [[next(it) for _ in range(N_DIR_SCRATCH)] for _ in range(2)]

    c = pl.program_id(1)

    @pl.when(c == 0)
    def _():
        for d in range(2):
            h_scr, ssd_scr, gla_scr = scratch[d][:3]
            if zero_init:
                h_scr[...] = jnp.zeros_like(h_scr)
                ssd_scr[...] = jnp.zeros_like(ssd_scr)
                gla_scr[...] = jnp.zeros_like(gla_scr)
            else:
                h_scr[...] = jnp.broadcast_to(i_lru[d], h_scr.shape)
                for h in range(SSD_HEADS):
                    rs, ls = _ssd_state_slices(h)
                    ssd_scr[rs, ls] = i_ssd[d, h]
                gla_scr[...] = i_gla[d]

    tasks, finals = [], []
    for d in range(2):
        t, f = _direction_tasks(d == 1, *params[d], dskip_ref, o_refs[d], *scratch[d])
        tasks.append(t)
        finals.append(f)
    tasks[1] = [(_delayed(gen, SCAN_STAGGER), n) for gen, n in tasks[1]]
    _round_robin([task for pair in zip(*tasks) for task in pair])
    for f in finals:
        f()

    if emit_state:
        @pl.when(c == nchunks - 1)
        def _():
            for d in range(2):
                h_scr, ssd_scr, gla_scr = scratch[d][:3]
                s_lru[d] = h_scr[0:1, :]
                for h in range(SSD_HEADS):
                    rs, ls = _ssd_state_slices(h)
                    s_ssd[d, h] = ssd_scr[rs, ls]
                s_gla[d] = gla_scr[...]


def _direction_tasks(rev, p_ref, lruw_ref, lruba_ref, lrubx_ref, lam_ref, dtb_ref, alog_ref, w2_ref, gb_ref,
                     dskip_ref, o_ref, h_scr, ssd_scr, gla_scr, la_scr, lb_scr, lh_scr):
    tc = T_CHUNK
    small = p_ref[:, SMALL_OFF:SMALL_OFF + LANES]

    c8h = (-0.5 * LRU_C) * _softplus(-lam_ref[...])
    ntile = tc // SUBLANES

    def lru_group(g):
        sl = slice(g * MXU_DIM, (g + 1) * MXU_DIM)
        xl = p_ref[:, sl]
        pre = _dot(xl.astype(BF16), lruw_ref[g])
        yield
        t_r = jnp.tanh(pre[:, :MXU_DIM] + lruba_ref[:, sl])
        yield
        t_i = jnp.tanh(pre[:, MXU_DIM:] + lrubx_ref[:, sl])
        yield
        log_a = c8h[:, sl] * t_r + c8h[:, sl]
        xlh = 0.5 * xl
        ix = xlh * t_i + xlh
        yield
        a = jnp.exp(log_a)
        th = jnp.tanh(log_a)
        yield
        b = jnp.sqrt(-th * (1.0 + a * a)) * ix
        slabs = range(g * MXU_DIM // LANES, (g + 1) * MXU_DIM // LANES)
        for k, slab in enumerate(slabs):
            la_scr[slab] = a[:, k * LANES:(k + 1) * LANES]
            lb_scr[slab] = b[:, k * LANES:(k + 1) * LANES]
        yield
        nrun = tc // LRU_RUN
        order = range(LRU_RUN - 1, -1, -1) if rev else range(LRU_RUN)
        run_a, run_b = {}, {}
        for k, slab in enumerate(slabs):
            pa = pb = None
            for r in order:
                ar = la_scr[slab, pl.ds(r, nrun, stride=LRU_RUN), :]
                br = lb_scr[slab, pl.ds(r, nrun, stride=LRU_RUN), :]
                if pa is not None:
                    br = ar * pb + br
                    ar = ar * pa
                run_a[k, r], run_b[k, r] = pa, pb = ar, br
        yield
        tot_a = jnp.concatenate([run_a[k, order[-1]] for k in range(len(slabs))], axis=1)
        tot_b = jnp.concatenate([run_b[k, order[-1]] for k in range(len(slabs))], axis=1)
        a_cum, b_cum = yield from _tile_scan(tot_a, tot_b, rev)
        carry_in = h_scr[0:1, sl]
        carry = carry_in
        nt = nrun // SUBLANES
        hs = [None] * nt
        for ti in (range(nt - 1, -1, -1) if rev else range(nt)):
            hs[ti] = a_cum[ti] * carry + b_cum[ti]
            carry = hs[ti][0:1, :] if rev else hs[ti][SUBLANES - 1:SUBLANES, :]
        h_scr[:, sl] = jnp.broadcast_to(carry, (SUBLANES, MXU_DIM))
        yield
        h_out = jnp.stack(hs, axis=0).reshape(nrun, MXU_DIM)
        row = lax.broadcasted_iota(jnp.int32, (nrun, MXU_DIM), 0)
        if rev:
            h_in = jnp.where(row == nrun - 1, carry_in, pltpu.roll(h_out, nrun - 1, 0))
        else:
            h_in = jnp.where(row == 0, carry_in, pltpu.roll(h_out, 1, 0))
        for k, slab in enumerate(slabs):
            hk = h_in[:, k * LANES:(k + 1) * LANES]
            for r in order:
                lh_scr[slab, pl.ds(r, nrun, stride=LRU_RUN), :] = run_a[k, r] * hk + run_b[k, r]
        yield
        for k, slab in enumerate(slabs):
            o_ref[:, g * MXU_DIM + k * LANES:g * MXU_DIM + (k + 1) * LANES] = lh_scr[slab].astype(o_ref.dtype)

    lane = lax.broadcasted_iota(jnp.int32, (1, LANES), 1)
    dt_lo = SMALL_DT + (SSD_HEADS if rev else 0)
    dt_mask = (lane >= dt_lo) & (lane < dt_lo + SSD_HEADS)
    dt = jnp.where(dt_mask, _softplus_log(small + dtb_ref[...]), 0.0)
    la = dt * (-LOG2E * jnp.exp(alog_ref[...]))
    bcum = _seg_cumsum(la, tc, rev)
    bcum_t = bcum.T
    tot = bcum[0:1, :] if rev else bcum[tc - 1:tc, :]

    half = tc // 2
    ri = lax.broadcasted_iota(jnp.int32, (half, half), 0)
    ci = lax.broadcasted_iota(jnp.int32, (half, half), 1)
    tri = (ci >= ri) if rev else (ci <= ri)
    lane_t = lax.broadcasted_iota(jnp.int32, (tc, LANES), 1)
    lo_half = lane_t < SSD_HEAD_DIM
    row_s = lax.broadcasted_iota(jnp.int32, (2 * SSD_STATE, SSD_LANES), 0)
    lane_s = lax.broadcasted_iota(jnp.int32, (1, SSD_LANES), 1) // SSD_HEAD_DIM
    b_off = 2 * D_MODEL
    c_off = b_off + SSD_GROUPS * SSD_STATE
    ssd_upd = {}

    def ssd_group(g):
        pair, gl = divmod(g, 2)
        b_tile = p_ref[:, b_off + pair * LANES:b_off + (pair + 1) * LANES].astype(BF16)
        c_tile = p_ref[:, c_off + pair * LANES:c_off + (pair + 1) * LANES]
        s_pair = ssd_scr[pair * LANES:(pair + 1) * LANES, :]
        gmask = lo_half if gl == 0 else jnp.logical_not(lo_half)
        c_g = jnp.where(gmask, c_tile, 0.0).astype(BF16)
        gmat = _dot_nt(c_g, b_tile)
        inter = _dot(c_g, s_pair.astype(BF16))
        yield
        wx_parts = []
        for hp in range(2):
            h0 = g * 4 + hp * 2
            col0 = bcum[:, dt_lo + h0:dt_lo + h0 + 1]
            col1 = bcum[:, dt_lo + h0 + 1:dt_lo + h0 + 2]
            dtc0 = dt[:, dt_lo + h0:dt_lo + h0 + 1]
            dtc1 = dt[:, dt_lo + h0 + 1:dt_lo + h0 + 2]
            xsl = slice(D_MODEL + h0 * SSD_HEAD_DIM, D_MODEL + (h0 + 2) * SSD_HEAD_DIM)
            xs = p_ref[:, xsl]
            xhat = xs * jnp.where(lo_half, dtc0, dtc1)
            xhat_bf = xhat.astype(BF16)
            yield
            outs = []
            for hh, col in ((0, col0), (1, col1)):
                rowv = bcum_t[dt_lo + h0 + hh:dt_lo + h0 + hh + 1, :]
                outs.append((yield from _ssd_intra(gmat, col, rowv, xhat_bf, tri, rev)))
            o_pair = jnp.where(lo_half, outs[0], outs[1])
            colp = jnp.where(lo_half, col0, col1)
            isl = slice(hp * LANES, (hp + 1) * LANES)
            o_pair = o_pair + jnp.exp2(colp) * inter[:, isl]
            if not rev:
                o_pair = o_pair + dskip_ref[:, h0 * SSD_HEAD_DIM:(h0 + 2) * SSD_HEAD_DIM] * xs
            o_ref[:, xsl] = o_pair.astype(o_ref.dtype)
            yield
            t0 = tot[:, dt_lo + h0:dt_lo + h0 + 1]
            t1 = tot[:, dt_lo + h0 + 1:dt_lo + h0 + 2]
            totp = jnp.where(lane < SSD_HEAD_DIM, t0, t1)
            wx_parts.append((jnp.exp2(totp - colp) * xhat).astype(BF16))
        wx = jnp.concatenate(wx_parts, axis=1)
        ssd_upd[g] = _dot_tn(b_tile, wx)
        yield
        d = jnp.zeros((1, SSD_LANES), F32)
        for hl in range(4):
            th = tot[:, dt_lo + g * 4 + hl:dt_lo + g * 4 + hl + 1]
            d = jnp.where(lane_s == hl, jnp.exp2(th), d)
        ssd_upd[("dec", g)] = d
        if gl == 1:
            first = row_s < SSD_STATE
            new = jnp.where(first, ssd_upd[g - 1], ssd_upd[g])
            dec = jnp.where(first, ssd_upd[("dec", g - 1)], d)
            ssd_scr[pair * LANES:(pair + 1) * LANES, :] = dec * s_pair + new

    small_bf = small.astype(BF16)
    gla_bg = {}

    def gla_log_decay(hd):
        ksl = slice(hd * GLA_HEAD_K, (hd + 1) * GLA_HEAD_K)
        z = _dot(small_bf, w2_ref[:, ksl]) + gb_ref[:, ksl]
        yield
        ld = -_softplus_log(-z) * (LOG2E / GLA_GATE_NORM)
        yield
        gla_bg[hd] = _seg_cumsum(ld, GLA_SUB, rev)
        yield

    nsub = tc // GLA_SUB
    nblk = GLA_SUB // GLA_BLK
    scale = GLA_HEAD_K ** -0.5
    gla_st = [gla_scr[hd * GLA_HEAD_K:(hd + 1) * GLA_HEAD_K, :] for hd in range(GLA_HEADS)]

    def gla_subchunk(sc_i, hd):
        ksl = slice(hd * GLA_HEAD_K, (hd + 1) * GLA_HEAD_K)
        st = gla_st[hd]
        rsl = slice(sc_i * GLA_SUB, (sc_i + 1) * GLA_SUB)
        q = p_ref[rsl, Q_OFF + hd * GLA_HEAD_K:Q_OFF + (hd + 1) * GLA_HEAD_K] * scale
        k = p_ref[rsl, K_OFF + hd * GLA_HEAD_K:K_OFF + (hd + 1) * GLA_HEAD_K]
        v = p_ref[rsl, V_OFF + hd * GLA_HEAD_V:V_OFF + (hd + 1) * GLA_HEAD_V].astype(BF16)
        b = gla_bg[hd][rsl, :]
        inter = _dot((q * jnp.exp2(b)).astype(BF16), st.astype(BF16))
        yield
        o_rows = []
        for blk in range(nblk):
            bsl = slice(blk * GLA_BLK, (blk + 1) * GLA_BLK)
            if rev:
                keys = slice(blk * GLA_BLK, GLA_SUB)
                ref_row = b[(blk + 1) * GLA_BLK - 1:(blk + 1) * GLA_BLK, :]
            else:
                keys = slice(0, (blk + 1) * GLA_BLK)
                ref_row = b[blk * GLA_BLK:blk * GLA_BLK + 1, :]
            nk = keys.stop - keys.start
            qt = (q[bsl, :] * jnp.exp2(b[bsl, :] - ref_row)).astype(BF16)
            kt = (k[keys, :] * jnp.exp2(ref_row - b[keys, :])).astype(BF16)
            a_blk = _dot_nt(qt, kt)
            yield
            ai = lax.broadcasted_iota(jnp.int32, (GLA_BLK, nk), 0)
            aj = lax.broadcasted_iota(jnp.int32, (GLA_BLK, nk), 1)
            vis = (aj >= ai) if rev else (aj <= ai + blk * GLA_BLK)
            a_blk = jnp.where(vis, a_blk, 0.0).astype(BF16)
            o_rows.append(inter[bsl, :] + _dot(a_blk, v[keys, :]))
            yield
        o = jnp.concatenate(o_rows, axis=0)
        o_ref[rsl, 2 * D_MODEL + hd * GLA_HEAD_V:2 * D_MODEL + (hd + 1) * GLA_HEAD_V] = o.astype(o_ref.dtype)
        end = b[0:1, :] if rev else b[GLA_SUB - 1:GLA_SUB, :]
        k_dec = (k * jnp.exp2(end - b)).astype(BF16)
        dec_col = jnp.broadcast_to(jnp.exp2(end), (GLA_HEAD_K, GLA_HEAD_K)).T[:, 0:1]
        gla_st[hd] = dec_col * st + _dot_tn(k_dec, v)
        yield

    def gla_head(hd):
        yield from gla_log_decay(hd)
        for step in range(nsub):
            yield from gla_subchunk((nsub - 1 - step) if rev else step, hd)

    def lru_all():
        for g in range(D_MODEL // MXU_DIM):
            yield from lru_group(g)

    def ssd_all():
        for g in range(SSD_GROUPS):
            yield from ssd_group(g)

    tasks = [(gla_head(hd), 1) for hd in range(GLA_HEADS)] + [(ssd_all(), 2), (lru_all(), 1)]

    def finalize():
        for hd in range(GLA_HEADS):
            gla_scr[hd * GLA_HEAD_K:(hd + 1) * GLA_HEAD_K, :] = gla_st[hd]

    return tasks, finalize


def _scan(p1, tok0, nseq, nchunks, lw, l, init, o_prev, st_prev, st_shapes):
    tc = T_CHUNK
    blk0 = tok0 // tc
    m = p1.shape[0]

    def row_block(d):
        return lambda b, c: (blk0 + b * nchunks + ((nchunks - 1 - c) if d else c), 0)

    const2 = lambda b, c: (0, 0)
    const3 = lambda b, c: (0, 0, 0)
    in_specs, args = [], []
    for d in range(2):
        in_specs += [
            pl.BlockSpec((tc, P1_COLS), row_block(d)),
            pl.BlockSpec((D_MODEL // MXU_DIM, MXU_DIM, 2 * MXU_DIM), const3),
            pl.BlockSpec((1, D_MODEL), const2),
            pl.BlockSpec((1, D_MODEL), const2),
            pl.BlockSpec((1, D_MODEL), const2),
            pl.BlockSpec((1, LANES), const2),
            pl.BlockSpec((1, LANES), const2),
            pl.BlockSpec((LANES, GLA_DK), const2),
            pl.BlockSpec((1, GLA_DK), const2),
        ]
        args += [p1, lw["lru_w"][d], lw["lru_ba"][d], lw["lru_bx"][d], lw["lam"][d], lw["dt_bias"][d],
                 lw["a_log"][d], lw["gate_w2"][d], lw["gate_b"][d]]
        assert len(in_specs) == (d + 1) * N_DIR_PARAMS
    in_specs.append(pl.BlockSpec((1, D_MODEL), const2))
    args.append(lw["d_skip"])
    state_specs = [
        pl.BlockSpec((None, None, 2, 1, D_MODEL), lambda b, c: (b, l, 0, 0, 0)),
        pl.BlockSpec((None, None, 2, SSD_HEADS, SSD_STATE, SSD_HEAD_DIM), lambda b, c: (b, l, 0, 0, 0, 0)),
        pl.BlockSpec((None, None, 2, GLA_ROWS, GLA_HEAD_V), lambda b, c: (b, l, 0, 0, 0)),
    ]
    if init is not None:
        in_specs += state_specs
        args += list(init)
    aliases = {}
    prev = list(o_prev or []) + list(st_prev or [])
    first_out = 0 if o_prev is not None else 2
    for k, arr in enumerate(prev):
        aliases[len(args)] = first_out + k
        in_specs.append(pl.BlockSpec(memory_space=pl.ANY))
        args.append(arr)
    out_specs = [pl.BlockSpec((tc, 3 * D_MODEL), row_block(d)) for d in range(2)]
    out_shape = [jax.ShapeDtypeStruct((m, 3 * D_MODEL), BF16)] * 2
    if st_shapes is not None:
        out_specs += state_specs
        out_shape += [jax.ShapeDtypeStruct(s, F32) for s in st_shapes]
    kern = functools.partial(_scan_kernel, zero_init=init is None, emit_state=st_shapes is not None,
                             n_alias=len(prev), nchunks=nchunks)
    dir_scratch = [
        pltpu.VMEM((SUBLANES, D_MODEL), F32),
        pltpu.VMEM((SSD_ROWS, SSD_LANES), F32),
        pltpu.VMEM((GLA_ROWS, GLA_HEAD_V), F32),
        pltpu.VMEM((D_MODEL // LANES, tc, LANES), F32),
        pltpu.VMEM((D_MODEL // LANES, tc, LANES), F32),
        pltpu.VMEM((D_MODEL // LANES, tc, LANES), F32),
    ]
    assert len(dir_scratch) == N_DIR_SCRATCH
    return pl.pallas_call(
        kern,
        grid=(nseq, nchunks),
        in_specs=in_specs,
        out_specs=out_specs,
        out_shape=out_shape,
        input_output_aliases=aliases,
        scratch_shapes=dir_scratch * 2,
        compiler_params=_params(("parallel", "arbitrary")),
        name="scan",
    )(*args)


def _post_kernel(x_ref, of_ref, ob_ref, p2_ref, mod_ref, ssdg_ref, glag_ref,
                 wl_ref, ws_ref, wg_ref, wo_ref, o_ref):
    dm = D_MODEL

    def both(sl):
        return of_ref[:, sl].astype(F32) + ob_ref[:, sl].astype(F32)

    def p2(k):
        return p2_ref[:, k * dm:(k + 1) * dm].astype(F32)

    y_lru = both(slice(0, dm)) * _gelu_tanh(p2(0))
    y_ssd = _rms(both(slice(dm, 2 * dm)) * _silu(p2(1)), ssdg_ref[...])
    parts = []
    for hd in range(GLA_HEADS):
        sl = slice(2 * dm + hd * GLA_HEAD_V, 2 * dm + (hd + 1) * GLA_HEAD_V)
        parts.append(_rms(both(sl), glag_ref[...]))
    y_gla = jnp.concatenate(parts, axis=1) * _silu(p2(2))
    merged = (_sigmoid(p2(3)) * _dot(y_lru.astype(BF16), wl_ref[...])
              + _sigmoid(p2(4)) * _dot(y_ssd.astype(BF16), ws_ref[...])
              + _sigmoid(p2(5)) * _dot(y_gla.astype(BF16), wg_ref[...]))
    out = _dot(merged.astype(BF16), wo_ref[...])
    o_ref[...] = x_ref[...] + mod_ref[:, 2 * dm:3 * dm] * out


def _post(x, o_f, o_b, p2, mod, row_of_block, lw, l):
    m = x.shape[0]
    tm = TM_POST
    wspec = pl.BlockSpec((None, D_MODEL, D_MODEL), lambda i: (l, 0, 0), pipeline_mode=pl.Buffered(1))
    return pl.pallas_call(
        _post_kernel,
        grid=(m // tm,),
        in_specs=[
            pl.BlockSpec((tm, D_MODEL), lambda i: (i, 0)),
            pl.BlockSpec((tm, 3 * D_MODEL), lambda i: (i, 0)),
            pl.BlockSpec((tm, 3 * D_MODEL), lambda i: (i, 0)),
            pl.BlockSpec((tm, P2_COLS), lambda i: (i, 0)),
            pl.BlockSpec((None, 1, 6 * D_MODEL), lambda i: (row_of_block(i, tm), 0, 0)),
            pl.BlockSpec((1, D_MODEL), lambda i: (0, 0)),
            pl.BlockSpec((1, GLA_HEAD_V), lambda i: (0, 0)),
            wspec, wspec, wspec, wspec,
        ],
        out_specs=pl.BlockSpec((tm, D_MODEL), lambda i: (i, 0)),
        out_shape=jax.ShapeDtypeStruct((m, D_MODEL), F32),
        compiler_params=_params(("parallel",)),
        name="post",
    )(x, o_f, o_b, p2, mod, lw["ssd_norm_g"], lw["gla_norm_g"],
      lw["w_br_lru"], lw["w_br_ssd"], lw["w_br_gla"], lw["w_out"])


def _ffn_kernel(x_ref, mod_ref, g_ref, wi_ref, wo_ref, fg_ref, o_ref, *, final_norm, nsplit):
    dm = D_MODEL
    x = x_ref[...]
    h = (_rms(x, g_ref[...]) * (1.0 + mod_ref[:, 4 * dm:5 * dm]) + mod_ref[:, 3 * dm:4 * dm]).astype(BF16)
    tf = D_FF // nsplit
    acc = jnp.zeros(x.shape, F32)
    for s in range(nsplit):
        fs = slice(s * tf, (s + 1) * tf)
        gate = _dot(h, wi_ref[:, fs])
        up = _dot(h, wi_ref[:, D_FF + s * tf:D_FF + (s + 1) * tf])
        acc = acc + _dot((_silu(gate) * up).astype(BF16), wo_ref[fs, :])
    y = x + mod_ref[:, 5 * dm:6 * dm] * acc
    if final_norm:
        y = _rms(y, fg_ref[...])
    o_ref[...] = y


def _ffn(x, mod, row_of_block, lw, l, final_g, final_norm):
    m = x.shape[0]
    tm = TM_FFN
    single = dict(pipeline_mode=pl.Buffered(1))
    kern = functools.partial(_ffn_kernel, final_norm=final_norm, nsplit=2)
    return pl.pallas_call(
        kern,
        grid=(m // tm,),
        in_specs=[
            pl.BlockSpec((tm, D_MODEL), lambda i: (i, 0)),
            pl.BlockSpec((None, 1, 6 * D_MODEL), lambda i: (row_of_block(i, tm), 0, 0)),
            pl.BlockSpec((1, D_MODEL), lambda i: (0, 0)),
            pl.BlockSpec((None, D_MODEL, 2 * D_FF), lambda i: (l, 0, 0), **single),
            pl.BlockSpec((None, D_FF, D_MODEL), lambda i: (l, 0, 0), **single),
            pl.BlockSpec((1, D_MODEL), lambda i: (0, 0)),
        ],
        out_specs=pl.BlockSpec((tm, D_MODEL), lambda i: (i, 0)),
        out_shape=jax.ShapeDtypeStruct((m, D_MODEL), F32),
        compiler_params=_params(("parallel",)),
        name="ffn",
    )(x, mod, lw["norm_ffn_g"], lw["w_ffn_in"], lw["w_ffn_out"], final_g)


def _snake_rows(src_ref, dst_ref):
    rows = src_ref.shape[0]
    for g0 in range(0, rows, 2 * GRID_W):
        dst_ref[g0:g0 + GRID_W, :] = src_ref[g0:g0 + GRID_W, :]
        for r in range(GRID_W):
            dst_ref[pl.ds(g0 + GRID_W + r, 1), :] = src_ref[pl.ds(g0 + 2 * GRID_W - 1 - r, 1), :]


def _assemble_kernel(xa_ref, xb_ref, o_ref, *, nb_ctx):
    i = pl.program_id(0)

    @pl.when(i < nb_ctx)
    def _():
        o_ref[...] = xa_ref[...]

    @pl.when(i >= nb_ctx)
    def _():
        _snake_rows(xb_ref, o_ref)


def _split_kernel(y_ref, ya_ref, yb_ref, *, nb_ctx):
    i = pl.program_id(0)

    @pl.when(i < nb_ctx)
    def _():
        ya_ref[...] = y_ref[...]

    @pl.when(i >= nb_ctx)
    def _():
        _snake_rows(y_ref, yb_ref)


def _assemble_tokens(x_ctx, x_lat):
    tr = TOKEN_ROWS
    nb_ctx, nb_lat = x_ctx.shape[0] // tr, x_lat.shape[0] // tr
    return pl.pallas_call(
        functools.partial(_assemble_kernel, nb_ctx=nb_ctx),
        grid=(nb_ctx + nb_lat,),
        in_specs=[pl.BlockSpec((tr, D_MODEL), lambda i: (jnp.minimum(i, nb_ctx - 1), 0)),
                  pl.BlockSpec((tr, D_MODEL), lambda i: (jnp.maximum(i - nb_ctx, 0), 0))],
        out_specs=pl.BlockSpec((tr, D_MODEL), lambda i: (i, 0)),
        out_shape=jax.ShapeDtypeStruct((x_ctx.shape[0] + x_lat.shape[0], D_MODEL), x_ctx.dtype),
        compiler_params=_params(("arbitrary",)),
        name="assemble_tokens",
    )(x_ctx, x_lat)


def _split_tokens(y, m_ctx):
    tr = TOKEN_ROWS
    nb_ctx, nb_lat = m_ctx // tr, (y.shape[0] - m_ctx) // tr
    return pl.pallas_call(
        functools.partial(_split_kernel, nb_ctx=nb_ctx),
        grid=(nb_ctx + nb_lat,),
        in_specs=[pl.BlockSpec((tr, D_MODEL), lambda i: (i, 0))],
        out_specs=[pl.BlockSpec((tr, D_MODEL), lambda i: (jnp.minimum(i, nb_ctx - 1), 0)),
                   pl.BlockSpec((tr, D_MODEL), lambda i: (jnp.maximum(i - nb_ctx, 0), 0))],
        out_shape=[jax.ShapeDtypeStruct((m_ctx, D_MODEL), y.dtype),
                   jax.ShapeDtypeStruct((y.shape[0] - m_ctx, D_MODEL), y.dtype)],
        compiler_params=_params(("arbitrary",)),
        name="split_tokens",
    )(y)


_O_GATE, _O_Z, _O_XBC = D_MODEL, 2 * D_MODEL, 3 * D_MODEL
_O_DT = _O_XBC + CONV_COLS - D_MODEL
_O_Q = _O_DT + 2 * SSD_HEADS
_O_G = _O_Q + 2 * GLA_DK + D_MODEL
_O_LR = _O_G + D_MODEL
_O_M = _O_LR + 2 * GLA_GATE_RANK
IN_WIDTH = _O_M + 3 * D_MODEL
W_PREP_ROWS = 128


def _wprep_kernel(w_ref, w1_ref, w2_ref):
    def cols(lo, hi):
        return w_ref[:, lo:hi].astype(BF16)

    w1_ref[:, 0:D_MODEL] = cols(0, D_MODEL)
    w1_ref[:, D_MODEL:CONV_COLS] = cols(_O_XBC, _O_DT)
    w1_ref[:, Q_OFF:SMALL_OFF] = cols(_O_Q, _O_G)
    assert _O_DT % LANES == SMALL_DT and _O_LR % LANES == SMALL_LR
    lane = lax.broadcasted_iota(jnp.int32, (w_ref.shape[0], LANES), 1)
    dt_tile = w_ref[:, _O_DT:_O_DT + LANES]
    lr_tile = w_ref[:, _O_LR - SMALL_LR:_O_LR - SMALL_LR + LANES]
    small = jnp.where(lane < SMALL_LR, dt_tile, jnp.where(lane < 2 * SMALL_LR, lr_tile, 0.0))
    w1_ref[:, SMALL_OFF:] = small.astype(BF16)
    w2_ref[:, 0:2 * D_MODEL] = cols(_O_GATE, _O_XBC)
    w2_ref[:, 2 * D_MODEL:3 * D_MODEL] = cols(_O_G, _O_LR)
    w2_ref[:, 3 * D_MODEL:] = cols(_O_M, IN_WIDTH)


def _prep_w_in(w_in):
    n_layers = w_in.shape[0]
    tr = W_PREP_ROWS
    return pl.pallas_call(
        _wprep_kernel,
        grid=(n_layers, D_MODEL // tr),
        in_specs=[pl.BlockSpec((None, tr, IN_WIDTH), lambda l, r: (l, r, 0))],
        out_specs=[pl.BlockSpec((None, tr, P1_COLS), lambda l, r: (l, r, 0)),
                   pl.BlockSpec((None, tr, P2_COLS), lambda l, r: (l, r, 0))],
        out_shape=[jax.ShapeDtypeStruct((n_layers, D_MODEL, P1_COLS), BF16),
                   jax.ShapeDtypeStruct((n_layers, D_MODEL, P2_COLS), BF16)],
        compiler_params=_params(("parallel", "parallel")),
        name="w_in_relayout",
    )(w_in)


def _layer_weights(l, w):
    dm = D_MODEL

    def block_diag_tiles(wa, wx):
        per = MXU_DIM // LRU_BLOCK_W
        eye = jnp.eye(per, dtype=wa.dtype)
        def tiles(wb):
            wb = wb.reshape(dm // MXU_DIM, per, LRU_BLOCK_W, LRU_BLOCK_W)
            t = jnp.einsum("gpwv,pq->gpwqv", wb, eye)
            return t.reshape(dm // MXU_DIM, MXU_DIM, MXU_DIM)
        return jnp.concatenate([tiles(wa), tiles(wx)], axis=2).astype(BF16)

    def lane_pad(v, lo):
        return jnp.zeros((1, LANES), F32).at[0, lo:lo + v.shape[0]].set(v)

    def gate_w2(d):
        lo = SMALL_LR + d * GLA_GATE_RANK
        return jnp.zeros((LANES, GLA_DK), F32).at[lo:lo + GLA_GATE_RANK].set(w["gla_gate_w2"][l, d]).astype(BF16)

    return dict(
        w1=w["w1"], w2=w["w2"],
        norm_mix_g=w["norm_mix_g"][l][None], norm_ffn_g=w["norm_ffn_g"][l][None],
        conv_w=jnp.concatenate([w["lru_conv_w"][l], w["ssd_conv_w"][l]], axis=1),
        conv_b=jnp.concatenate([w["lru_conv_b"][l], w["ssd_conv_b"][l]])[None],
        lru_w=[block_diag_tiles(0.5 * w["lru_w_a"][l, d], 0.5 * w["lru_w_x"][l, d]) for d in range(2)],
        lru_ba=[0.5 * w["lru_b_a"][l, d][None] for d in range(2)],
        lru_bx=[0.5 * w["lru_b_x"][l, d][None] for d in range(2)],
        lam=[w["lru_lambda"][l, d][None] for d in range(2)],
        dt_bias=[lane_pad(w["ssd_dt_bias"][l, d], SMALL_DT + d * SSD_HEADS) for d in range(2)],
        a_log=[lane_pad(w["ssd_a_log"][l, d], SMALL_DT + d * SSD_HEADS) for d in range(2)],
        d_skip=jnp.repeat(w["ssd_d"][l], SSD_HEAD_DIM)[None],
        gate_w2=[gate_w2(d) for d in range(2)],
        gate_b=[w["gla_gate_b"][l, d][None] for d in range(2)],
        ssd_norm_g=w["ssd_norm_g"][l][None], gla_norm_g=w["gla_norm_g"][l][None],
        w_br_lru=w["w_br_lru_bf"], w_br_ssd=w["w_br_ssd_bf"], w_br_gla=w["w_br_gla_bf"], w_out=w["w_out_bf"],
        w_ffn_in=w["w_ffn_in_bf"], w_ffn_out=w["w_ffn_out_bf"],
    )


def kernel(x_prompt, x_sample, state_lru, state_ssd, state_gla, c, c_ctx, norm_mix_g, norm_ffn_g, w_ada, b_ada, w_in, lru_conv_w, lru_conv_b, lru_w_a, lru_b_a, lru_w_x, lru_b_x, lru_lambda, ssd_conv_w, ssd_conv_b, ssd_dt_bias, ssd_a_log, ssd_d, ssd_norm_g, gla_gate_w2, gla_gate_b, gla_norm_g, w_br_lru, w_br_ssd, w_br_gla, w_out, w_ffn_in, w_ffn_out, final_norm_g):
    w = dict(norm_mix_g=norm_mix_g, norm_ffn_g=norm_ffn_g, w_in=w_in, lru_conv_w=lru_conv_w,
             lru_conv_b=lru_conv_b, lru_w_a=lru_w_a, lru_b_a=lru_b_a, lru_w_x=lru_w_x, lru_b_x=lru_b_x,
             lru_lambda=lru_lambda, ssd_conv_w=ssd_conv_w, ssd_conv_b=ssd_conv_b, ssd_dt_bias=ssd_dt_bias,
             ssd_a_log=ssd_a_log, ssd_d=ssd_d, ssd_norm_g=ssd_norm_g, gla_gate_w2=gla_gate_w2,
             gla_gate_b=gla_gate_b, gla_norm_g=gla_norm_g, w_br_lru=w_br_lru, w_br_ssd=w_br_ssd,
             w_br_gla=w_br_gla, w_out=w_out, w_ffn_in=w_ffn_in, w_ffn_out=w_ffn_out)
    w["w1"], w["w2"] = _prep_w_in(w_in)
    for name in ("w_br_lru", "w_br_ssd", "w_br_gla", "w_out", "w_ffn_in", "w_ffn_out"):
        w[name + "_bf"] = w[name].astype(BF16)
    n_layers = w_in.shape[0]
    b_ctx, t_ctx, dm = x_prompt.shape
    b_lat, t_lat, _ = x_sample.shape
    m_ctx = b_ctx * t_ctx
    m_lat = b_lat * t_lat
    nch_ctx = t_ctx // T_CHUNK
    nch_lat = t_lat // T_CHUNK

    n_rows = -(-(1 + b_lat) // SUBLANES) * SUBLANES
    cond = jnp.zeros((n_rows, dm), F32).at[0].set(c_ctx).at[1:1 + b_lat].set(c)
    mod = _modulation(cond, w_ada, b_ada)

    def row_of_block(i, tm):
        tok = i * tm
        return jnp.where(tok < m_ctx, 0, 1 + (tok - m_ctx) // t_lat)

    def seq_pos(tok):
        in_ctx = tok < m_ctx
        pos = jnp.where(in_ctx, lax.rem(tok, t_ctx), lax.rem(tok - m_ctx, t_lat))
        return pos, jnp.where(in_ctx, t_ctx, t_lat)

    x = _assemble_tokens(x_prompt.reshape(m_ctx, dm), x_sample.reshape(m_lat, dm))

    st_shapes = [(b_ctx, n_layers, 2, 1, dm),
                 (b_ctx, n_layers, 2, SSD_HEADS, SSD_STATE, SSD_HEAD_DIM),
                 (b_ctx, n_layers, 2, GLA_ROWS, GLA_HEAD_V)]
    init = (state_lru.reshape(b_lat, n_layers, 2, 1, dm), state_ssd,
            state_gla.reshape(b_lat, n_layers, 2, GLA_ROWS, GLA_HEAD_V))
    states = None
    for l in range(n_layers):
        lw = _layer_weights(l, w)
        mod_l = mod[l][:, None, :]
        p1, p2 = _inproj(x, mod_l, row_of_block, seq_pos, lw, l)
        o_f, o_b, *states = _scan(p1, 0, b_ctx, nch_ctx, lw, l, None, None, states, st_shapes)
        outs = _scan(p1, m_ctx, b_lat, nch_lat, lw, l, init, [o_f, o_b], None, None)
        x = _post(x, outs[0], outs[1], p2, mod_l, row_of_block, lw, l)
        x = _ffn(x, mod_l, row_of_block, lw, l, final_norm_g[None], l == n_layers - 1)

    y_ctx, y_lat = _split_tokens(x, m_ctx)
    y_prompt = y_ctx.reshape(b_ctx, t_ctx, dm)
    y_sample = y_lat.reshape(b_lat, t_lat, dm)
    return (y_prompt, y_sample, states[0].reshape(b_ctx, n_layers, 2, dm), states[1],
            states[2].reshape(b_ctx, n_layers, 2, GLA_HEADS, GLA_HEAD_K, GLA_HEAD_V))
```

```python
import functools

import jax
import jax.numpy as jnp
from jax import lax
from jax.experimental import pallas as pl
from jax.experimental.pallas import tpu as pltpu

F32 = jnp.float32
BF16 = jnp.bfloat16

D_MODEL = 1024
GRID_W = 64
CONV_W = 4
CONV_LEFT = 2
LRU_BLOCK_W = 64
LRU_C = 8.0
SSD_HEAD_DIM = 64
SSD_HEADS = 16
SSD_GROUPS = 4
SSD_STATE = 64
GLA_HEADS = 4
GLA_HEAD_K = 128
GLA_HEAD_V = 256
GLA_DK = GLA_HEADS * GLA_HEAD_K
GLA_GATE_RANK = 16
GLA_GATE_NORM = 16.0
D_FF = 2816
EPS = 1e-6

LANES = 128
SUBLANES = 8
MXU_DIM = 256
VMEM_LIMIT_BYTES = 60 * 1024 * 1024

CONV_COLS = D_MODEL + D_MODEL + 2 * SSD_GROUPS * SSD_STATE
Q_OFF = CONV_COLS
K_OFF = Q_OFF + GLA_DK
V_OFF = K_OFF + GLA_DK
SMALL_OFF = V_OFF + D_MODEL
P1_COLS = SMALL_OFF + LANES
P2_COLS = 6 * D_MODEL
SMALL_DT = 0
SMALL_LR = 2 * SSD_HEADS

T_CHUNK = 256
IN_CHUNK = 256
TM_POST = 512
TM_FFN = 512
TOKEN_ROWS = 1024
LRU_RUN = 4
GLA_SUB = 64
GLA_BLK = 16
NEG_BIG = -1e30
LOG2E = 1.4426950408889634
SSD_ROWS = SSD_GROUPS * SSD_STATE
SSD_LANES = (SSD_HEADS // SSD_GROUPS) * SSD_HEAD_DIM
GLA_ROWS = GLA_HEADS * GLA_HEAD_K


def _params(sem, **kw):
    return pltpu.CompilerParams(dimension_semantics=sem, vmem_limit_bytes=VMEM_LIMIT_BYTES, **kw)


def _softplus(x):
    return jnp.maximum(x, 0.0) + jnp.log1p(jnp.exp(-jnp.abs(x)))


def _softplus_log(x):
    return jnp.maximum(x, 0.0) + jnp.log(1.0 + jnp.exp(-jnp.abs(x)))


def _sigmoid(x):
    return 0.5 * (1.0 + jnp.tanh(0.5 * x))


def _silu(x):
    return x * _sigmoid(x)


def _gelu_tanh(x):
    c = 0.7978845608028654
    return 0.5 * x * (1.0 + jnp.tanh(c * (x + 0.044715 * (x * x * x))))


def _rms(x, g):
    return x * lax.rsqrt(jnp.mean(x * x, axis=-1, keepdims=True) + EPS) * g


def _dot(a, b):
    return jnp.dot(a, b, preferred_element_type=F32)


def _dot_nt(a, b):
    return lax.dot_general(a, b, (((1,), (1,)), ((), ())), preferred_element_type=F32)


def _dot_tn(a, b):
    return lax.dot_general(a, b, (((0,), (0,)), ((), ())), preferred_element_type=F32)


def _mod_kernel(c_ref, w_ref, b_ref, o_ref):
    c = _silu(c_ref[...]).astype(BF16)
    o_ref[...] = _dot(c, w_ref[...].astype(BF16)) + b_ref[...]


def _modulation(cond, w_ada, b_ada):
    n_layers = w_ada.shape[0]
    rows = cond.shape[0]
    tn = D_MODEL
    return pl.pallas_call(
        _mod_kernel,
        grid=(n_layers, 6 * D_MODEL // tn),
        in_specs=[
            pl.BlockSpec((rows, D_MODEL), lambda l, j: (0, 0)),
            pl.BlockSpec((None, D_MODEL, tn), lambda l, j: (l, 0, j)),
            pl.BlockSpec((None, 1, tn), lambda l, j: (l, 0, j)),
        ],
        out_specs=pl.BlockSpec((None, rows, tn), lambda l, j: (l, 0, j)),
        out_shape=jax.ShapeDtypeStruct((n_layers, rows, 6 * D_MODEL), F32),
        compiler_params=_params(("parallel", "parallel")),
        name="modulation",
    )(cond, w_ada, b_ada.reshape(n_layers, 1, 6 * D_MODEL))


def _inproj_kernel(x_ref, xp_ref, xn_ref, mod_ref, g_ref, w1_ref, w2_ref, cw_ref, cb_ref,
                   p1_ref, p2_ref, hext, cbuf, obuf, *, seq_pos):
    tc = T_CHUNK
    sh = mod_ref[:, 0:D_MODEL]
    sc1 = 1.0 + mod_ref[:, D_MODEL:2 * D_MODEL]
    g = g_ref[...]
    h_main = _rms(x_ref[...], g) * sc1 + sh
    hext[SUBLANES:SUBLANES + tc, :] = h_main
    pos, seq_len = seq_pos(pl.program_id(0) * tc)
    head = jnp.where(pos == 0, 0.0, 1.0)
    tail = jnp.where(pos + tc == seq_len, 0.0, 1.0)
    hext[0:SUBLANES, :] = (_rms(xp_ref[...], g) * sc1 + sh) * head
    hext[SUBLANES + tc:, :] = (_rms(xn_ref[...], g) * sc1 + sh) * tail
    he = hext[...].astype(BF16)
    hb = h_main.astype(BF16)

    def conv_columns():
        half = tc // 2
        for c0 in range(0, CONV_COLS, IN_CHUNK):
            res = _dot(he, w1_ref[:, c0:c0 + IN_CHUNK])
            for k in range(IN_CHUNK // LANES):
                cbuf[c0 // LANES + k] = res[:, k * LANES:(k + 1) * LANES]
            yield
            for k in range(IN_CHUNK // LANES):
                slab = c0 // LANES + k
                ls = slice(slab * LANES, (slab + 1) * LANES)
                for parity in range(2):
                    xc = cb_ref[:, ls]
                    for j in range(CONV_W):
                        row0 = SUBLANES - CONV_LEFT + j + parity
                        xc = xc + cw_ref[j:j + 1, ls] * cbuf[slab, pl.ds(row0, half, stride=2), :]
                    obuf[slab, pl.ds(parity, half, stride=2), :] = xc if c0 < D_MODEL else _silu(xc)
                if k % 2 == 1:
                    yield
            for k in range(IN_CHUNK // LANES):
                slab = c0 // LANES + k
                p1_ref[:, slab * LANES:(slab + 1) * LANES] = obuf[slab]
            yield

    def plain_columns():
        for c0 in range(CONV_COLS, P1_COLS, IN_CHUNK):
            cs = slice(c0, min(c0 + IN_CHUNK, P1_COLS))
            p1_ref[:, cs] = _dot(hb, w1_ref[:, cs])
            yield
        for c0 in range(0, P2_COLS, IN_CHUNK):
            cs = slice(c0, c0 + IN_CHUNK)
            p2_ref[:, cs] = _dot(hb, w2_ref[:, cs]).astype(p2_ref.dtype)
            yield

    _round_robin([(conv_columns(), 1), (plain_columns(), 1)])


def _inproj(x, mod, row_of_block, seq_pos, lw, l):
    m = x.shape[0]
    tc = T_CHUNK
    rows8 = tc // SUBLANES
    last8 = m // SUBLANES - 1
    single = dict(pipeline_mode=pl.Buffered(1))
    return pl.pallas_call(
        functools.partial(_inproj_kernel, seq_pos=seq_pos),
        grid=(m // tc,),
        in_specs=[
            pl.BlockSpec((tc, D_MODEL), lambda i: (i, 0)),
            pl.BlockSpec((SUBLANES, D_MODEL), lambda i: (jnp.maximum(i * rows8 - 1, 0), 0)),
            pl.BlockSpec((SUBLANES, D_MODEL), lambda i: (jnp.minimum((i + 1) * rows8, last8), 0)),
            pl.BlockSpec((None, 1, 6 * D_MODEL), lambda i: (row_of_block(i, tc), 0, 0)),
            pl.BlockSpec((1, D_MODEL), lambda i: (0, 0)),
            pl.BlockSpec((None, D_MODEL, P1_COLS), lambda i: (l, 0, 0), **single),
            pl.BlockSpec((None, D_MODEL, P2_COLS), lambda i: (l, 0, 0), **single),
            pl.BlockSpec((CONV_W, CONV_COLS), lambda i: (0, 0)),
            pl.BlockSpec((1, CONV_COLS), lambda i: (0, 0)),
        ],
        out_specs=[pl.BlockSpec((tc, P1_COLS), lambda i: (i, 0)),
                   pl.BlockSpec((tc, P2_COLS), lambda i: (i, 0))],
        out_shape=[jax.ShapeDtypeStruct((m, P1_COLS), F32), jax.ShapeDtypeStruct((m, P2_COLS), BF16)],
        scratch_shapes=[pltpu.VMEM((tc + 2 * SUBLANES, D_MODEL), F32),
                        pltpu.VMEM((CONV_COLS // LANES, tc + 2 * SUBLANES, LANES), F32),
                        pltpu.VMEM((CONV_COLS // LANES, tc, LANES), F32)],
        compiler_params=_params(("parallel",)),
        name="inproj",
    )(x, x, x, mod, lw["norm_mix_g"], lw["w1"], lw["w2"], lw["conv_w"], lw["conv_b"])


def _tile_scan(a, b, rev):
    t, w = a.shape
    a = a.reshape(t // SUBLANES, SUBLANES, w)
    b = b.reshape(t // SUBLANES, SUBLANES, w)
    pos = lax.broadcasted_iota(jnp.int32, (1, SUBLANES, w), 1)
    s = 1
    while s < SUBLANES:
        shift = (SUBLANES - s) if rev else s
        valid = (pos < SUBLANES - s) if rev else (pos >= s)
        a_sh = jnp.where(valid, pltpu.roll(a, shift, 1), 1.0)
        b_sh = jnp.where(valid, pltpu.roll(b, shift, 1), 0.0)
        yield
        b = a * b_sh + b
        a = a * a_sh
        yield
        s *= 2
    return a, b


def _delayed(gen, turns):
    for _ in range(turns):
        yield
    yield from gen


def _round_robin(tasks):
    tasks = list(tasks)
    while tasks:
        for task in list(tasks):
            gen, stages = task
            try:
                for _ in range(stages):
                    next(gen)
            except StopIteration:
                tasks.remove(task)


def _seg_cumsum(x, seg, rev):
    t, w = x.shape
    ntile = t // SUBLANES
    per_seg = seg // SUBLANES
    x = x.reshape(ntile, SUBLANES, w)
    pos = lax.broadcasted_iota(jnp.int32, (1, SUBLANES, w), 1)
    s = 1
    while s < SUBLANES:
        shift = (SUBLANES - s) if rev else s
        valid = (pos < SUBLANES - s) if rev else (pos >= s)
        x = x + jnp.where(valid, pltpu.roll(x, shift, 1), 0.0)
        s *= 2
    tiles = [None] * ntile
    for s0 in range(0, ntile, per_seg):
        carry = None
        for j in (range(s0 + per_seg - 1, s0 - 1, -1) if rev else range(s0, s0 + per_seg)):
            tiles[j] = x[j] if carry is None else x[j] + carry
            carry = tiles[j][0:1, :] if rev else tiles[j][SUBLANES - 1:SUBLANES, :]
    return jnp.stack(tiles, axis=0).reshape(t, w)


def _ssd_intra(gmat, col, rowv, xhat_bf, tri, rev):
    half = tri.shape[0]
    lo, hi = slice(0, half), slice(half, 2 * half)

    def blk(rs, cs, masked):
        e = col[rs, :] - rowv[:, cs]
        if masked:
            e = jnp.where(tri, e, NEG_BIG)
        return (gmat[rs, cs] * jnp.exp2(e)).astype(BF16)

    if rev:
        out_top = _dot(jnp.concatenate([blk(lo, lo, True), blk(lo, hi, False)], axis=1), xhat_bf)
        yield
        out_bot = _dot(blk(hi, hi, True), xhat_bf[hi, :])
    else:
        out_top = _dot(blk(lo, lo, True), xhat_bf[lo, :])
        yield
        out_bot = _dot(jnp.concatenate([blk(hi, lo, False), blk(hi, hi, True)], axis=1), xhat_bf)
    yield
    return jnp.concatenate([out_top, out_bot], axis=0)


def _ssd_state_slices(h):
    g, hl = divmod(h, SSD_HEADS // SSD_GROUPS)
    return (slice(g * SSD_STATE, (g + 1) * SSD_STATE),
            slice(hl * SSD_HEAD_DIM, (hl + 1) * SSD_HEAD_DIM))


N_DIR_PARAMS = 9
N_DIR_SCRATCH = 6
SCAN_STAGGER = 6


def _scan_kernel(*refs, zero_init, emit_state, n_alias, nchunks):
    it = iter(refs)
    params = [[next(it) for _ in range(N_DIR_PARAMS)] for _ in range(2)]
    dskip_ref = next(it)
    if not zero_init:
        i_lru, i_ssd, i_gla = next(it), next(it), next(it)
    for _ in range(n_alias):
        next(it)
    o_refs = [next(it), next(it)]
    if emit_state:
        s_lru, s_ssd, s_gla = next(it), next(it), next(it)
    scratch = [[next(it) for _ in range(N_DIR_SCRATCH)] for _ in range(2)]

    c = pl.program_id(1)

    @pl.when(c == 0)
    def _():
        for d in range(2):
            h_scr, ssd_scr, gla_scr = scratch[d][:3]
            if zero_init:
                h_scr[...] = jnp.zeros_like(h_scr)
                ssd_scr[...] = jnp.zeros_like(ssd_scr)
                gla_scr[...] = jnp.zeros_like(gla_scr)
            else:
                h_scr[...] = jnp.broadcast_to(i_lru[d], h_scr.shape)
                for h in range(SSD_HEADS):
                    rs, ls = _ssd_state_slices(h)
                    ssd_scr[rs, ls] = i_ssd[d, h]
                gla_scr[...] = i_gla[d]

    tasks, finals = [], []
    for d in range(2):
        t, f = _direction_tasks(d == 1, *params[d], dskip_ref, o_refs[d], *scratch[d])
        tasks.append(t)
        finals.append(f)
    tasks[1] = [(_delayed(gen, SCAN_STAGGER), n) for gen, n in tasks[1]]
    _round_robin([task for pair in zip(*tasks) for task in pair])
    for f in finals:
        f()

    if emit_state:
        @pl.when(c == nchunks - 1)
        def _():
            for d in range(2):
                h_scr, ssd_scr, gla_scr = scratch[d][:3]
                s_lru[d] = h_scr[0:1, :]
                for h in range(SSD_HEADS):
                    rs, ls = _ssd_state_slices(h)
                    s_ssd[d, h] = ssd_scr[rs, ls]
                s_gla[d] = gla_scr[...]


def _direction_tasks(rev, p_ref, lruw_ref, lruba_ref, lrubx_ref, lam_ref, dtb_ref, alog_ref, w2_ref, gb_ref,
                     dskip_ref, o_ref, h_scr, ssd_scr, gla_scr, la_scr, lb_scr, lh_scr):
    tc = T_CHUNK
    small = p_ref[:, SMALL_OFF:SMALL_OFF + LANES]

    c8h = (-0.5 * LRU_C) * _softplus(-lam_ref[...])
    ntile = tc // SUBLANES

    def lru_group(g):
        sl = slice(g * MXU_DIM, (g + 1) * MXU_DIM)
        xl = p_ref[:, sl]
        pre = _dot(xl.astype(BF16), lruw_ref[g])
        yield
        t_r = jnp.tanh(pre[:, :MXU_DIM] + lruba_ref[:, sl])
        yield
        t_i = jnp.tanh(pre[:, MXU_DIM:] + lrubx_ref[:, sl])
        yield
        log_a = c8h[:, sl] * t_r + c8h[:, sl]
        xlh = 0.5 * xl
        ix = xlh * t_i + xlh
        yield
        a = jnp.exp(log_a)
        th = jnp.tanh(log_a)
        yield
        b = jnp.sqrt(-th * (1.0 + a * a)) * ix
        slabs = range(g * MXU_DIM // LANES, (g + 1) * MXU_DIM // LANES)
        for k, slab in enumerate(slabs):
            la_scr[slab] = a[:, k * LANES:(k + 1) * LANES]
            lb_scr[slab] = b[:, k * LANES:(k + 1) * LANES]
        yield
        nrun = tc // LRU_RUN
        order = range(LRU_RUN - 1, -1, -1) if rev else range(LRU_RUN)
        run_a, run_b = {}, {}
        for k, slab in enumerate(slabs):
            pa = pb = None
            for r in order:
                ar = la_scr[slab, pl.ds(r, nrun, stride=LRU_RUN), :]
                br = lb_scr[slab, pl.ds(r, nrun, stride=LRU_RUN), :]
                if pa is not None:
                    br = ar * pb + br
                    ar = ar * pa
                run_a[k, r], run_b[k, r] = pa, pb = ar, br
        yield
        tot_a = jnp.concatenate([run_a[k, order[-1]] for k in range(len(slabs))], axis=1)
        tot_b = jnp.concatenate([run_b[k, order[-1]] for k in range(len(slabs))], axis=1)
        a_cum, b_cum = yield from _tile_scan(tot_a, tot_b, rev)
        carry_in = h_scr[0:1, sl]
        carry = carry_in
        nt = nrun // SUBLANES
        hs = [None] * nt
        for ti in (range(nt - 1, -1, -1) if rev else range(nt)):
            hs[ti] = a_cum[ti] * carry + b_cum[ti]
            carry = hs[ti][0:1, :] if rev else hs[ti][SUBLANES - 1:SUBLANES, :]
        h_scr[:, sl] = jnp.broadcast_to(carry, (SUBLANES, MXU_DIM))
        yield
        h_out = jnp.stack(hs, axis=0).reshape(nrun, MXU_DIM)
        row = lax.broadcasted_iota(jnp.int32, (nrun, MXU_DIM), 0)
        if rev:
            h_in = jnp.where(row == nrun - 1, carry_in, pltpu.roll(h_out, nrun - 1, 0))
        else:
            h_in = jnp.where(row == 0, carry_in, pltpu.roll(h_out, 1, 0))
        for k, slab in enumerate(slabs):
            hk = h_in[:, k * LANES:(k + 1) * LANES]
            for r in order:
                lh_scr[slab, pl.ds(r, nrun, stride=LRU_RUN), :] = run_a[k, r] * hk + run_b[k, r]
        yield
        for k, slab in enumerate(slabs):
            o_ref[:, g * MXU_DIM + k * LANES:g * MXU_DIM + (k + 1) * LANES] = lh_scr[slab].astype(o_ref.dtype)

    lane = lax.broadcasted_iota(jnp.int32, (1, LANES), 1)
    dt_lo = SMALL_DT + (SSD_HEADS if rev else 0)
    dt_mask = (lane >= dt_lo) & (lane < dt_lo + SSD_HEADS)
    dt = jnp.where(dt_mask, _softplus_log(small + dtb_ref[...]), 0.0)
    la = dt * (-LOG2E * jnp.exp(alog_ref[...]))
    bcum = _seg_cumsum(la, tc, rev)
    bcum_t = bcum.T
    tot = bcum[0:1, :] if rev else bcum[tc - 1:tc, :]

    half = tc // 2
    ri = lax.broadcasted_iota(jnp.int32, (half, half), 0)
    ci = lax.broadcasted_iota(jnp.int32, (half, half), 1)
    tri = (ci >= ri) if rev else (ci <= ri)
    lane_t = lax.broadcasted_iota(jnp.int32, (tc, LANES), 1)
    lo_half = lane_t < SSD_HEAD_DIM
    row_s = lax.broadcasted_iota(jnp.int32, (2 * SSD_STATE, SSD_LANES), 0)
    lane_s = lax.broadcasted_iota(jnp.int32, (1, SSD_LANES), 1) // SSD_HEAD_DIM
    b_off = 2 * D_MODEL
    c_off = b_off + SSD_GROUPS * SSD_STATE
    ssd_upd = {}

    def ssd_group(g):
        pair, gl = divmod(g, 2)
        b_tile = p_ref[:, b_off + pair * LANES:b_off + (pair + 1) * LANES].astype(BF16)
        c_tile = p_ref[:, c_off + pair * LANES:c_off + (pair + 1) * LANES]
        s_pair = ssd_scr[pair * LANES:(pair + 1) * LANES, :]
        gmask = lo_half if gl == 0 else jnp.logical_not(lo_half)
        c_g = jnp.where(gmask, c_tile, 0.0).astype(BF16)
        gmat = _dot_nt(c_g, b_tile)
        inter = _dot(c_g, s_pair.astype(BF16))
        yield
        wx_parts = []
        for hp in range(2):
            h0 = g * 4 + hp * 2
            col0 = bcum[:, dt_lo + h0:dt_lo + h0 + 1]
            col1 = bcum[:, dt_lo + h0 + 1:dt_lo + h0 + 2]
            dtc0 = dt[:, dt_lo + h0:dt_lo + h0 + 1]
            dtc1 = dt[:, dt_lo + h0 + 1:dt_lo + h0 + 2]
            xsl = slice(D_MODEL + h0 * SSD_HEAD_DIM, D_MODEL + (h0 + 2) * SSD_HEAD_DIM)
            xs = p_ref[:, xsl]
            xhat = xs * jnp.where(lo_half, dtc0, dtc1)
            xhat_bf = xhat.astype(BF16)
            yield
            outs = []
            for hh, col in ((0, col0), (1, col1)):
                rowv = bcum_t[dt_lo + h0 + hh:dt_lo + h0 + hh + 1, :]
                outs.append((yield from _ssd_intra(gmat, col, rowv, xhat_bf, tri, rev)))
            o_pair = jnp.where(lo_half, outs[0], outs[1])
            colp = jnp.where(lo_half, col0, col1)
            isl = slice(hp * LANES, (hp + 1) * LANES)
            o_pair = o_pair + jnp.exp2(colp) * inter[:, isl]
            if not rev:
                o_pair = o_pair + dskip_ref[:, h0 * SSD_HEAD_DIM:(h0 + 2) * SSD_HEAD_DIM] * xs
            o_ref[:, xsl] = o_pair.astype(o_ref.dtype)
            yield
            t0 = tot[:, dt_lo + h0:dt_lo + h0 + 1]
            t1 = tot[:, dt_lo + h0 + 1:dt_lo + h0 + 2]
            totp = jnp.where(lane < SSD_HEAD_DIM, t0, t1)
            wx_parts.append((jnp.exp2(totp - colp) * xhat).astype(BF16))
        wx = jnp.concatenate(wx_parts, axis=1)
        ssd_upd[g] = _dot_tn(b_tile, wx)
        yield
        d = jnp.zeros((1, SSD_LANES), F32)
        for hl in range(4):
            th = tot[:, dt_lo + g * 4 + hl:dt_lo + g * 4 + hl + 1]
            d = jnp.where(lane_s == hl, jnp.exp2(th), d)
        ssd_upd[("dec", g)] = d
        if gl == 1:
            first = row_s < SSD_STATE
            new = jnp.where(first, ssd_upd[g - 1], ssd_upd[g])
            dec = jnp.where(first, ssd_upd[("dec", g - 1)], d)
            ssd_scr[pair * LANES:(pair + 1) * LANES, :] = dec * s_pair + new

    small_bf = small.astype(BF16)
    gla_bg = {}

    def gla_log_decay(hd):
        ksl = slice(hd * GLA_HEAD_K, (hd + 1) * GLA_HEAD_K)
        z = _dot(small_bf, w2_ref[:, ksl]) + gb_ref[:, ksl]
        yield
        ld = -_softplus_log(-z) * (LOG2E / GLA_GATE_NORM)
        yield
        gla_bg[hd] = _seg_cumsum(ld, GLA_SUB, rev)
        yield

    nsub = tc // GLA_SUB
    nblk = GLA_SUB // GLA_BLK
    scale = GLA_HEAD_K ** -0.5
    gla_st = [gla_scr[hd * GLA_HEAD_K:(hd + 1) * GLA_HEAD_K, :] for hd in range(GLA_HEADS)]

    def gla_subchunk(sc_i, hd):
        ksl = slice(hd * GLA_HEAD_K, (hd + 1) * GLA_HEAD_K)
        st = gla_st[hd]
        rsl = slice(sc_i * GLA_SUB, (sc_i + 1) * GLA_SUB)
        q = p_ref[rsl, Q_OFF + hd * GLA_HEAD_K:Q_OFF + (hd + 1) * GLA_HEAD_K] * scale
        k = p_ref[rsl, K_OFF + hd * GLA_HEAD_K:K_OFF + (hd + 1) * GLA_HEAD_K]
        v = p_ref[rsl, V_OFF + hd * GLA_HEAD_V:V_OFF + (hd + 1) * GLA_HEAD_V].astype(BF16)
        b = gla_bg[hd][rsl, :]
        inter = _dot((q * jnp.exp2(b)).astype(BF16), st.astype(BF16))
        yield
        o_rows = []
        for blk in range(nblk):
            bsl = slice(blk * GLA_BLK, (blk + 1) * GLA_BLK)
            if rev:
                keys = slice(blk * GLA_BLK, GLA_SUB)
                ref_row = b[(blk + 1) * GLA_BLK - 1:(blk + 1) * GLA_BLK, :]
            else:
                keys = slice(0, (blk + 1) * GLA_BLK)
                ref_row = b[blk * GLA_BLK:blk * GLA_BLK + 1, :]
            nk = keys.stop - keys.start
            qt = (q[bsl, :] * jnp.exp2(b[bsl, :] - ref_row)).astype(BF16)
            kt = (k[keys, :] * jnp.exp2(ref_row - b[keys, :])).astype(BF16)
            a_blk = _dot_nt(qt, kt)
            yield
            ai = lax.broadcasted_iota(jnp.int32, (GLA_BLK, nk), 0)
            aj = lax.broadcasted_iota(jnp.int32, (GLA_BLK, nk), 1)
            vis = (aj >= ai) if rev else (aj <= ai + blk * GLA_BLK)
            a_blk = jnp.where(vis, a_blk, 0.0).astype(BF16)
            o_rows.append(inter[bsl, :] + _dot(a_blk, v[keys, :]))
            yield
        o = jnp.concatenate(o_rows, axis=0)
        o_ref[rsl, 2 * D_MODEL + hd * GLA_HEAD_V:2 * D_MODEL + (hd + 1) * GLA_HEAD_V] = o.astype(o_ref.dtype)
        end = b[0:1, :] if rev else b[GLA_SUB - 1:GLA_SUB, :]
        k_dec = (k * jnp.exp2(end - b)).astype(BF16)
        dec_col = jnp.broadcast_to(jnp.exp2(end), (GLA_HEAD_K, GLA_HEAD_K)).T[:, 0:1]
        gla_st[hd] = dec_col * st + _dot_tn(k_dec, v)
        yield

    def gla_head(hd):
        yield from gla_log_decay(hd)
        for step in range(nsub):
            yield from gla_subchunk((nsub - 1 - step) if rev else step, hd)

    def lru_all():
        for g in range(D_MODEL // MXU_DIM):
            yield from lru_group(g)

    def ssd_all():
        for g in range(SSD_GROUPS):
            yield from ssd_group(g)

    tasks = [(gla_head(hd), 1) for hd in range(GLA_HEADS)] + [(ssd_all(), 2), (lru_all(), 1)]

    def finalize():
        for hd in range(GLA_HEADS):
            gla_scr[hd * GLA_HEAD_K:(hd + 1) * GLA_HEAD_K, :] = gla_st[hd]

    return tasks, finalize


def _scan(p1, tok0, nseq, nchunks, lw, l, init, o_prev, st_prev, st_shapes):
    tc = T_CHUNK
    blk0 = tok0 // tc
    m = p1.shape[0]

    def row_block(d):
        return lambda b, c: (blk0 + b * nchunks + ((nchunks - 1 - c) if d else c), 0)

    const2 = lambda b, c: (0, 0)
    const3 = lambda b, c: (0, 0, 0)
    in_specs, args = [], []
    for d in range(2):
        in_specs += [
            pl.BlockSpec((tc, P1_COLS), row_block(d)),
            pl.BlockSpec((D_MODEL // MXU_DIM, MXU_DIM, 2 * MXU_DIM), const3),
            pl.BlockSpec((1, D_MODEL), const2),
            pl.BlockSpec((1, D_MODEL), const2),
            pl.BlockSpec((1, D_MODEL), const2),
            pl.BlockSpec((1, LANES), const2),
            pl.BlockSpec((1, LANES), const2),
            pl.BlockSpec((LANES, GLA_DK), const2),
            pl.BlockSpec((1, GLA_DK), const2),
        ]
        args += [p1, lw["lru_w"][d], lw["lru_ba"][d], lw["lru_bx"][d], lw["lam"][d], lw["dt_bias"][d],
                 lw["a_log"][d], lw["gate_w2"][d], lw["gate_b"][d]]
        assert len(in_specs) == (d + 1) * N_DIR_PARAMS
    in_specs.append(pl.BlockSpec((1, D_MODEL), const2))
    args.append(lw["d_skip"])
    state_specs = [
        pl.BlockSpec((None, None, 2, 1, D_MODEL), lambda b, c: (b, l, 0, 0, 0)),
        pl.BlockSpec((None, None, 2, SSD_HEADS, SSD_STATE, SSD_HEAD_DIM), lambda b, c: (b, l, 0, 0, 0, 0)),
        pl.BlockSpec((None, None, 2, GLA_ROWS, GLA_HEAD_V), lambda b, c: (b, l, 0, 0, 0)),
    ]
    if init is not None:
        in_specs += state_specs
        args += list(init)
    aliases = {}
    prev = list(o_prev or []) + list(st_prev or [])
    first_out = 0 if o_prev is not None else 2
    for k, arr in enumerate(prev):
        aliases[len(args)] = first_out + k
        in_specs.append(pl.BlockSpec(memory_space=pl.ANY))
        args.append(arr)
    out_specs = [pl.BlockSpec((tc, 3 * D_MODEL), row_block(d)) for d in range(2)]
    out_shape = [jax.ShapeDtypeStruct((m, 3 * D_MODEL), BF16)] * 2
    if st_shapes is not None:
        out_specs += state_specs
        out_shape += [jax.ShapeDtypeStruct(s, F32) for s in st_shapes]
    kern = functools.partial(_scan_kernel, zero_init=init is None, emit_state=st_shapes is not None,
                             n_alias=len(prev), nchunks=nchunks)
    dir_scratch = [
        pltpu.VMEM((SUBLANES, D_MODEL), F32),
        pltpu.VMEM((SSD_ROWS, SSD_LANES), F32),
        pltpu.VMEM((GLA_ROWS, GLA_HEAD_V), F32),
        pltpu.VMEM((D_MODEL // LANES, tc, LANES), F32),
        pltpu.VMEM((D_MODEL // LANES, tc, LANES), F32),
        pltpu.VMEM((D_MODEL // LANES, tc, LANES), F32),
    ]
    assert len(dir_scratch) == N_DIR_SCRATCH
    return pl.pallas_call(
        kern,
        grid=(nseq, nchunks),
        in_specs=in_specs,
        out_specs=out_specs,
        out_shape=out_shape,
        input_output_aliases=aliases,
        scratch_shapes=dir_scratch * 2,
        compiler_params=_params(("parallel", "arbitrary")),
        name="scan",
    )(*args)


def _post_kernel(x_ref, of_ref, ob_ref, p2_ref, mod_ref, ssdg_ref, glag_ref,
                 wl_ref, ws_ref, wg_ref, wo_ref, o_ref):
    dm = D_MODEL

    def both(sl):
        return of_ref[:, sl].astype(F32) + ob_ref[:, sl].astype(F32)

    def p2(k):
        return p2_ref[:, k * dm:(k + 1) * dm].astype(F32)

    y_lru = both(slice(0, dm)) * _gelu_tanh(p2(0))
    y_ssd = _rms(both(slice(dm, 2 * dm)) * _silu(p2(1)), ssdg_ref[...])
    parts = []
    for hd in range(GLA_HEADS):
        sl = slice(2 * dm + hd * GLA_HEAD_V, 2 * dm + (hd + 1) * GLA_HEAD_V)
        parts.append(_rms(both(sl), glag_ref[...]))
    y_gla = jnp.concatenate(parts, axis=1) * _silu(p2(2))
    merged = (_sigmoid(p2(3)) * _dot(y_lru.astype(BF16), wl_ref[...])
              + _sigmoid(p2(4)) * _dot(y_ssd.astype(BF16), ws_ref[...])
              + _sigmoid(p2(5)) * _dot(y_gla.astype(BF16), wg_ref[...]))
    out = _dot(merged.astype(BF16), wo_ref[...])
    o_ref[...] = x_ref[...] + mod_ref[:, 2 * dm:3 * dm] * out


def _post(x, o_f, o_b, p2, mod, row_of_block, lw, l):
    m = x.shape[0]
    tm = TM_POST
    wspec = pl.BlockSpec((None, D_MODEL, D_MODEL), lambda i: (l, 0, 0), pipeline_mode=pl.Buffered(1))
    return pl.pallas_call(
        _post_kernel,
        grid=(m // tm,),
        in_specs=[
            pl.BlockSpec((tm, D_MODEL), lambda i: (i, 0)),
            pl.BlockSpec((tm, 3 * D_MODEL), lambda i: (i, 0)),
            pl.BlockSpec((tm, 3 * D_MODEL), lambda i: (i, 0)),
            pl.BlockSpec((tm, P2_COLS), lambda i: (i, 0)),
            pl.BlockSpec((None, 1, 6 * D_MODEL), lambda i: (row_of_block(i, tm), 0, 0)),
            pl.BlockSpec((1, D_MODEL), lambda i: (0, 0)),
            pl.BlockSpec((1, GLA_HEAD_V), lambda i: (0, 0)),
            wspec, wspec, wspec, wspec,
        ],
        out_specs=pl.BlockSpec((tm, D_MODEL), lambda i: (i, 0)),
        out_shape=jax.ShapeDtypeStruct((m, D_MODEL), F32),
        compiler_params=_params(("parallel",)),
        name="post",
    )(x, o_f, o_b, p2, mod, lw["ssd_norm_g"], lw["gla_norm_g"],
      lw["w_br_lru"], lw["w_br_ssd"], lw["w_br_gla"], lw["w_out"])


def _ffn_kernel(x_ref, mod_ref, g_ref, wi_ref, wo_ref, fg_ref, o_ref, *, final_norm, nsplit):
    dm = D_MODEL
    x = x_ref[...]
    h = (_rms(x, g_ref[...]) * (1.0 + mod_ref[:, 4 * dm:5 * dm]) + mod_ref[:, 3 * dm:4 * dm]).astype(BF16)
    tf = D_FF // nsplit
    acc = jnp.zeros(x.shape, F32)
    for s in range(nsplit):
        fs = slice(s * tf, (s + 1) * tf)
        gate = _dot(h, wi_ref[:, fs])
        up = _dot(h, wi_ref[:, D_FF + s * tf:D_FF + (s + 1) * tf])
        acc = acc + _dot((_silu(gate) * up).astype(BF16), wo_ref[fs, :])
    y = x + mod_ref[:, 5 * dm:6 * dm] * acc
    if final_norm:
        y = _rms(y, fg_ref[...])
    o_ref[...] = y


def _ffn(x, mod, row_of_block, lw, l, final_g, final_norm):
    m = x.shape[0]
    tm = TM_FFN
    single = dict(pipeline_mode=pl.Buffered(1))
    kern = functools.partial(_ffn_kernel, final_norm=final_norm, nsplit=11)
    return pl.pallas_call(
        kern,
        grid=(m // tm,),
        in_specs=[
            pl.BlockSpec((tm, D_MODEL), lambda i: (i, 0)),
            pl.BlockSpec((None, 1, 6 * D_MODEL), lambda i: (row_of_block(i, tm), 0, 0)),
            pl.BlockSpec((1, D_MODEL), lambda i: (0, 0)),
            pl.BlockSpec((None, D_MODEL, 2 * D_FF), lambda i: (l, 0, 0), **single),
            pl.BlockSpec((None, D_FF, D_MODEL), lambda i: (l, 0, 0), **single),
            pl.BlockSpec((1, D_MODEL), lambda i: (0, 0)),
        ],
        out_specs=pl.BlockSpec((tm, D_MODEL), lambda i: (i, 0)),
        out_shape=jax.ShapeDtypeStruct((m, D_MODEL), F32),
        compiler_params=_params(("parallel",)),
        name="ffn",
    )(x, mod, lw["norm_ffn_g"], lw["w_ffn_in"], lw["w_ffn_out"], final_g)


def _snake_rows(src_ref, dst_ref):
    rows = src_ref.shape[0]
    for g0 in range(0, rows, 2 * GRID_W):
        dst_ref[g0:g0 + GRID_W, :] = src_ref[g0:g0 + GRID_W, :]
        for r in range(GRID_W):
            dst_ref[pl.ds(g0 + GRID_W + r, 1), :] = src_ref[pl.ds(g0 + 2 * GRID_W - 1 - r, 1), :]


def _assemble_kernel(xa_ref, xb_ref, o_ref, *, nb_ctx):
    i = pl.program_id(0)

    @pl.when(i < nb_ctx)
    def _():
        o_ref[...] = xa_ref[...]

    @pl.when(i >= nb_ctx)
    def _():
        _snake_rows(xb_ref, o_ref)


def _split_kernel(y_ref, ya_ref, yb_ref, *, nb_ctx):
    i = pl.program_id(0)

    @pl.when(i < nb_ctx)
    def _():
        ya_ref[...] = y_ref[...]

    @pl.when(i >= nb_ctx)
    def _():
        _snake_rows(y_ref, yb_ref)


def _assemble_tokens(x_ctx, x_lat):
    tr = TOKEN_ROWS
    nb_ctx, nb_lat = x_ctx.shape[0] // tr, x_lat.shape[0] // tr
    return pl.pallas_call(
        functools.partial(_assemble_kernel, nb_ctx=nb_ctx),
        grid=(nb_ctx + nb_lat,),
        in_specs=[pl.BlockSpec((tr, D_MODEL), lambda i: (jnp.minimum(i, nb_ctx - 1), 0)),
                  pl.BlockSpec((tr, D_MODEL), lambda i: (jnp.maximum(i - nb_ctx, 0), 0))],
        out_specs=pl.BlockSpec((tr, D_MODEL), lambda i: (i, 0)),
        out_shape=jax.ShapeDtypeStruct((x_ctx.shape[0] + x_lat.shape[0], D_MODEL), x_ctx.dtype),
        compiler_params=_params(("arbitrary",)),
        name="assemble_tokens",
    )(x_ctx, x_lat)


def _split_tokens(y, m_ctx):
    tr = TOKEN_ROWS
    nb_ctx, nb_lat = m_ctx // tr, (y.shape[0] - m_ctx) // tr
    return pl.pallas_call(
        functools.partial(_split_kernel, nb_ctx=nb_ctx),
        grid=(nb_ctx + nb_lat,),
        in_specs=[pl.BlockSpec((tr, D_MODEL), lambda i: (i, 0))],
        out_specs=[pl.BlockSpec((tr, D_MODEL), lambda i: (jnp.minimum(i, nb_ctx - 1), 0)),
                   pl.BlockSpec((tr, D_MODEL), lambda i: (jnp.maximum(i - nb_ctx, 0), 0))],
        out_shape=[jax.ShapeDtypeStruct((m_ctx, D_MODEL), y.dtype),
                   jax.ShapeDtypeStruct((y.shape[0] - m_ctx, D_MODEL), y.dtype)],
        compiler_params=_params(("arbitrary",)),
        name="split_tokens",
    )(y)


_O_GATE, _O_Z, _O_XBC = D_MODEL, 2 * D_MODEL, 3 * D_MODEL
_O_DT = _O_XBC + CONV_COLS - D_MODEL
_O_Q = _O_DT + 2 * SSD_HEADS
_O_G = _O_Q + 2 * GLA_DK + D_MODEL
_O_LR = _O_G + D_MODEL
_O_M = _O_LR + 2 * GLA_GATE_RANK
IN_WIDTH = _O_M + 3 * D_MODEL
W_PREP_ROWS = 128


def _wprep_kernel(w_ref, w1_ref, w2_ref):
    def cols(lo, hi):
        return w_ref[:, lo:hi].astype(BF16)

    w1_ref[:, 0:D_MODEL] = cols(0, D_MODEL)
    w1_ref[:, D_MODEL:CONV_COLS] = cols(_O_XBC, _O_DT)
    w1_ref[:, Q_OFF:SMALL_OFF] = cols(_O_Q, _O_G)
    assert _O_DT % LANES == SMALL_DT and _O_LR % LANES == SMALL_LR
    lane = lax.broadcasted_iota(jnp.int32, (w_ref.shape[0], LANES), 1)
    dt_tile = w_ref[:, _O_DT:_O_DT + LANES]
    lr_tile = w_ref[:, _O_LR - SMALL_LR:_O_LR - SMALL_LR + LANES]
    small = jnp.where(lane < SMALL_LR, dt_tile, jnp.where(lane < 2 * SMALL_LR, lr_tile, 0.0))
    w1_ref[:, SMALL_OFF:] = small.astype(BF16)
    w2_ref[:, 0:2 * D_MODEL] = cols(_O_GATE, _O_XBC)
    w2_ref[:, 2 * D_MODEL:3 * D_MODEL] = cols(_O_G, _O_LR)
    w2_ref[:, 3 * D_MODEL:] = cols(_O_M, IN_WIDTH)


def _prep_w_in(w_in):
    n_layers = w_in.shape[0]
    tr = W_PREP_ROWS
    return pl.pallas_call(
        _wprep_kernel,
        grid=(n_layers, D_MODEL // tr),
        in_specs=[pl.BlockSpec((None, tr, IN_WIDTH), lambda l, r: (l, r, 0))],
        out_specs=[pl.BlockSpec((None, tr, P1_COLS), lambda l, r: (l, r, 0)),
                   pl.BlockSpec((None, tr, P2_COLS), lambda l, r: (l, r, 0))],
        out_shape=[jax.ShapeDtypeStruct((n_layers, D_MODEL, P1_COLS), BF16),
                   jax.ShapeDtypeStruct((n_layers, D_MODEL, P2_COLS), BF16)],
        compiler_params=_params(("parallel", "parallel")),
        name="w_in_relayout",
    )(w_in)


def _layer_weights(l, w):
    dm = D_MODEL

    def block_diag_tiles(wa, wx):
        per = MXU_DIM // LRU_BLOCK_W
        eye = jnp.eye(per, dtype=wa.dtype)
        def tiles(wb):
            wb = wb.reshape(dm // MXU_DIM, per, LRU_BLOCK_W, LRU_BLOCK_W)
            t = jnp.einsum("gpwv,pq->gpwqv", wb, eye)
            return t.reshape(dm // MXU_DIM, MXU_DIM, MXU_DIM)
        return jnp.concatenate([tiles(wa), tiles(wx)], axis=2).astype(BF16)

    def lane_pad(v, lo):
        return jnp.zeros((1, LANES), F32).at[0, lo:lo + v.shape[0]].set(v)

    def gate_w2(d):
        lo = SMALL_LR + d * GLA_GATE_RANK
        return jnp.zeros((LANES, GLA_DK), F32).at[lo:lo + GLA_GATE_RANK].set(w["gla_gate_w2"][l, d]).astype(BF16)

    return dict(
        w1=w["w1"], w2=w["w2"],
        norm_mix_g=w["norm_mix_g"][l][None], norm_ffn_g=w["norm_ffn_g"][l][None],
        conv_w=jnp.concatenate([w["lru_conv_w"][l], w["ssd_conv_w"][l]], axis=1),
        conv_b=jnp.concatenate([w["lru_conv_b"][l], w["ssd_conv_b"][l]])[None],
        lru_w=[block_diag_tiles(0.5 * w["lru_w_a"][l, d], 0.5 * w["lru_w_x"][l, d]) for d in range(2)],
        lru_ba=[0.5 * w["lru_b_a"][l, d][None] for d in range(2)],
        lru_bx=[0.5 * w["lru_b_x"][l, d][None] for d in range(2)],
        lam=[w["lru_lambda"][l, d][None] for d in range(2)],
        dt_bias=[lane_pad(w["ssd_dt_bias"][l, d], SMALL_DT + d * SSD_HEADS) for d in range(2)],
        a_log=[lane_pad(w["ssd_a_log"][l, d], SMALL_DT + d * SSD_HEADS) for d in range(2)],
        d_skip=jnp.repeat(w["ssd_d"][l], SSD_HEAD_DIM)[None],
        gate_w2=[gate_w2(d) for d in range(2)],
        gate_b=[w["gla_gate_b"][l, d][None] for d in range(2)],
        ssd_norm_g=w["ssd_norm_g"][l][None], gla_norm_g=w["gla_norm_g"][l][None],
        w_br_lru=w["w_br_lru_bf"], w_br_ssd=w["w_br_ssd_bf"], w_br_gla=w["w_br_gla_bf"], w_out=w["w_out_bf"],
        w_ffn_in=w["w_ffn_in_bf"], w_ffn_out=w["w_ffn_out_bf"],
    )


def kernel(x_prompt, x_sample, state_lru, state_ssd, state_gla, c, c_ctx, norm_mix_g, norm_ffn_g, w_ada, b_ada, w_in, lru_conv_w, lru_conv_b, lru_w_a, lru_b_a, lru_w_x, lru_b_x, lru_lambda, ssd_conv_w, ssd_conv_b, ssd_dt_bias, ssd_a_log, ssd_d, ssd_norm_g, gla_gate_w2, gla_gate_b, gla_norm_g, w_br_lru, w_br_ssd, w_br_gla, w_out, w_ffn_in, w_ffn_out, final_norm_g):
    w = dict(norm_mix_g=norm_mix_g, norm_ffn_g=norm_ffn_g, w_in=w_in, lru_conv_w=lru_conv_w,
             lru_conv_b=lru_conv_b, lru_w_a=lru_w_a, lru_b_a=lru_b_a, lru_w_x=lru_w_x, lru_b_x=lru_b_x,
             lru_lambda=lru_lambda, ssd_conv_w=ssd_conv_w, ssd_conv_b=ssd_conv_b, ssd_dt_bias=ssd_dt_bias,
             ssd_a_log=ssd_a_log, ssd_d=ssd_d, ssd_norm_g=ssd_norm_g, gla_gate_w2=gla_gate_w2,
             gla_gate_b=gla_gate_b, gla_norm_g=gla_norm_g, w_br_lru=w_br_lru, w_br_ssd=w_br_ssd,
             w_br_gla=w_br_gla, w_out=w_out, w_ffn_in=w_ffn_in, w_ffn_out=w_ffn_out)
    w["w1"], w["w2"] = _prep_w_in(w_in)
    for name in ("w_br_lru", "w_br_ssd", "w_br_gla", "w_out", "w_ffn_in", "w_ffn_out"):
        w[name + "_bf"] = w[name].astype(BF16)
    n_layers = w_in.shape[0]
    b_ctx, t_ctx, dm = x_prompt.shape
    b_lat, t_lat, _ = x_sample.shape
    m_ctx = b_ctx * t_ctx
    m_lat = b_lat * t_lat
    nch_ctx = t_ctx // T_CHUNK
    nch_lat = t_lat // T_CHUNK

    n_rows = -(-(1 + b_lat) // SUBLANES) * SUBLANES
    cond = jnp.zeros((n_rows, dm), F32).at[0].set(c_ctx).at[1:1 + b_lat].set(c)
    mod = _modulation(cond, w_ada, b_ada)

    def row_of_block(i, tm):
        tok = i * tm
        return jnp.where(tok < m_ctx, 0, 1 + (tok - m_ctx) // t_lat)

    def seq_pos(tok):
        in_ctx = tok < m_ctx
        pos = jnp.where(in_ctx, lax.rem(tok, t_ctx), lax.rem(tok - m_ctx, t_lat))
        return pos, jnp.where(in_ctx, t_ctx, t_lat)

    x = _assemble_tokens(x_prompt.reshape(m_ctx, dm), x_sample.reshape(m_lat, dm))

    st_shapes = [(b_ctx, n_layers, 2, 1, dm),
                 (b_ctx, n_layers, 2, SSD_HEADS, SSD_STATE, SSD_HEAD_DIM),
                 (b_ctx, n_layers, 2, GLA_ROWS, GLA_HEAD_V)]
    init = (state_lru.reshape(b_lat, n_layers, 2, 1, dm), state_ssd,
            state_gla.reshape(b_lat, n_layers, 2, GLA_ROWS, GLA_HEAD_V))
    states = None
    for l in range(n_layers):
        lw = _layer_weights(l, w)
        mod_l = mod[l][:, None, :]
        p1, p2 = _inproj(x, mod_l, row_of_block, seq_pos, lw, l)
        o_f, o_b, *states = _scan(p1, 0, b_ctx, nch_ctx, lw, l, None, None, states, st_shapes)
        outs = _scan(p1, m_ctx, b_lat, nch_lat, lw, l, init, [o_f, o_b], None, None)
        x = _post(x, outs[0], outs[1], p2, mod_l, row_of_block, lw, l)
        x = _ffn(x, mod_l, row_of_block, lw, l, final_norm_g[None], l == n_layers - 1)

    y_ctx, y_lat = _split_tokens(x, m_ctx)
    y_prompt = y_ctx.reshape(b_ctx, t_ctx, dm)
    y_sample = y_lat.reshape(b_lat, t_lat, dm)
    return (y_prompt, y_sample, states[0].reshape(b_ctx, n_layers, 2, dm), states[1],
            states[2].reshape(b_ctx, n_layers, 2, GLA_HEADS, GLA_HEAD_K, GLA_HEAD_V))
```
